```python
import math
import jax, jax.numpy as jnp
from jax import lax
import numpy as np

D_MODEL = 4096
BATCH = 1
SEQ = 8192
DEPTH = 1

HEAD_DIM = 128
D_MIX = D_MODEL
DIFF_WIDTH = D_MIX // 2
DIL_WIDTH = D_MIX - DIFF_WIDTH
DIFF_HEADS = DIFF_WIDTH // (2 * HEAD_DIM)
DIL_HEADS = DIL_WIDTH // HEAD_DIM
IN_COLS = 3 * DIFF_WIDTH + 3 * DIL_WIDTH
ROT_DIM = HEAD_DIM // 4
ROPE_THETA = 500000.0
DILATED_CONFIGS = ((128, 1), (512, 4), (2048, 16))
BLK = 128
PEER_HEADS = 8
N_KEYS = 128
N_EXPERTS = N_KEYS * N_KEYS
PEER_QDIM = 256
PEER_TOPK = 16
PEER_BLK = 32
NORM_EPS = 1e-6
NEG_INF = -1e30

kernel_name = "hymba_diffattn_longnet_peer_adaln"


def rmsnorm(x, g):
    xf = x.astype(jnp.float32)
    y = xf * lax.rsqrt(jnp.mean(xf * xf, axis=-1, keepdims=True) + NORM_EPS)
    return (y * g.astype(jnp.float32)).astype(x.dtype)


def modulate(h, shift, scale):
    return h * (1 + scale[:, None, :]) + shift[:, None, :]


def rotary_tables(positions):
    inv_freq = jnp.power(ROPE_THETA, -jnp.arange(0, ROT_DIM, 2, dtype=jnp.float32) / ROT_DIM)
    ang = positions.astype(jnp.float32)[..., None] * inv_freq
    return jnp.cos(ang)[:, :, None, :], jnp.sin(ang)[:, :, None, :]


def apply_partial_rotary(t, cos, sin):
    half = ROT_DIM // 2
    t1 = t[..., :half].astype(jnp.float32)
    t2 = t[..., half:ROT_DIM].astype(jnp.float32)
    rot = jnp.concatenate([t1 * cos - t2 * sin, t2 * cos + t1 * sin], axis=-1).astype(t.dtype)
    return jnp.concatenate([rot, t[..., ROT_DIM:]], axis=-1)


def diff_attention(q1, q2, k1, k2, v, lam, lam_init, subln_g):
    B, S, H, dh = q1.shape
    nqb = S // BLK
    scale = dh ** -0.5
    kpos = jnp.arange(S)

    def qblock(args):
        b, qb1, qb2 = args
        qpos = b * BLK + jnp.arange(BLK)
        mask = kpos[None, :] <= qpos[:, None]

        def probs(qb, k):
            s = jnp.einsum('bqhd,bkhd->bhqk', qb, k, preferred_element_type=jnp.float32) * scale
            return jax.nn.softmax(jnp.where(mask, s, NEG_INF), axis=-1)

        a = probs(qb1, k1) - lam * probs(qb2, k2)
        return jnp.einsum('bhqk,bkhe->bqhe', a.astype(v.dtype), v)

    blocks = lambda t: t.reshape(B, nqb, BLK, H, t.shape[-1]).transpose(1, 0, 2, 3, 4)
    out = lax.map(qblock, (jnp.arange(nqb), blocks(q1), blocks(q2)))
    out = out.transpose(1, 0, 2, 3, 4).reshape(B, S, H, 2 * dh)
    return rmsnorm(out, subln_g) * (1 - lam_init)


def dilated_branch(q, k, v, window, dilation):
    B, S, H, dh = q.shape
    reach = window // dilation
    L = S // dilation
    nb = -(-L // BLK)
    Lp = nb * BLK

    def to_sub(t):
        t = t.reshape(B, L, dilation, H, dh).transpose(0, 2, 1, 3, 4)
        return jnp.pad(t, ((0, 0), (0, 0), (0, Lp - L), (0, 0), (0, 0)))

    blk = lambda u: u.reshape(B, dilation, nb, BLK, H, dh)

    def with_prev(t):
        prev = jnp.pad(t, ((0, 0), (0, 0), (BLK, 0), (0, 0), (0, 0)))[:, :, :Lp]
        return jnp.concatenate([blk(prev), blk(t)], axis=3)

    qs, ks, vs = to_sub(q), to_sub(k), to_sub(v)
    qb, kb, vb = blk(qs), with_prev(ks), with_prev(vs)
    qi = jnp.arange(BLK)[:, None]
    kj = jnp.arange(2 * BLK)[None, :]
    dist = qi + BLK - kj
    band = (dist >= 0) & (dist <= reach)
    has_prev = (jnp.arange(nb) > 0)[:, None, None]
    mask = band[None] & (has_prev | (kj >= BLK)[None])
    s = jnp.einsum('brnqhd,brnkhd->brnhqk', qb, kb, preferred_element_type=jnp.float32) * (dh ** -0.5)
    s = jnp.where(mask[:, None], s, NEG_INF)
    m = jnp.max(s, axis=-1, keepdims=True)
    p = jnp.exp(s - m)
    l = jnp.sum(p, axis=-1, keepdims=True)
    o = jnp.einsum('brnhqk,brnkhd->brnqhd', (p / l).astype(v.dtype), vb)
    lse = (m + jnp.log(l))[..., 0]
    o = o.reshape(B, dilation, Lp, H, dh)[:, :, :L].transpose(0, 2, 1, 3, 4).reshape(B, S, H, dh)
    lse = lse.transpose(0, 1, 2, 4, 3).reshape(B, dilation, Lp, H)[:, :, :L]
    lse = lse.transpose(0, 2, 1, 3).reshape(B, S, H)
    return o, lse


def dilated_attention(q, k, v):
    outs, lses = zip(*[dilated_branch(q, k, v, w, d) for (w, d) in DILATED_CONFIGS])
    wts = jax.nn.softmax(jnp.stack(lses, axis=0), axis=0)
    return jnp.einsum('gbsh,gbshd->bshd', wts.astype(q.dtype), jnp.stack(outs, axis=0))


def hybrid_mixer(h, cos, sin, w_in, w_out, lam_q1, lam_k1, lam_q2, lam_k2,
                 diff_subln_g, dil_out_g, lam_init):
    B, S, _ = h.shape
    proj = jnp.einsum('bsd,de->bse', h, w_in)
    cuts = [DIFF_WIDTH, 2 * DIFF_WIDTH, 3 * DIFF_WIDTH,
            3 * DIFF_WIDTH + DIL_WIDTH, 3 * DIFF_WIDTH + 2 * DIL_WIDTH]
    dq, dk, dv, sq, sk, sv = jnp.split(proj, cuts, axis=-1)
    rope = lambda t: apply_partial_rotary(t, cos, sin)
    dq = dq.reshape(B, S, DIFF_HEADS, 2, HEAD_DIM)
    dk = dk.reshape(B, S, DIFF_HEADS, 2, HEAD_DIM)
    q1, q2 = rope(dq[..., 0, :]), rope(dq[..., 1, :])
    k1, k2 = rope(dk[..., 0, :]), rope(dk[..., 1, :])
    dv = dv.reshape(B, S, DIFF_HEADS, 2 * HEAD_DIM)
    f32 = jnp.float32
    lam = (jnp.exp(jnp.sum(lam_q1.astype(f32) * lam_k1.astype(f32)))
           - jnp.exp(jnp.sum(lam_q2.astype(f32) * lam_k2.astype(f32))) + lam_init)
    y_diff = diff_attention(q1, q2, k1, k2, dv, lam, lam_init, diff_subln_g).reshape(B, S, DIFF_WIDTH)
    sq = rope(sq.reshape(B, S, DIL_HEADS, HEAD_DIM))
    sk = rope(sk.reshape(B, S, DIL_HEADS, HEAD_DIM))
    sv = sv.reshape(B, S, DIL_HEADS, HEAD_DIM)
    y_dil = rmsnorm(dilated_attention(sq, sk, sv), dil_out_g.reshape(DIL_HEADS, HEAD_DIM))
    y_dil = y_dil.reshape(B, S, DIL_WIDTH)
    return jnp.einsum('bse,ed->bsd', jnp.concatenate([y_diff, y_dil], axis=-1), w_out)


def peer_ffn(h, w_q, subkeys, u, v):
    B, S, D = h.shape
    q = jnp.einsum('bsd,de->bse', h, w_q).reshape(B, S, PEER_HEADS, 2, PEER_QDIM // 2)
    sc = jnp.einsum('bshpk,hpnk->bshpn', q, subkeys, preferred_element_type=jnp.float32)
    top_v, top_i = lax.top_k(sc, PEER_TOPK)
    cand = (top_v[..., 0, :, None] + top_v[..., 1, None, :]).reshape(B, S, PEER_HEADS, PEER_TOPK ** 2)
    cand_id = (top_i[..., 0, :, None] * N_KEYS + top_i[..., 1, None, :]).reshape(B, S, PEER_HEADS, PEER_TOPK ** 2)
    fin_v, fin_pos = lax.top_k(cand, PEER_TOPK)
    expert_id = jnp.take_along_axis(cand_id, fin_pos, axis=-1)
    gate = jax.nn.softmax(fin_v, axis=-1)
    T = B * S
    nblk = T // PEER_BLK
    E = PEER_HEADS * PEER_TOPK
    hb = h.reshape(nblk, PEER_BLK, D)
    ib = expert_id.reshape(nblk, PEER_BLK, E)
    gb = gate.reshape(nblk, PEER_BLK, E)

    def block(args):
        ht, it, gt = args
        ue = jnp.take(u, it, axis=0)
        act = jax.nn.gelu(jnp.einsum('td,ted->te', ht, ue, preferred_element_type=jnp.float32),
                          approximate=False)
        ve = jnp.take(v, it, axis=0)
        return jnp.einsum('te,ted->td', (gt * act).astype(v.dtype), ve)

    return lax.map(block, (hb, ib, gb)).reshape(B, S, D)


def setup_inputs(seed: int = 0) -> dict:
    key = jax.random.key(seed)
    ks = jax.random.split(key, 20)
    nrm = lambda k, shape, s: jax.random.normal(k, shape, jnp.float32) * s
    return {
        "x": nrm(ks[0], (BATCH, SEQ, D_MODEL), 1.0),
        "c": nrm(ks[1], (BATCH, D_MODEL), 1.0),
        "positions": jnp.broadcast_to(jnp.arange(SEQ, dtype=jnp.int32), (BATCH, SEQ)),
        "norm1_g": 1.0 + nrm(ks[2], (DEPTH, D_MODEL), 0.02),
        "norm2_g": 1.0 + nrm(ks[3], (DEPTH, D_MODEL), 0.02),
        "w_ada": nrm(ks[4], (DEPTH, D_MODEL, 6 * D_MODEL), 0.5 * D_MODEL ** -0.5),
        "b_ada": nrm(ks[5], (DEPTH, 6 * D_MODEL), 0.01),
        "w_in": nrm(ks[6], (DEPTH, D_MODEL, IN_COLS), D_MODEL ** -0.5),
        "lam_q1": nrm(ks[7], (DEPTH, HEAD_DIM), 0.1),
        "lam_k1": nrm(ks[8], (DEPTH, HEAD_DIM), 0.1),
        "lam_q2": nrm(ks[9], (DEPTH, HEAD_DIM), 0.1),
        "lam_k2": nrm(ks[10], (DEPTH, HEAD_DIM), 0.1),
        "diff_subln_g": 1.0 + nrm(ks[11], (DEPTH, 2 * HEAD_DIM), 0.02),
        "dil_out_g": 1.0 + nrm(ks[12], (DEPTH, DIL_WIDTH), 0.02),
        "w_out": nrm(ks[13], (DEPTH, D_MIX, D_MODEL), D_MIX ** -0.5),
        "peer_wq": nrm(ks[14], (DEPTH, D_MODEL, PEER_HEADS * PEER_QDIM), D_MODEL ** -0.5),
        "peer_subkeys": nrm(ks[15], (DEPTH, PEER_HEADS, 2, N_KEYS, PEER_QDIM // 2), (PEER_QDIM // 2) ** -0.5),
        "peer_u": nrm(ks[16], (DEPTH, N_EXPERTS, D_MODEL), D_MODEL ** -0.5),
        "peer_v": nrm(ks[17], (DEPTH, N_EXPERTS, D_MODEL), 0.5),
        "final_g": 1.0 + nrm(ks[18], (D_MODEL,), 0.02),
    }


def reference(x, c, positions, norm1_g, norm2_g, w_ada, b_ada, w_in, lam_q1, lam_k1,
              lam_q2, lam_k2, diff_subln_g, dil_out_g, w_out, peer_wq, peer_subkeys,
              peer_u, peer_v, final_g):
    cos, sin = rotary_tables(positions)
    c_act = jax.nn.silu(c)
    for l in range(DEPTH):
        lam_init = 0.8 - 0.6 * math.exp(-0.3 * l)
        mod = jnp.einsum('bd,de->be', c_act, w_ada[l]) + b_ada[l]
        sh1, sc1, g1, sh2, sc2, g2 = jnp.split(mod, 6, axis=-1)
        h = modulate(rmsnorm(x, norm1_g[l]), sh1, sc1)
        x = x + g1[:, None, :] * hybrid_mixer(h, cos, sin, w_in[l], w_out[l], lam_q1[l], lam_k1[l],
                                              lam_q2[l], lam_k2[l], diff_subln_g[l], dil_out_g[l], lam_init)
        h = modulate(rmsnorm(x, norm2_g[l]), sh2, sc2)
        x = x + g2[:, None, :] * peer_ffn(h, peer_wq[l], peer_subkeys[l], peer_u[l], peer_v[l])
    return rmsnorm(x, final_g)
```

```python
import functools
import math

import jax
import jax.numpy as jnp
from jax import lax
from jax.experimental import pallas as pl
from jax.experimental.pallas import tpu as pltpu

F32 = jnp.float32
BF16 = jnp.bfloat16

HEAD_DIM = 128
ROT_DIM = HEAD_DIM // 4
ROT_HALF = ROT_DIM // 2
ROPE_THETA = 500000.0
DILATED_CONFIGS = ((128, 1), (512, 4), (2048, 16))
DIL_BACK = max(w for w, _ in DILATED_CONFIGS)
PEER_HEADS = 8
N_KEYS = 128
PEER_TOPK = 16
NORM_EPS = 1e-6
NEG_INF = -1e30
LANES = 128
MIB = 1024 * 1024

NT_DIMS = (((1,), (1,)), ((), ()))
TN_DIMS = (((0,), (0,)), ((), ()))


def _params(semantics, vmem_mib):
    return pltpu.CompilerParams(dimension_semantics=semantics, vmem_limit_bytes=vmem_mib * MIB)


def _tile(n, pref):
    t = min(n, pref)
    assert n % t == 0, (n, pref)
    return t


def _ada_kernel(c_ref, w_ref, b_ref, o_ref):
    c = c_ref[...]
    ca = c * jax.nn.sigmoid(c)
    for j in range(o_ref.shape[1] // LANES):
        sl = slice(j * LANES, (j + 1) * LANES)
        o_ref[:, sl] = jnp.sum(w_ref[:, sl] * ca, axis=0, keepdims=True) + b_ref[:, sl]


def ada_mod(c, w_ada, b_ada):
    D, N = w_ada.shape
    tn = _tile(N, 1024)
    cb = jnp.broadcast_to(c.reshape(D, 1), (D, LANES))
    return pl.pallas_call(
        _ada_kernel,
        grid=(N // tn,),
        in_specs=[pl.BlockSpec((D, LANES), lambda j: (0, 0)),
                  pl.BlockSpec((D, tn), lambda j: (0, j)),
                  pl.BlockSpec((1, tn), lambda j: (0, j))],
        out_specs=pl.BlockSpec((1, tn), lambda j: (0, j)),
        out_shape=jax.ShapeDtypeStruct((1, N), F32),
        compiler_params=_params(("arbitrary",), 48),
        name="ada_mod",
    )(cb, w_ada, b_ada.reshape(1, N))


def _rope_kernel(pos_ref, freq_ref, c_ref, s1_ref, s2_ref):
    ang = pos_ref[...].astype(F32) * freq_ref[...]
    lane = lax.broadcasted_iota(jnp.int32, ang.shape, 1)
    cs = jnp.cos(ang)
    sn = jnp.sin(ang)
    c_ref[...] = jnp.where(lane < ROT_DIM, cs, 1.0)
    s1_ref[...] = jnp.where(lane < ROT_HALF, -sn, 0.0)
    s2_ref[...] = jnp.where((lane >= ROT_HALF) & (lane < ROT_DIM), sn, 0.0)


def rope_tables(positions):
    S = positions.shape[0]
    ts = _tile(S, 1024)
    inv_freq = jnp.power(ROPE_THETA, -jnp.arange(0, ROT_DIM, 2, dtype=F32) / ROT_DIM)
    freq = jnp.tile(inv_freq, LANES // ROT_HALF).reshape(1, LANES)
    spec = pl.BlockSpec((ts, LANES), lambda i: (i, 0))
    return pl.pallas_call(
        _rope_kernel,
        grid=(S // ts,),
        in_specs=[pl.BlockSpec((ts, 1), lambda i: (i, 0)), pl.BlockSpec((1, LANES), lambda i: (0, 0))],
        out_specs=[spec, spec, spec],
        out_shape=[jax.ShapeDtypeStruct((S, LANES), F32)] * 3,
        compiler_params=_params(("arbitrary",), 32),
        name="rope_tables",
    )(positions.reshape(S, 1), freq)


def _rms_mod(x, g, scale, shift):
    y = x * lax.rsqrt(jnp.mean(x * x, axis=-1, keepdims=True) + NORM_EPS) * g
    return y * (1 + scale) + shift


def _norm_mod_kernel(x_ref, g_ref, sh_ref, sc_ref, o_ref):
    o_ref[...] = _rms_mod(x_ref[...], g_ref[...], sc_ref[...], sh_ref[...]).astype(o_ref.dtype)


def norm_mod(x, g, mod, shift_idx, scale_idx):
    S, D = x.shape
    tm = _tile(S, 256)
    return pl.pallas_call(
        _norm_mod_kernel,
        grid=(S // tm,),
        in_specs=[pl.BlockSpec((tm, D), lambda i: (i, 0)),
                  pl.BlockSpec((1, D), lambda i: (0, 0)),
                  pl.BlockSpec((1, D), lambda i: (0, shift_idx)),
                  pl.BlockSpec((1, D), lambda i: (0, scale_idx))],
        out_specs=pl.BlockSpec((tm, D), lambda i: (i, 0)),
        out_shape=jax.ShapeDtypeStruct((S, D), BF16),
        compiler_params=_params(("arbitrary",), 32),
        name="norm_mod",
    )(x, g.reshape(1, D), mod, mod)


def _in_proj_kernel(h_ref, w_ref, c_ref, s1_ref, s2_ref, o_ref, *, sec_width):
    tn = o_ref.shape[1]
    acc = jnp.dot(h_ref[...], w_ref[...], preferred_element_type=F32)
    section = (pl.program_id(1) * tn) // sec_width
    is_value = (section == 2) | (section == 5)

    @pl.when(is_value)
    def _():
        o_ref[...] = acc.astype(o_ref.dtype)

    @pl.when(jnp.logical_not(is_value))
    def _():
        cs, s1, s2 = c_ref[...], s1_ref[...], s2_ref[...]
        for j in range(tn // HEAD_DIM):
            sl = slice(j * HEAD_DIM, (j + 1) * HEAD_DIM)
            t = acc[:, sl]
            r = (t * cs + pltpu.roll(t, HEAD_DIM - ROT_HALF, 1) * s1 + pltpu.roll(t, ROT_HALF, 1) * s2)
            o_ref[:, sl] = r.astype(o_ref.dtype)


def in_proj(h, w_in, tables):
    S, D = h.shape
    N = w_in.shape[1]
    sec_width = N // 6
    tm = _tile(S, 1024)
    tn = _tile(sec_width, 512)
    tspec = pl.BlockSpec((tm, LANES), lambda i, j: (i, 0))
    return pl.pallas_call(
        functools.partial(_in_proj_kernel, sec_width=sec_width),
        grid=(S // tm, N // tn),
        in_specs=[pl.BlockSpec((tm, D), lambda i, j: (i, 0)),
                  pl.BlockSpec((D, tn), lambda i, j: (0, j)),
                  tspec, tspec, tspec],
        out_specs=pl.BlockSpec((tm, tn), lambda i, j: (i, j)),
        out_shape=jax.ShapeDtypeStruct((S, N), BF16),
        compiler_params=_params(("arbitrary", "arbitrary"), 48),
        name="in_proj",
    )(h, w_in, *tables)


def _diff_attn_kernel(lam_ref, q_ref, k_ref, v_ref, g_ref, o_ref,
                      m1, l1, a1, m2, l2, a2, *, tq, tk, lam_init):
    i = pl.program_id(1)
    kv = pl.program_id(2)
    scale = HEAD_DIM ** -0.5

    @pl.when(kv == 0)
    def _():
        for m, l, a in ((m1, l1, a1), (m2, l2, a2)):
            m[...] = jnp.full(m.shape, -jnp.inf, F32)
            l[...] = jnp.zeros(l.shape, F32)
            a[...] = jnp.zeros(a.shape, F32)

    def step(masked):
        v = v_ref[...]
        if masked:
            qpos = i * tq + lax.broadcasted_iota(jnp.int32, (tq, tk), 0)
            kpos = kv * tk + lax.broadcasted_iota(jnp.int32, (tq, tk), 1)
            keep = kpos <= qpos
        for c, (m, l, a) in enumerate(((m1, l1, a1), (m2, l2, a2))):
            sl = slice(c * HEAD_DIM, (c + 1) * HEAD_DIM)
            s = lax.dot_general(q_ref[:, sl], k_ref[:, sl], NT_DIMS, preferred_element_type=F32) * scale
            if masked:
                s = jnp.where(keep, s, NEG_INF)
            m_prev = m[...]
            m_new = jnp.maximum(m_prev, jnp.max(s, axis=-1, keepdims=True))
            alpha = jnp.exp(m_prev - m_new)
            p = jnp.exp(s - m_new)
            l[...] = alpha * l[...] + jnp.sum(p, axis=-1, keepdims=True)
            a[...] = alpha * a[...] + jnp.dot(p.astype(v.dtype), v, preferred_element_type=F32)
            m[...] = m_new

    first_q, last_q = i * tq, i * tq + tq - 1
    first_k, last_k = kv * tk, kv * tk + tk - 1
    pl.when(last_k <= first_q)(lambda: step(False))
    pl.when((last_k > first_q) & (first_k <= last_q))(lambda: step(True))

    @pl.when(kv == pl.num_programs(2) - 1)
    def _():
        lp = lam_ref[...]
        lam = (jnp.exp(jnp.sum(lp[0:1] * lp[1:2], axis=-1, keepdims=True))
               - jnp.exp(jnp.sum(lp[2:3] * lp[3:4], axis=-1, keepdims=True)) + lam_init)
        o = a1[...] / l1[...] - lam * (a2[...] / l2[...])
        y = o * lax.rsqrt(jnp.mean(o * o, axis=-1, keepdims=True) + NORM_EPS) * g_ref[...]
        o_ref[...] = (y * (1 - lam_init)).astype(o_ref.dtype)


def diff_attn(proj, lam_rows, subln_g, n_heads, lam_init):
    S = proj.shape[0]
    tq = _tile(S, 512)
    tk = _tile(S, 512)
    hw = 2 * HEAD_DIM

    def kv_idx(h, i, kv):
        return jnp.minimum(kv, (i * tq + tq - 1) // tk)

    return pl.pallas_call(
        functools.partial(_diff_attn_kernel, tq=tq, tk=tk, lam_init=lam_init),
        grid=(n_heads, S // tq, S // tk),
        in_specs=[pl.BlockSpec((4, HEAD_DIM), lambda h, i, kv: (0, 0)),
                  pl.BlockSpec((tq, hw), lambda h, i, kv: (i, h)),
                  pl.BlockSpec((tk, hw), lambda h, i, kv: (kv_idx(h, i, kv), n_heads + h)),
                  pl.BlockSpec((tk, hw), lambda h, i, kv: (kv_idx(h, i, kv), 2 * n_heads + h)),
                  pl.BlockSpec((1, hw), lambda h, i, kv: (0, 0))],
        out_specs=pl.BlockSpec((tq, hw), lambda h, i, kv: (i, h)),
        out_shape=jax.ShapeDtypeStruct((S, n_heads * hw), BF16),
        scratch_shapes=[pltpu.VMEM((tq, 1), F32), pltpu.VMEM((tq, 1), F32), pltpu.VMEM((tq, hw), F32),
                        pltpu.VMEM((tq, 1), F32), pltpu.VMEM((tq, 1), F32), pltpu.VMEM((tq, hw), F32)],
        compiler_params=_params(("arbitrary", "arbitrary", "arbitrary"), 48),
        name="diff_attn",
    )(lam_rows, proj, proj, proj, subln_g.reshape(1, hw))


def _dil_bias(delta):
    count = jnp.zeros(delta.shape, F32)
    for window, dilation in DILATED_CONFIGS:
        ok = (delta >= 0) & (delta <= window) & ((delta & (dilation - 1)) == 0)
        count = count + jnp.where(ok, 1.0, 0.0)
    return jnp.where(count > 0, jnp.log(jnp.maximum(count, 1.0)), NEG_INF)


def _dil_attn_kernel(q_ref, k_ref, v_ref, g_ref, o_ref, bias_ref, *, tq, win):
    i = pl.program_id(1)
    scale = HEAD_DIM ** -0.5
    qi = lax.broadcasted_iota(jnp.int32, (tq, win), 0)
    kj = lax.broadcasted_iota(jnp.int32, (tq, win), 1)

    @pl.when((pl.program_id(0) == 0) & (i == 0))
    def _():
        bias_ref[...] = _dil_bias(qi + DIL_BACK - kj)

    def attend(start, bias):
        kw = k_ref[pl.ds(start, win), :]
        vw = v_ref[pl.ds(start, win), :]
        s = lax.dot_general(q_ref[...], kw, NT_DIMS, preferred_element_type=F32) * scale + bias
        p = jnp.exp(s - jnp.max(s, axis=-1, keepdims=True))
        o = jnp.dot(p.astype(vw.dtype), vw, preferred_element_type=F32) / jnp.sum(p, axis=-1, keepdims=True)
        y = o * lax.rsqrt(jnp.mean(o * o, axis=-1, keepdims=True) + NORM_EPS) * g_ref[...]
        o_ref[...] = y.astype(o_ref.dtype)

    n_clamped = DIL_BACK // tq

    @pl.when(i >= n_clamped)
    def _():
        attend(pl.multiple_of(i * tq - DIL_BACK, tq), bias_ref[...])

    @pl.when(i < n_clamped)
    def _():
        attend(0, _dil_bias(i * tq + qi - kj))


def dil_attn(proj, out_g, n_heads, col0):
    S = proj.shape[0]
    for window, dilation in DILATED_CONFIGS:
        assert window % dilation == 0 and dilation & (dilation - 1) == 0
    tq = _tile(S, 256)
    win = DIL_BACK + tq
    assert S >= win and DIL_BACK % tq == 0
    c0 = col0 // HEAD_DIM
    return pl.pallas_call(
        functools.partial(_dil_attn_kernel, tq=tq, win=win),
        grid=(n_heads, S // tq),
        in_specs=[pl.BlockSpec((tq, HEAD_DIM), lambda h, i: (i, c0 + h)),
                  pl.BlockSpec((S, HEAD_DIM), lambda h, i: (0, c0 + n_heads + h)),
                  pl.BlockSpec((S, HEAD_DIM), lambda h, i: (0, c0 + 2 * n_heads + h)),
                  pl.BlockSpec((1, HEAD_DIM), lambda h, i: (0, h))],
        out_specs=pl.BlockSpec((tq, HEAD_DIM), lambda h, i: (i, h)),
        out_shape=jax.ShapeDtypeStruct((S, n_heads * HEAD_DIM), BF16),
        scratch_shapes=[pltpu.VMEM((tq, win), F32)],
        compiler_params=_params(("arbitrary", "arbitrary"), 48),
        name="dil_attn",
    )(proj, proj, proj, out_g.reshape(1, n_heads * HEAD_DIM))


def _out_proj_kernel(ya_ref, yb_ref, w_ref, x_ref, gate_ref, g_ref, sh_ref, sc_ref,
                     x1_ref, h_ref, acc_ref, *, nka):
    k = pl.program_id(1)

    @pl.when(k == 0)
    def _():
        acc_ref[...] = jnp.zeros(acc_ref.shape, F32)

    @pl.when(k < nka)
    def _():
        acc_ref[...] += jnp.dot(ya_ref[...], w_ref[...], preferred_element_type=F32)

    @pl.when(k >= nka)
    def _():
        acc_ref[...] += jnp.dot(yb_ref[...], w_ref[...], preferred_element_type=F32)

    @pl.when(k == pl.num_programs(1) - 1)
    def _():
        x1 = x_ref[...] + gate_ref[...] * acc_ref[...]
        x1_ref[...] = x1
        h_ref[...] = _rms_mod(x1, g_ref[...], sc_ref[...], sh_ref[...]).astype(h_ref.dtype)


def out_proj(ya, yb, w_out, x, mod, gate_idx, norm_g, shift_idx, scale_idx):
    S, D = x.shape
    ka, kb = ya.shape[1], yb.shape[1]
    tm = _tile(S, 256)
    tk = _tile(math.gcd(ka, kb), 512)
    nka, nkb = ka // tk, kb // tk
    row = lambda idx: pl.BlockSpec((1, D), lambda i, k: (0, idx))
    return pl.pallas_call(
        functools.partial(_out_proj_kernel, nka=nka),
        grid=(S // tm, nka + nkb),
        in_specs=[pl.BlockSpec((tm, tk), lambda i, k: (i, jnp.minimum(k, nka - 1))),
                  pl.BlockSpec((tm, tk), lambda i, k: (i, jnp.maximum(k - nka, 0))),
                  pl.BlockSpec((tk, D), lambda i, k: (k, 0)),
                  pl.BlockSpec((tm, D), lambda i, k: (i, 0)),
                  row(gate_idx), row(0), row(shift_idx), row(scale_idx)],
        out_specs=[pl.BlockSpec((tm, D), lambda i, k: (i, 0)), pl.BlockSpec((tm, D), lambda i, k: (i, 0))],
        out_shape=[jax.ShapeDtypeStruct((S, D), F32), jax.ShapeDtypeStruct((S, D), BF16)],
        scratch_shapes=[pltpu.VMEM((tm, D), F32)],
        compiler_params=_params(("arbitrary", "arbitrary"), 48),
        name="out_proj",
    )(ya, yb, w_out, x, mod, norm_g.reshape(1, D), mod, mod)


def _top16(x):
    rows = lax.broadcasted_iota(jnp.int32, x.shape, 0)
    rank = jnp.full(x.shape, PEER_TOPK, jnp.int32)
    vals = []
    for r in range(PEER_TOPK):
        m = jnp.max(x, axis=0, keepdims=True)
        idx = jnp.min(jnp.where(x == m, rows, x.shape[0]), axis=0, keepdims=True)
        hit = rows == idx
        rank = jnp.where(hit, r, rank)
        x = jnp.where(hit, -jnp.inf, x)
        vals.append(m)
    return jnp.concatenate(vals, axis=0), rank


def _peer_router_kernel(h_ref, wq_ref, sk_ref, rank1_ref, cnt0_ref, e0_ref, e1_ref):
    q = jnp.dot(h_ref[...], wq_ref[...], preferred_element_type=F32)
    half = q.shape[1] // 2
    s = [lax.dot_general(sk_ref[0, p], q[:, p * half:(p + 1) * half], NT_DIMS,
                         precision=lax.Precision.HIGHEST, preferred_element_type=F32)
         for p in range(2)]
    a, rank0 = _top16(s[0])
    b, rank1 = _top16(s[1])
    cand = jnp.concatenate([a[p:p + 1] + b for p in range(PEER_TOPK)], axis=0)
    fin, sel_rank = _top16(cand)
    sel = jnp.where(sel_rank < PEER_TOPK, 1.0, 0.0)
    cnt = jnp.concatenate([jnp.sum(sel[p * PEER_TOPK:(p + 1) * PEER_TOPK], axis=0, keepdims=True)
                           for p in range(PEER_TOPK)], axis=0)
    cnt0 = jnp.zeros(s[0].shape, F32)
    for p in range(PEER_TOPK):
        cnt0 = jnp.where(rank0 == p, cnt[p:p + 1], cnt0)
    z = jnp.sum(jnp.exp(fin - fin[0:1]), axis=0, keepdims=True)
    rank1_ref[0] = rank1.astype(F32)
    cnt0_ref[0] = cnt0
    e0_ref[0] = jnp.exp(s[0] - a[0:1])
    e1_ref[0] = jnp.exp(s[1] - b[0:1]) / z


def peer_router(h, w_q, subkeys):
    S, D = h.shape
    n_heads = subkeys.shape[0]
    qd = w_q.shape[1] // n_heads
    T = _tile(S, 512)
    ospec = pl.BlockSpec((1, N_KEYS, T), lambda t, hd: (hd, 0, t))
    return pl.pallas_call(
        _peer_router_kernel,
        grid=(S // T, n_heads),
        in_specs=[pl.BlockSpec((T, D), lambda t, hd: (t, 0)),
                  pl.BlockSpec((D, qd), lambda t, hd: (0, hd)),
                  pl.BlockSpec((1, 2, N_KEYS, qd // 2), lambda t, hd: (hd, 0, 0, 0))],
        out_specs=[ospec] * 4,
        out_shape=[jax.ShapeDtypeStruct((n_heads, N_KEYS, S), F32)] * 4,
        compiler_params=_params(("arbitrary", "arbitrary"), 48),
        name="peer_router",
    )(h, w_q, subkeys)


def _gelu(x):
    return 0.5 * x * (1.0 + lax.erf(x * (0.5 ** 0.5)))


PEER_GI = 8
PEER_GJ = 32


def _peer_expert_kernel(h_ref, u_ref, v_ref, rank1_ref, cnt0_ref, e0_ref, e1_ref, o_ref, w_ref):
    ib, jb = pl.program_id(1), pl.program_id(2)
    gi, gj, D = u_ref.shape
    n_heads, _, T = rank1_ref.shape
    i0 = pl.multiple_of(ib * gi, gi)
    j0 = pl.multiple_of(jb * gj, gj)

    @pl.when((ib == 0) & (jb == 0))
    def _():
        o_ref[...] = jnp.zeros(o_ref.shape, F32)

    u = u_ref[...].reshape(gi * gj, D)
    act = _gelu(lax.dot_general(u, h_ref[...], NT_DIMS, preferred_element_type=F32))
    for lc in range(T // LANES):
        lsl = slice(lc * LANES, (lc + 1) * LANES)
        c0 = [cnt0_ref[hd, pl.ds(i0, gi), lsl] for hd in range(n_heads)]
        e0 = [e0_ref[hd, pl.ds(i0, gi), lsl] for hd in range(n_heads)]
        for ii in range(gi):
            rsl = slice(ii * gj, (ii + 1) * gj)
            g = jnp.zeros((gj, LANES), F32)
            for hd in range(n_heads):
                r1 = rank1_ref[hd, pl.ds(j0, gj), lsl]
                e1 = e1_ref[hd, pl.ds(j0, gj), lsl]
                g = g + jnp.where(r1 < c0[hd][ii:ii + 1], e1 * e0[hd][ii:ii + 1], 0.0)
            w_ref[rsl, lsl] = (g * act[rsl, lsl]).astype(w_ref.dtype)
    v = v_ref[...].reshape(gi * gj, D)
    o_ref[...] += lax.dot_general(w_ref[...], v, TN_DIMS, preferred_element_type=F32)


def peer_experts(h, u, v, tables):
    S, D = h.shape
    n_heads = tables[0].shape[0]
    T = _tile(S, 512)
    u3 = u.reshape(N_KEYS, N_KEYS, D)
    v3 = v.reshape(N_KEYS, N_KEYS, D)
    wspec = pl.BlockSpec((PEER_GI, PEER_GJ, D), lambda t, i, j: (i, j, 0))
    tspec = pl.BlockSpec((n_heads, N_KEYS, T), lambda t, i, j: (0, 0, t))
    return pl.pallas_call(
        _peer_expert_kernel,
        grid=(S // T, N_KEYS // PEER_GI, N_KEYS // PEER_GJ),
        in_specs=[pl.BlockSpec((T, D), lambda t, i, j: (t, 0)), wspec, wspec, tspec, tspec, tspec, tspec],
        out_specs=pl.BlockSpec((T, D), lambda t, i, j: (t, 0)),
        out_shape=jax.ShapeDtypeStruct((S, D), F32),
        scratch_shapes=[pltpu.VMEM((PEER_GI * PEER_GJ, T), BF16)],
        compiler_params=_params(("arbitrary", "arbitrary", "arbitrary"), 56),
        name="peer_experts",
    )(h, u3, v3, *tables)


def _final_kernel(x_ref, y_ref, gate_ref, g_ref, o_ref, *, normalize):
    x = x_ref[...] + gate_ref[...] * y_ref[...]
    if normalize:
        x = x * lax.rsqrt(jnp.mean(x * x, axis=-1, keepdims=True) + NORM_EPS) * g_ref[...]
    o_ref[...] = x


def final_norm(x, y, mod, gate_idx, g, normalize):
    S, D = x.shape
    tm = _tile(S, 256)
    blk = pl.BlockSpec((tm, D), lambda i: (i, 0))
    return pl.pallas_call(
        functools.partial(_final_kernel, normalize=normalize),
        grid=(S // tm,),
        in_specs=[blk, blk, pl.BlockSpec((1, D), lambda i: (0, gate_idx)), pl.BlockSpec((1, D), lambda i: (0, 0))],
        out_specs=blk,
        out_shape=jax.ShapeDtypeStruct((S, D), F32),
        compiler_params=_params(("arbitrary",), 48),
        name="final_norm",
    )(x, y, mod, g.reshape(1, D))


def kernel(x, c, positions, norm1_g, norm2_g, w_ada, b_ada, w_in, lam_q1, lam_k1, lam_q2, lam_k2,
           diff_subln_g, dil_out_g, w_out, peer_wq, peer_subkeys, peer_u, peer_v, final_g):
    B, S, D = x.shape
    depth = w_ada.shape[0]
    assert B == 1, "one sequence per call"
    diff_width = D // 2
    diff_heads = diff_width // (2 * HEAD_DIM)
    dil_heads = (D - diff_width) // HEAD_DIM
    xs = x.reshape(S, D)
    tables = rope_tables(positions.reshape(S))
    for l in range(depth):
        lam_init = 0.8 - 0.6 * math.exp(-0.3 * l)
        mod = ada_mod(c, w_ada[l], b_ada[l])
        h = norm_mod(xs, norm1_g[l], mod, 0, 1)
        proj = in_proj(h, w_in[l].astype(BF16), tables)
        lam_rows = jnp.stack([lam_q1[l], lam_k1[l], lam_q2[l], lam_k2[l]])
        y_diff = diff_attn(proj, lam_rows, diff_subln_g[l], diff_heads, lam_init)
        y_dil = dil_attn(proj, dil_out_g[l], dil_heads, 3 * diff_width)
        xs, h = out_proj(y_diff, y_dil, w_out[l].astype(BF16), xs, mod, 2, norm2_g[l], 3, 4)
        gates = peer_router(h, peer_wq[l].astype(BF16), peer_subkeys[l])
        y = peer_experts(h, peer_u[l].astype(BF16), peer_v[l].astype(BF16), gates)
        xs = final_norm(xs, y, mod, 5, final_g, normalize=(l == depth - 1))
    return xs.reshape(B, S, D)
```

```python
import functools
import math

import jax
import jax.numpy as jnp
from jax import lax
from jax.experimental import pallas as pl
from jax.experimental.pallas import tpu as pltpu

F32 = jnp.float32
BF16 = jnp.bfloat16

HEAD_DIM = 128
ROT_DIM = HEAD_DIM // 4
ROT_HALF = ROT_DIM // 2
ROPE_THETA = 500000.0
DILATED_CONFIGS = ((128, 1), (512, 4), (2048, 16))
DIL_BACK = max(w for w, _ in DILATED_CONFIGS)
PEER_HEADS = 8
N_KEYS = 128
PEER_TOPK = 16
NORM_EPS = 1e-6
NEG_INF = -1e30
QK_SCALE = HEAD_DIM ** -0.5 * math.log2(math.e)
DIFF_ROW_SPLIT = 2
LANES = 128
MIB = 1024 * 1024

NT_DIMS = (((1,), (1,)), ((), ()))
TN_DIMS = (((0,), (0,)), ((), ()))


def _params(semantics, vmem_mib):
    return pltpu.CompilerParams(dimension_semantics=semantics, vmem_limit_bytes=vmem_mib * MIB)


def _tile(n, pref):
    t = min(n, pref)
    assert n % t == 0, (n, pref)
    return t


def _ada_kernel(c_ref, w_ref, b_ref, o_ref):
    c = c_ref[...]
    ca = c * jax.nn.sigmoid(c)
    for j in range(o_ref.shape[1] // LANES):
        sl = slice(j * LANES, (j + 1) * LANES)
        o_ref[:, sl] = jnp.sum(w_ref[:, sl] * ca, axis=0, keepdims=True) + b_ref[:, sl]


def ada_mod(c, w_ada, b_ada):
    D, N = w_ada.shape
    tn = _tile(N, 1024)
    cb = jnp.broadcast_to(c.reshape(D, 1), (D, LANES))
    return pl.pallas_call(
        _ada_kernel,
        grid=(N // tn,),
        in_specs=[pl.BlockSpec((D, LANES), lambda j: (0, 0)),
                  pl.BlockSpec((D, tn), lambda j: (0, j)),
                  pl.BlockSpec((1, tn), lambda j: (0, j))],
        out_specs=pl.BlockSpec((1, tn), lambda j: (0, j)),
        out_shape=jax.ShapeDtypeStruct((1, N), F32),
        compiler_params=_params(("arbitrary",), 48),
        name="ada_mod",
    )(cb, w_ada, b_ada.reshape(1, N))


def _rope_kernel(pos_ref, freq_ref, c_ref, s1_ref, s2_ref):
    ang = pos_ref[...].astype(F32) * freq_ref[...]
    lane = lax.broadcasted_iota(jnp.int32, ang.shape, 1)
    cs = jnp.cos(ang)
    sn = jnp.sin(ang)
    c_ref[...] = jnp.where(lane < ROT_DIM, cs, 1.0)
    s1_ref[...] = jnp.where(lane < ROT_HALF, -sn, 0.0)
    s2_ref[...] = jnp.where((lane >= ROT_HALF) & (lane < ROT_DIM), sn, 0.0)


def rope_tables(positions):
    S = positions.shape[0]
    ts = _tile(S, 1024)
    inv_freq = jnp.power(ROPE_THETA, -jnp.arange(0, ROT_DIM, 2, dtype=F32) / ROT_DIM)
    freq = jnp.tile(inv_freq, LANES // ROT_HALF).reshape(1, LANES)
    spec = pl.BlockSpec((ts, LANES), lambda i: (i, 0))
    return pl.pallas_call(
        _rope_kernel,
        grid=(S // ts,),
        in_specs=[pl.BlockSpec((ts, 1), lambda i: (i, 0)), pl.BlockSpec((1, LANES), lambda i: (0, 0))],
        out_specs=[spec, spec, spec],
        out_shape=[jax.ShapeDtypeStruct((S, LANES), F32)] * 3,
        compiler_params=_params(("arbitrary",), 32),
        name="rope_tables",
    )(positions.reshape(S, 1), freq)


def _rms_mod(x, g, scale, shift):
    y = x * lax.rsqrt(jnp.mean(x * x, axis=-1, keepdims=True) + NORM_EPS) * g
    return y * (1 + scale) + shift


def _norm_mod_kernel(x_ref, g_ref, sh_ref, sc_ref, o_ref):
    o_ref[...] = _rms_mod(x_ref[...], g_ref[...], sc_ref[...], sh_ref[...]).astype(o_ref.dtype)


def norm_mod(x, g, mod, shift_idx, scale_idx):
    S, D = x.shape
    tm = _tile(S, 256)
    return pl.pallas_call(
        _norm_mod_kernel,
        grid=(S // tm,),
        in_specs=[pl.BlockSpec((tm, D), lambda i: (i, 0)),
                  pl.BlockSpec((1, D), lambda i: (0, 0)),
                  pl.BlockSpec((1, D), lambda i: (0, shift_idx)),
                  pl.BlockSpec((1, D), lambda i: (0, scale_idx))],
        out_specs=pl.BlockSpec((tm, D), lambda i: (i, 0)),
        out_shape=jax.ShapeDtypeStruct((S, D), BF16),
        compiler_params=_params(("arbitrary",), 32),
        name="norm_mod",
    )(x, g.reshape(1, D), mod, mod)


def _in_proj_kernel(h_ref, w_ref, c_ref, s1_ref, s2_ref, o_ref, *, sec_width):
    tn = o_ref.shape[1]
    acc = jnp.dot(h_ref[...], w_ref[...], preferred_element_type=F32)
    section = (pl.program_id(1) * tn) // sec_width
    is_value = (section == 2) | (section == 5)
    q_scale = jnp.where((section == 0) | (section == 3), QK_SCALE, 1.0).astype(F32)

    @pl.when(is_value)
    def _():
        o_ref[...] = acc.astype(o_ref.dtype)

    @pl.when(jnp.logical_not(is_value))
    def _():
        cs, s1, s2 = c_ref[...], s1_ref[...], s2_ref[...]
        for j in range(tn // HEAD_DIM):
            sl = slice(j * HEAD_DIM, (j + 1) * HEAD_DIM)
            t = acc[:, sl]
            r = (t * cs + pltpu.roll(t, HEAD_DIM - ROT_HALF, 1) * s1 + pltpu.roll(t, ROT_HALF, 1) * s2)
            o_ref[:, sl] = (r * q_scale).astype(o_ref.dtype)


def in_proj(h, w_in, tables):
    S, D = h.shape
    N = w_in.shape[1]
    sec_width = N // 6
    tm = _tile(S, 1024)
    tn = _tile(sec_width, 512)
    tspec = pl.BlockSpec((tm, LANES), lambda i, j: (i, 0))
    return pl.pallas_call(
        functools.partial(_in_proj_kernel, sec_width=sec_width),
        grid=(S // tm, N // tn),
        in_specs=[pl.BlockSpec((tm, D), lambda i, j: (i, 0)),
                  pl.BlockSpec((D, tn), lambda i, j: (0, j)),
                  tspec, tspec, tspec],
        out_specs=pl.BlockSpec((tm, tn), lambda i, j: (i, j)),
        out_shape=jax.ShapeDtypeStruct((S, N), BF16),
        compiler_params=_params(("arbitrary", "arbitrary"), 48),
        name="in_proj",
    )(h, w_in, *tables)


def _diff_attn_kernel(lam_ref, q_ref, k_ref, v_ref, g_ref, o_ref,
                      m1, l1, a1, m2, l2, a2, *, tq, lam_init):
    i = pl.program_id(1)
    for m, l, a in ((m1, l1, a1), (m2, l2, a2)):
        m[...] = jnp.full(m.shape, -jnp.inf, F32)
        l[...] = jnp.zeros(l.shape, F32)
        a[...] = jnp.zeros(a.shape, F32)

    stats = ((m1, l1, a1), (m2, l2, a2))
    tr = tq // DIFF_ROW_SPLIT
    streams = [(r, c) for r in range(DIFF_ROW_SPLIT) for c in range(2)]

    def chunk(kv, masked):
        k0 = pl.multiple_of(kv * tq, tq)
        v = v_ref[pl.ds(k0, tq), :]

        def scores(r, c):
            sl = slice(c * HEAD_DIM, (c + 1) * HEAD_DIM)
            return lax.dot_general(q_ref[r * tr:(r + 1) * tr, sl], k_ref[pl.ds(k0, tq), sl], NT_DIMS,
                                   preferred_element_type=F32)

        s_next = scores(*streams[0])
        for n, (r, c) in enumerate(streams):
            s = s_next
            if n + 1 < len(streams):
                s_next = scores(*streams[n + 1])
            m, l, a = stats[c]
            rows = slice(r * tr, (r + 1) * tr)
            if masked:
                keep = (lax.broadcasted_iota(jnp.int32, (tr, tq), 1)
                        <= r * tr + lax.broadcasted_iota(jnp.int32, (tr, tq), 0))
                s = jnp.where(keep, s, NEG_INF)
            m_prev = m[rows]
            m_new = jnp.maximum(m_prev, jnp.max(s, axis=-1, keepdims=True))
            alpha = jnp.exp2(m_prev - m_new)
            p = jnp.exp2(s - m_new)
            l[rows] = alpha * l[rows] + jnp.sum(p, axis=-1, keepdims=True)
            a[rows] = alpha * a[rows] + jnp.dot(p.astype(v.dtype), v, preferred_element_type=F32)
            m[rows] = m_new

    def body(kv, carry):
        chunk(kv, False)
        return carry

    lax.fori_loop(0, i, body, 0)
    chunk(i, True)

    lp = lam_ref[...]
    lam = (jnp.exp(jnp.sum(lp[0:1] * lp[1:2], axis=-1, keepdims=True))
           - jnp.exp(jnp.sum(lp[2:3] * lp[3:4], axis=-1, keepdims=True)) + lam_init)
    o = a1[...] / l1[...] - lam * (a2[...] / l2[...])
    y = o * lax.rsqrt(jnp.mean(o * o, axis=-1, keepdims=True) + NORM_EPS) * g_ref[...]
    o_ref[...] = (y * (1 - lam_init)).astype(o_ref.dtype)


def diff_attn(proj, lam_rows, subln_g, n_heads, lam_init):
    S = proj.shape[0]
    tq = _tile(S, 512)
    hw = 2 * HEAD_DIM
    return pl.pallas_call(
        functools.partial(_diff_attn_kernel, tq=tq, lam_init=lam_init),
        grid=(n_heads, S // tq),
        in_specs=[pl.BlockSpec((4, HEAD_DIM), lambda h, i: (0, 0)),
                  pl.BlockSpec((tq, hw), lambda h, i: (i, h)),
                  pl.BlockSpec((S, hw), lambda h, i: (0, n_heads + h)),
                  pl.BlockSpec((S, hw), lambda h, i: (0, 2 * n_heads + h)),
                  pl.BlockSpec((1, hw), lambda h, i: (0, 0))],
        out_specs=pl.BlockSpec((tq, hw), lambda h, i: (i, h)),
        out_shape=jax.ShapeDtypeStruct((S, n_heads * hw), BF16),
        scratch_shapes=[pltpu.VMEM((tq, 1), F32), pltpu.VMEM((tq, 1), F32), pltpu.VMEM((tq, hw), F32),
                        pltpu.VMEM((tq, 1), F32), pltpu.VMEM((tq, 1), F32), pltpu.VMEM((tq, hw), F32)],
        compiler_params=_params(("arbitrary", "arbitrary"), 48),
        name="diff_attn",
    )(lam_rows, proj, proj, proj, subln_g.reshape(1, hw))


def _dil_bias(delta):
    count = jnp.zeros(delta.shape, F32)
    for window, dilation in DILATED_CONFIGS:
        ok = (delta >= 0) & (delta <= window) & ((delta & (dilation - 1)) == 0)
        count = count + jnp.where(ok, 1.0, 0.0)
    return jnp.where(count > 0, jnp.log2(jnp.maximum(count, 1.0)), NEG_INF)


def _dil_attn_kernel(q_ref, k_ref, v_ref, g_ref, o_ref, bias_ref, *, tq, win):
    i = pl.program_id(1)
    qi = lax.broadcasted_iota(jnp.int32, (tq, win), 0)
    kj = lax.broadcasted_iota(jnp.int32, (tq, win), 1)

    @pl.when((pl.program_id(0) == 0) & (i == 0))
    def _():
        bias_ref[...] = _dil_bias(qi + DIL_BACK - kj)

    def attend(start, bias):
        kw = k_ref[pl.ds(start, win), :]
        vw = v_ref[pl.ds(start, win), :]
        s = lax.dot_general(q_ref[...], kw, NT_DIMS, preferred_element_type=F32) + bias
        p = jnp.exp2(s - jnp.max(s, axis=-1, keepdims=True))
        o = jnp.dot(p.astype(vw.dtype), vw, preferred_element_type=F32) / jnp.sum(p, axis=-1, keepdims=True)
        y = o * lax.rsqrt(jnp.mean(o * o, axis=-1, keepdims=True) + NORM_EPS) * g_ref[...]
        o_ref[...] = y.astype(o_ref.dtype)

    n_clamped = DIL_BACK // tq

    @pl.when(i >= n_clamped)
    def _():
        attend(pl.multiple_of(i * tq - DIL_BACK, tq), bias_ref[...])

    @pl.when(i < n_clamped)
    def _():
        attend(0, _dil_bias(i * tq + qi - kj))


def dil_attn(proj, out_g, n_heads, col0):
    S = proj.shape[0]
    for window, dilation in DILATED_CONFIGS:
        assert window % dilation == 0 and dilation & (dilation - 1) == 0
    tq = _tile(S, 256)
    win = DIL_BACK + tq
    assert S >= win and DIL_BACK % tq == 0
    c0 = col0 // HEAD_DIM
    return pl.pallas_call(
        functools.partial(_dil_attn_kernel, tq=tq, win=win),
        grid=(n_heads, S // tq),
        in_specs=[pl.BlockSpec((tq, HEAD_DIM), lambda h, i: (i, c0 + h)),
                  pl.BlockSpec((S, HEAD_DIM), lambda h, i: (0, c0 + n_heads + h)),
                  pl.BlockSpec((S, HEAD_DIM), lambda h, i: (0, c0 + 2 * n_heads + h)),
                  pl.BlockSpec((1, HEAD_DIM), lambda h, i: (0, h))],
        out_specs=pl.BlockSpec((tq, HEAD_DIM), lambda h, i: (i, h)),
        out_shape=jax.ShapeDtypeStruct((S, n_heads * HEAD_DIM), BF16),
        scratch_shapes=[pltpu.VMEM((tq, win), F32)],
        compiler_params=_params(("arbitrary", "arbitrary"), 48),
        name="dil_attn",
    )(proj, proj, proj, out_g.reshape(1, n_heads * HEAD_DIM))


def _out_proj_kernel(ya_ref, yb_ref, w_ref, x_ref, gate_ref, g_ref, sh_ref, sc_ref,
                     x1_ref, h_ref, acc_ref, *, nka):
    k = pl.program_id(1)

    @pl.when(k == 0)
    def _():
        acc_ref[...] = jnp.zeros(acc_ref.shape, F32)

    @pl.when(k < nka)
    def _():
        acc_ref[...] += jnp.dot(ya_ref[...], w_ref[...], preferred_element_type=F32)

    @pl.when(k >= nka)
    def _():
        acc_ref[...] += jnp.dot(yb_ref[...], w_ref[...], preferred_element_type=F32)

    @pl.when(k == pl.num_programs(1) - 1)
    def _():
        x1 = x_ref[...] + gate_ref[...] * acc_ref[...]
        x1_ref[...] = x1
        h_ref[...] = _rms_mod(x1, g_ref[...], sc_ref[...], sh_ref[...]).astype(h_ref.dtype)


def out_proj(ya, yb, w_out, x, mod, gate_idx, norm_g, shift_idx, scale_idx):
    S, D = x.shape
    ka, kb = ya.shape[1], yb.shape[1]
    tm = _tile(S, 256)
    tk = _tile(math.gcd(ka, kb), 1024)
    nka, nkb = ka // tk, kb // tk
    row = lambda idx: pl.BlockSpec((1, D), lambda i, k: (0, idx))
    return pl.pallas_call(
        functools.partial(_out_proj_kernel, nka=nka),
        grid=(S // tm, nka + nkb),
        in_specs=[pl.BlockSpec((tm, tk), lambda i, k: (i, jnp.minimum(k, nka - 1))),
                  pl.BlockSpec((tm, tk), lambda i, k: (i, jnp.maximum(k - nka, 0))),
                  pl.BlockSpec((tk, D), lambda i, k: (k, 0)),
                  pl.BlockSpec((tm, D), lambda i, k: (i, 0)),
                  row(gate_idx), row(0), row(shift_idx), row(scale_idx)],
        out_specs=[pl.BlockSpec((tm, D), lambda i, k: (i, 0)), pl.BlockSpec((tm, D), lambda i, k: (i, 0))],
        out_shape=[jax.ShapeDtypeStruct((S, D), F32), jax.ShapeDtypeStruct((S, D), BF16)],
        scratch_shapes=[pltpu.VMEM((tm, D), F32)],
        compiler_params=_params(("arbitrary", "arbitrary"), 48),
        name="out_proj",
    )(ya, yb, w_out, x, mod, norm_g.reshape(1, D), mod, mod)


def _top16(x):
    rows = lax.broadcasted_iota(jnp.int32, x.shape, 0)
    rank = jnp.full(x.shape, PEER_TOPK, jnp.int32)
    vals = []
    for r in range(PEER_TOPK):
        m = jnp.max(x, axis=0, keepdims=True)
        idx = jnp.min(jnp.where(x == m, rows, x.shape[0]), axis=0, keepdims=True)
        hit = rows == idx
        rank = jnp.where(hit, r, rank)
        x = jnp.where(hit, -jnp.inf, x)
        vals.append(m)
    return jnp.concatenate(vals, axis=0), rank


def _peer_router_kernel(h_ref, wq_ref, sk_ref, rank1_ref, cnt0_ref, e0_ref, e1_ref):
    q = jnp.dot(h_ref[...], wq_ref[...], preferred_element_type=F32)
    half = q.shape[1] // 2
    s = [lax.dot_general(sk_ref[0, p], q[:, p * half:(p + 1) * half], NT_DIMS,
                         precision=lax.Precision.HIGHEST, preferred_element_type=F32)
         for p in range(2)]
    a, rank0 = _top16(s[0])
    b, rank1 = _top16(s[1])
    n_q = [PEER_TOPK // (p + 1) for p in range(PEER_TOPK)]
    pad = -sum(n_q) % 8
    cand = jnp.concatenate([a[p:p + 1] + b[:n_q[p]] for p in range(PEER_TOPK)]
                           + [jnp.full((pad, a.shape[1]), -jnp.inf, F32)], axis=0)
    fin, sel_rank = _top16(cand)
    sel = jnp.where(sel_rank < PEER_TOPK, 1.0, 0.0)
    starts = [sum(n_q[:p]) for p in range(PEER_TOPK)]
    cnt = jnp.concatenate([jnp.sum(sel[starts[p]:starts[p] + n_q[p]], axis=0, keepdims=True)
                           for p in range(PEER_TOPK)], axis=0)
    cnt0 = jnp.zeros(s[0].shape, F32)
    for p in range(PEER_TOPK):
        cnt0 = jnp.where(rank0 == p, cnt[p:p + 1], cnt0)
    z = jnp.sum(jnp.exp(fin - fin[0:1]), axis=0, keepdims=True)
    rank1_ref[0] = rank1.astype(F32)
    cnt0_ref[0] = cnt0
    e0_ref[0] = jnp.exp(s[0] - a[0:1])
    e1_ref[0] = jnp.exp(s[1] - b[0:1]) / z


def peer_router(h, w_q, subkeys):
    S, D = h.shape
    n_heads = subkeys.shape[0]
    qd = w_q.shape[1] // n_heads
    T = _tile(S, 512)
    ospec = pl.BlockSpec((1, N_KEYS, T), lambda t, hd: (hd, 0, t))
    return pl.pallas_call(
        _peer_router_kernel,
        grid=(S // T, n_heads),
        in_specs=[pl.BlockSpec((T, D), lambda t, hd: (t, 0)),
                  pl.BlockSpec((D, qd), lambda t, hd: (0, hd)),
                  pl.BlockSpec((1, 2, N_KEYS, qd // 2), lambda t, hd: (hd, 0, 0, 0))],
        out_specs=[ospec] * 4,
        out_shape=[jax.ShapeDtypeStruct((n_heads, N_KEYS, S), F32)] * 4,
        compiler_params=_params(("arbitrary", "arbitrary"), 48),
        name="peer_router",
    )(h, w_q, subkeys)


def _gelu(x):
    return 0.5 * x * (1.0 + lax.erf(x * (0.5 ** 0.5)))


PEER_GI = 8
PEER_GJ = 64


def _peer_expert_kernel(h_ref, u_ref, v_ref, rank1_ref, cnt0_ref, e0_ref, e1_ref, o_ref, w_ref):
    ib, jb = pl.program_id(1), pl.program_id(2)
    gi, gj, D = u_ref.shape
    n_heads, _, T = rank1_ref.shape
    i0 = pl.multiple_of(ib * gi, gi)
    j0 = pl.multiple_of(jb * gj, gj)

    @pl.when((ib == 0) & (jb == 0))
    def _():
        o_ref[...] = jnp.zeros(o_ref.shape, F32)

    u = u_ref[...].reshape(gi * gj, D)
    act = _gelu(lax.dot_general(u, h_ref[...], NT_DIMS, preferred_element_type=F32))
    for lc in range(T // LANES):
        lsl = slice(lc * LANES, (lc + 1) * LANES)
        c0 = [cnt0_ref[hd, pl.ds(i0, gi), lsl] for hd in range(n_heads)]
        e0 = [e0_ref[hd, pl.ds(i0, gi), lsl] for hd in range(n_heads)]
        for ii in range(gi):
            rsl = slice(ii * gj, (ii + 1) * gj)
            g = jnp.zeros((gj, LANES), F32)
            for hd in range(n_heads):
                r1 = rank1_ref[hd, pl.ds(j0, gj), lsl]
                e1 = e1_ref[hd, pl.ds(j0, gj), lsl]
                g = g + jnp.where(r1 < c0[hd][ii:ii + 1], e1 * e0[hd][ii:ii + 1], 0.0)
            w_ref[rsl, lsl] = (g * act[rsl, lsl]).astype(w_ref.dtype)
    v = v_ref[...].reshape(gi * gj, D)
    o_ref[...] += lax.dot_general(w_ref[...], v, TN_DIMS, preferred_element_type=F32)


def peer_experts(h, u, v, tables):
    S, D = h.shape
    n_heads = tables[0].shape[0]
    T = _tile(S, 512)
    u3 = u.reshape(N_KEYS, N_KEYS, D)
    v3 = v.reshape(N_KEYS, N_KEYS, D)
    wspec = pl.BlockSpec((PEER_GI, PEER_GJ, D), lambda t, i, j: (i, j, 0))
    once = pl.Buffered(1)
    tspec = pl.BlockSpec((n_heads, N_KEYS, T), lambda t, i, j: (0, 0, t), pipeline_mode=once)
    return pl.pallas_call(
        _peer_expert_kernel,
        grid=(S // T, N_KEYS // PEER_GI, N_KEYS // PEER_GJ),
        in_specs=[pl.BlockSpec((T, D), lambda t, i, j: (t, 0), pipeline_mode=once),
                  wspec, wspec, tspec, tspec, tspec, tspec],
        out_specs=pl.BlockSpec((T, D), lambda t, i, j: (t, 0)),
        out_shape=jax.ShapeDtypeStruct((S, D), F32),
        scratch_shapes=[pltpu.VMEM((PEER_GI * PEER_GJ, T), BF16)],
        compiler_params=_params(("arbitrary", "arbitrary", "arbitrary"), 56),
        name="peer_experts",
    )(h, u3, v3, *tables)


def _final_kernel(x_ref, y_ref, gate_ref, g_ref, o_ref, *, normalize):
    x = x_ref[...] + gate_ref[...] * y_ref[...]
    if normalize:
        x = x * lax.rsqrt(jnp.mean(x * x, axis=-1, keepdims=True) + NORM_EPS) * g_ref[...]
    o_ref[...] = x


def final_norm(x, y, mod, gate_idx, g, normalize):
    S, D = x.shape
    tm = _tile(S, 256)
    blk = pl.BlockSpec((tm, D), lambda i: (i, 0))
    return pl.pallas_call(
        functools.partial(_final_kernel, normalize=normalize),
        grid=(S // tm,),
        in_specs=[blk, blk, pl.BlockSpec((1, D), lambda i: (0, gate_idx)), pl.BlockSpec((1, D), lambda i: (0, 0))],
        out_specs=blk,
        out_shape=jax.ShapeDtypeStruct((S, D), F32),
        compiler_params=_params(("arbitrary",), 48),
        name="final_norm",
    )(x, y, mod, g.reshape(1, D))


def kernel(x, c, positions, norm1_g, norm2_g, w_ada, b_ada, w_in, lam_q1, lam_k1, lam_q2, lam_k2,
           diff_subln_g, dil_out_g, w_out, peer_wq, peer_subkeys, peer_u, peer_v, final_g):
    B, S, D = x.shape
    depth = w_ada.shape[0]
    assert B == 1, "one sequence per call"
    diff_width = D // 2
    diff_heads = diff_width // (2 * HEAD_DIM)
    dil_heads = (D - diff_width) // HEAD_DIM
    xs = x.reshape(S, D)
    tables = rope_tables(positions.reshape(S))
    for l in range(depth):
        lam_init = 0.8 - 0.6 * math.exp(-0.3 * l)
        mod = ada_mod(c, w_ada[l], b_ada[l])
        h = norm_mod(xs, norm1_g[l], mod, 0, 1)
        proj = in_proj(h, w_in[l].astype(BF16), tables)
        lam_rows = jnp.stack([lam_q1[l], lam_k1[l], lam_q2[l], lam_k2[l]])
        y_diff = diff_attn(proj, lam_rows, diff_subln_g[l], diff_heads, lam_init)
        y_dil = dil_attn(proj, dil_out_g[l], dil_heads, 3 * diff_width)
        xs, h = out_proj(y_diff, y_dil, w_out[l].astype(BF16), xs, mod, 2, norm2_g[l], 3, 4)
        gates = peer_router(h, peer_wq[l].astype(BF16), peer_subkeys[l])
        y = peer_experts(h, peer_u[l].astype(BF16), peer_v[l].astype(BF16), gates)
        xs = final_norm(xs, y, mod, 5, final_g, normalize=(l == depth - 1))
    return xs.reshape(B, S, D)
```

```python
import functools
import math

import jax
import jax.numpy as jnp
from jax import lax
from jax.experimental import pallas as pl
from jax.experimental.pallas import tpu as pltpu

F32 = jnp.float32
BF16 = jnp.bfloat16

HEAD_DIM = 128
ROT_DIM = HEAD_DIM // 4
ROT_HALF = ROT_DIM // 2
ROPE_THETA = 500000.0
DILATED_CONFIGS = ((128, 1), (512, 4), (2048, 16))
DIL_BACK = max(w for w, _ in DILATED_CONFIGS)
PEER_HEADS = 8
N_KEYS = 128
PEER_TOPK = 16
NORM_EPS = 1e-6
NEG_INF = -1e30
QK_SCALE = HEAD_DIM ** -0.5 * math.log2(math.e)
DIFF_ROW_SPLIT = 2
LANES = 128
MIB = 1024 * 1024

NT_DIMS = (((1,), (1,)), ((), ()))
TN_DIMS = (((0,), (0,)), ((), ()))


def _params(semantics, vmem_mib):
    return pltpu.CompilerParams(dimension_semantics=semantics, vmem_limit_bytes=vmem_mib * MIB)


def _tile(n, pref):
    t = min(n, pref)
    assert n % t == 0, (n, pref)
    return t


def _ada_kernel(c_ref, w_ref, b_ref, o_ref):
    c = c_ref[...]
    ca = c * jax.nn.sigmoid(c)
    for j in range(o_ref.shape[1] // LANES):
        sl = slice(j * LANES, (j + 1) * LANES)
        o_ref[:, sl] = jnp.sum(w_ref[:, sl] * ca, axis=0, keepdims=True) + b_ref[:, sl]


def ada_mod(c, w_ada, b_ada):
    D, N = w_ada.shape
    tn = _tile(N, 1024)
    cb = jnp.broadcast_to(c.reshape(D, 1), (D, LANES))
    return pl.pallas_call(
        _ada_kernel,
        grid=(N // tn,),
        in_specs=[pl.BlockSpec((D, LANES), lambda j: (0, 0)),
                  pl.BlockSpec((D, tn), lambda j: (0, j)),
                  pl.BlockSpec((1, tn), lambda j: (0, j))],
        out_specs=pl.BlockSpec((1, tn), lambda j: (0, j)),
        out_shape=jax.ShapeDtypeStruct((1, N), F32),
        compiler_params=_params(("arbitrary",), 48),
        name="ada_mod",
    )(cb, w_ada, b_ada.reshape(1, N))


def _rope_kernel(pos_ref, freq_ref, c_ref, s1_ref, s2_ref):
    ang = pos_ref[...].astype(F32) * freq_ref[...]
    lane = lax.broadcasted_iota(jnp.int32, ang.shape, 1)
    cs = jnp.cos(ang)
    sn = jnp.sin(ang)
    c_ref[...] = jnp.where(lane < ROT_DIM, cs, 1.0)
    s1_ref[...] = jnp.where(lane < ROT_HALF, -sn, 0.0)
    s2_ref[...] = jnp.where((lane >= ROT_HALF) & (lane < ROT_DIM), sn, 0.0)


def rope_tables(positions):
    S = positions.shape[0]
    ts = _tile(S, 1024)
    inv_freq = jnp.power(ROPE_THETA, -jnp.arange(0, ROT_DIM, 2, dtype=F32) / ROT_DIM)
    freq = jnp.tile(inv_freq, LANES // ROT_HALF).reshape(1, LANES)
    spec = pl.BlockSpec((ts, LANES), lambda i: (i, 0))
    return pl.pallas_call(
        _rope_kernel,
        grid=(S // ts,),
        in_specs=[pl.BlockSpec((ts, 1), lambda i: (i, 0)), pl.BlockSpec((1, LANES), lambda i: (0, 0))],
        out_specs=[spec, spec, spec],
        out_shape=[jax.ShapeDtypeStruct((S, LANES), F32)] * 3,
        compiler_params=_params(("arbitrary",), 32),
        name="rope_tables",
    )(positions.reshape(S, 1), freq)


def _rms_mod(x, g, scale, shift):
    y = x * lax.rsqrt(jnp.mean(x * x, axis=-1, keepdims=True) + NORM_EPS) * g
    return y * (1 + scale) + shift


def _norm_mod_kernel(x_ref, g_ref, sh_ref, sc_ref, o_ref):
    o_ref[...] = _rms_mod(x_ref[...], g_ref[...], sc_ref[...], sh_ref[...]).astype(o_ref.dtype)


def norm_mod(x, g, mod, shift_idx, scale_idx):
    S, D = x.shape
    tm = _tile(S, 256)
    return pl.pallas_call(
        _norm_mod_kernel,
        grid=(S // tm,),
        in_specs=[pl.BlockSpec((tm, D), lambda i: (i, 0)),
                  pl.BlockSpec((1, D), lambda i: (0, 0)),
                  pl.BlockSpec((1, D), lambda i: (0, shift_idx)),
                  pl.BlockSpec((1, D), lambda i: (0, scale_idx))],
        out_specs=pl.BlockSpec((tm, D), lambda i: (i, 0)),
        out_shape=jax.ShapeDtypeStruct((S, D), BF16),
        compiler_params=_params(("arbitrary",), 32),
        name="norm_mod",
    )(x, g.reshape(1, D), mod, mod)


def _in_proj_kernel(h_ref, w_ref, c_ref, s1_ref, s2_ref, o_ref, *, per_sec):
    tn = o_ref.shape[1]
    acc = jnp.dot(h_ref[...], w_ref[...], preferred_element_type=F32)
    section = pl.program_id(1) // per_sec
    q_scale = jnp.where((section == 0) | (section == 2), QK_SCALE, 1.0).astype(F32)
    cs, s1, s2 = c_ref[...], s1_ref[...], s2_ref[...]
    for j in range(tn // HEAD_DIM):
        sl = slice(j * HEAD_DIM, (j + 1) * HEAD_DIM)
        t = acc[:, sl]
        r = (t * cs + pltpu.roll(t, HEAD_DIM - ROT_HALF, 1) * s1 + pltpu.roll(t, ROT_HALF, 1) * s2)
        o_ref[:, sl] = (r * q_scale).astype(o_ref.dtype)


def in_proj(h, w_in, tables):
    S, D = h.shape
    sec_width = w_in.shape[1] // 6
    tm = _tile(S, 1024)
    tn = _tile(sec_width, 512)
    per_sec = sec_width // tn
    src = lambda j: j + jnp.where(j >= 2 * per_sec, per_sec, 0)
    tspec = pl.BlockSpec((tm, LANES), lambda i, j: (i, 0))
    return pl.pallas_call(
        functools.partial(_in_proj_kernel, per_sec=per_sec),
        grid=(S // tm, 4 * per_sec),
        in_specs=[pl.BlockSpec((tm, D), lambda i, j: (i, 0)),
                  pl.BlockSpec((D, tn), lambda i, j: (0, src(j))),
                  tspec, tspec, tspec],
        out_specs=pl.BlockSpec((tm, tn), lambda i, j: (i, j)),
        out_shape=jax.ShapeDtypeStruct((S, 4 * sec_width), BF16),
        compiler_params=_params(("arbitrary", "arbitrary"), 48),
        name="in_proj",
    )(h, w_in, *tables)


def _v_proj_t_kernel(w_ref, h_ref, o_ref):
    o_ref[...] = lax.dot_general(w_ref[...], h_ref[...], NT_DIMS, preferred_element_type=F32).astype(o_ref.dtype)


def v_proj_t(w_t, h):
    N, D = w_t.shape
    S = h.shape[0]
    tn = _tile(N, 512)
    ts = _tile(S, 1024)
    return pl.pallas_call(
        _v_proj_t_kernel,
        grid=(S // ts, N // tn),
        in_specs=[pl.BlockSpec((tn, D), lambda i, j: (j, 0)), pl.BlockSpec((ts, D), lambda i, j: (i, 0))],
        out_specs=pl.BlockSpec((tn, ts), lambda i, j: (j, i)),
        out_shape=jax.ShapeDtypeStruct((N, S), BF16),
        compiler_params=_params(("arbitrary", "arbitrary"), 48),
        name="v_proj_t",
    )(w_t, h)


def _diff_attn_kernel(lam_ref, q_ref, k_ref, vt_ref, g_ref, o_ref,
                      m1, l1, a1, m2, l2, a2, *, tq, lam_init):
    i = pl.program_id(1)
    stats = ((m1, l1, a1), (m2, l2, a2))
    for m, l, a in stats:
        m[...] = jnp.full(m.shape, -jnp.inf, F32)
        l[...] = jnp.zeros(l.shape, F32)
        a[...] = jnp.zeros(a.shape, F32)

    def chunk(kv, masked):
        k0 = pl.multiple_of(kv * tq, tq)
        vt = vt_ref[:, pl.ds(k0, tq)]

        def scores(c):
            sl = slice(c * HEAD_DIM, (c + 1) * HEAD_DIM)
            return lax.dot_general(k_ref[pl.ds(k0, tq), sl], q_ref[:, sl], NT_DIMS, preferred_element_type=F32)

        s_next = scores(0)
        for c, (m, l, a) in enumerate(stats):
            s = s_next
            if c == 0:
                s_next = scores(1)
            if masked:
                keep = (lax.broadcasted_iota(jnp.int32, (tq, tq), 0) <= lax.broadcasted_iota(jnp.int32, (tq, tq), 1))
                s = jnp.where(keep, s, NEG_INF)
            m_prev = m[...]
            m_new = jnp.maximum(m_prev, jnp.max(s, axis=0, keepdims=True))
            alpha = jnp.exp2(m_prev - m_new)
            p = jnp.exp2(s - m_new)
            l[...] = alpha * l[...] + jnp.sum(p, axis=0, keepdims=True)
            a[...] = alpha * a[...] + jnp.dot(vt, p.astype(vt.dtype), preferred_element_type=F32)
            m[...] = m_new

    def body(kv, carry):
        chunk(kv, False)
        return carry

    lax.fori_loop(0, i, body, 0)
    chunk(i, True)

    lp = lam_ref[...]
    lam = (jnp.exp(jnp.sum(lp[0:1] * lp[1:2], axis=-1, keepdims=True))
           - jnp.exp(jnp.sum(lp[2:3] * lp[3:4], axis=-1, keepdims=True)) + lam_init)
    o = (a1[...] / l1[...] - lam * (a2[...] / l2[...])).T
    y = o * lax.rsqrt(jnp.mean(o * o, axis=-1, keepdims=True) + NORM_EPS) * g_ref[...]
    o_ref[...] = (y * (1 - lam_init)).astype(o_ref.dtype)


def diff_attn(proj, v_t, lam_rows, subln_g, n_heads, lam_init):
    S = proj.shape[0]
    tq = _tile(S, 512)
    hw = 2 * HEAD_DIM
    return pl.pallas_call(
        functools.partial(_diff_attn_kernel, tq=tq, lam_init=lam_init),
        grid=(n_heads, S // tq),
        in_specs=[pl.BlockSpec((4, HEAD_DIM), lambda h, i: (0, 0)),
                  pl.BlockSpec((tq, hw), lambda h, i: (i, h)),
                  pl.BlockSpec((S, hw), lambda h, i: (0, n_heads + h)),
                  pl.BlockSpec((hw, S), lambda h, i: (h, 0)),
                  pl.BlockSpec((1, hw), lambda h, i: (0, 0))],
        out_specs=pl.BlockSpec((tq, hw), lambda h, i: (i, h)),
        out_shape=jax.ShapeDtypeStruct((S, n_heads * hw), BF16),
        scratch_shapes=[pltpu.VMEM((1, tq), F32), pltpu.VMEM((1, tq), F32), pltpu.VMEM((hw, tq), F32),
                        pltpu.VMEM((1, tq), F32), pltpu.VMEM((1, tq), F32), pltpu.VMEM((hw, tq), F32)],
        compiler_params=_params(("arbitrary", "arbitrary"), 48),
        name="diff_attn",
    )(lam_rows, proj, proj, v_t, subln_g.reshape(1, hw))


def _dil_bias(delta):
    count = jnp.zeros(delta.shape, F32)
    for window, dilation in DILATED_CONFIGS:
        ok = (delta >= 0) & (delta <= window) & ((delta & (dilation - 1)) == 0)
        count = count + jnp.where(ok, 1.0, 0.0)
    return jnp.where(count > 0, jnp.log2(jnp.maximum(count, 1.0)), NEG_INF)


DIL_HEADS_PER_STEP = 2


def _dil_attn_kernel(q_ref, k_ref, vt_ref, g_ref, o_ref, bias_ref, *, tq, win):
    i = pl.program_id(1)

    @pl.when((pl.program_id(0) == 0) & (i == 0))
    def _():
        kj = lax.broadcasted_iota(jnp.int32, bias_ref.shape, 0)
        qi = lax.broadcasted_iota(jnp.int32, bias_ref.shape, 1)
        bias_ref[...] = _dil_bias(qi + DIL_BACK - kj)

    start = pl.multiple_of(jnp.maximum(i * tq - DIL_BACK, 0), tq)
    row0 = pl.multiple_of(jnp.maximum(DIL_BACK - i * tq, 0), tq)
    bias = bias_ref[pl.ds(row0, win), :]
    heads = q_ref.shape[1] // HEAD_DIM

    def scores(hd):
        sl = slice(hd * HEAD_DIM, (hd + 1) * HEAD_DIM)
        return lax.dot_general(k_ref[pl.ds(start, win), sl], q_ref[:, sl], NT_DIMS,
                               preferred_element_type=F32) + bias

    s_next = scores(0)
    for hd in range(heads):
        sl = slice(hd * HEAD_DIM, (hd + 1) * HEAD_DIM)
        s = s_next
        if hd + 1 < heads:
            s_next = scores(hd + 1)
        p = jnp.exp2(s - jnp.max(s, axis=0, keepdims=True))
        vtw = vt_ref[sl, pl.ds(start, win)]
        o_t = jnp.dot(vtw, p.astype(vtw.dtype), preferred_element_type=F32) / jnp.sum(p, axis=0, keepdims=True)
        o = o_t.T
        y = o * lax.rsqrt(jnp.mean(o * o, axis=-1, keepdims=True) + NORM_EPS) * g_ref[:, sl]
        o_ref[:, sl] = y.astype(o_ref.dtype)


def dil_attn(proj, v_t, out_g, n_heads, col0, row0):
    S = proj.shape[0]
    for window, dilation in DILATED_CONFIGS:
        assert window % dilation == 0 and dilation & (dilation - 1) == 0
    tq = _tile(S, 256)
    win = DIL_BACK + tq
    hp = DIL_HEADS_PER_STEP
    hw = hp * HEAD_DIM
    assert S >= win and DIL_BACK % tq == 0 and win % LANES == 0
    assert n_heads % hp == 0 and col0 % hw == 0 and row0 % hw == 0
    c0 = col0 // hw
    r0 = row0 // hw
    ng = n_heads // hp
    return pl.pallas_call(
        functools.partial(_dil_attn_kernel, tq=tq, win=win),
        grid=(ng, S // tq),
        in_specs=[pl.BlockSpec((tq, hw), lambda h, i: (i, c0 + h)),
                  pl.BlockSpec((S, hw), lambda h, i: (0, c0 + ng + h)),
                  pl.BlockSpec((hw, S), lambda h, i: (r0 + h, 0)),
                  pl.BlockSpec((1, hw), lambda h, i: (0, h))],
        out_specs=pl.BlockSpec((tq, hw), lambda h, i: (i, h)),
        out_shape=jax.ShapeDtypeStruct((S, n_heads * HEAD_DIM), BF16),
        scratch_shapes=[pltpu.VMEM((win + DIL_BACK, tq), F32)],
        compiler_params=_params(("arbitrary", "arbitrary"), 48),
        name="dil_attn",
    )(proj, proj, v_t, out_g.reshape(1, n_heads * HEAD_DIM))


def _out_proj_kernel(ya_ref, yb_ref, w_ref, x_ref, gate_ref, g_ref, sh_ref, sc_ref,
                     x1_ref, h_ref, acc_ref, *, nka):
    k = pl.program_id(1)

    @pl.when(k == 0)
    def _():
        acc_ref[...] = jnp.zeros(acc_ref.shape, F32)

    @pl.when(k < nka)
    def _():
        acc_ref[...] += jnp.dot(ya_ref[...], w_ref[...], preferred_element_type=F32)

    @pl.when(k >= nka)
    def _():
        acc_ref[...] += jnp.dot(yb_ref[...], w_ref[...], preferred_element_type=F32)

    @pl.when(k == pl.num_programs(1) - 1)
    def _():
        x1 = x_ref[...] + gate_ref[...] * acc_ref[...]
        x1_ref[...] = x1
        h_ref[...] = _rms_mod(x1, g_ref[...], sc_ref[...], sh_ref[...]).astype(h_ref.dtype)


def out_proj(ya, yb, w_out, x, mod, gate_idx, norm_g, shift_idx, scale_idx):
    S, D = x.shape
    ka, kb = ya.shape[1], yb.shape[1]
    tm = _tile(S, 256)
    tk = _tile(math.gcd(ka, kb), 1024)
    nka, nkb = ka // tk, kb // tk
    row = lambda idx: pl.BlockSpec((1, D), lambda i, k: (0, idx))
    return pl.pallas_call(
        functools.partial(_out_proj_kernel, nka=nka),
        grid=(S // tm, nka + nkb),
        in_specs=[pl.BlockSpec((tm, tk), lambda i, k: (i, jnp.minimum(k, nka - 1))),
                  pl.BlockSpec((tm, tk), lambda i, k: (i, jnp.maximum(k - nka, 0))),
                  pl.BlockSpec((tk, D), lambda i, k: (k, 0)),
                  pl.BlockSpec((tm, D), lambda i, k: (i, 0)),
                  row(gate_idx), row(0), row(shift_idx), row(scale_idx)],
        out_specs=[pl.BlockSpec((tm, D), lambda i, k: (i, 0)), pl.BlockSpec((tm, D), lambda i, k: (i, 0))],
        out_shape=[jax.ShapeDtypeStruct((S, D), F32), jax.ShapeDtypeStruct((S, D), BF16)],
        scratch_shapes=[pltpu.VMEM((tm, D), F32)],
        compiler_params=_params(("arbitrary", "arbitrary"), 48),
        name="out_proj",
    )(ya, yb, w_out, x, mod, norm_g.reshape(1, D), mod, mod)


def _top16(x):
    rows = lax.broadcasted_iota(jnp.int32, x.shape, 0)
    rank = jnp.full(x.shape, PEER_TOPK, jnp.int32)
    vals = []
    for r in range(PEER_TOPK):
        m = jnp.max(x, axis=0, keepdims=True)
        idx = jnp.min(jnp.where(x == m, rows, x.shape[0]), axis=0, keepdims=True)
        hit = rows == idx
        rank = jnp.where(hit, r, rank)
        x = jnp.where(hit, -jnp.inf, x)
        vals.append(m)
    return jnp.concatenate(vals, axis=0), rank


def _peer_router_kernel(h_ref, wq_ref, sk_ref, rank1_ref, cnt0_ref, e0_ref, e1_ref):
    q = jnp.dot(h_ref[...], wq_ref[...], preferred_element_type=F32)
    half = q.shape[1] // 2
    s = [lax.dot_general(sk_ref[0, p], q[:, p * half:(p + 1) * half], NT_DIMS,
                         precision=lax.Precision.HIGHEST, preferred_element_type=F32)
         for p in range(2)]
    a, rank0 = _top16(s[0])
    b, rank1 = _top16(s[1])
    n_q = [PEER_TOPK // (p + 1) for p in range(PEER_TOPK)]
    pad = -sum(n_q) % 8
    cand = jnp.concatenate([a[p:p + 1] + b[:n_q[p]] for p in range(PEER_TOPK)]
                           + [jnp.full((pad, a.shape[1]), -jnp.inf, F32)], axis=0)
    fin, sel_rank = _top16(cand)
    sel = jnp.where(sel_rank < PEER_TOPK, 1.0, 0.0)
    starts = [sum(n_q[:p]) for p in range(PEER_TOPK)]
    cnt = jnp.concatenate([jnp.sum(sel[starts[p]:starts[p] + n_q[p]], axis=0, keepdims=True)
                           for p in range(PEER_TOPK)], axis=0)
    cnt0 = jnp.zeros(s[0].shape, F32)
    for p in range(PEER_TOPK):
        cnt0 = jnp.where(rank0 == p, cnt[p:p + 1], cnt0)
    z = jnp.sum(jnp.exp(fin - fin[0:1]), axis=0, keepdims=True)
    rank1_ref[0] = rank1.astype(F32)
    cnt0_ref[0] = cnt0
    e0_ref[0] = jnp.exp(s[0] - a[0:1])
    e1_ref[0] = jnp.exp(s[1] - b[0:1]) / z


def peer_router(h, w_q, subkeys):
    S, D = h.shape
    n_heads = subkeys.shape[0]
    qd = w_q.shape[1] // n_heads
    T = _tile(S, 512)
    ospec = pl.BlockSpec((1, N_KEYS, T), lambda t, hd: (hd, 0, t))
    return pl.pallas_call(
        _peer_router_kernel,
        grid=(S // T, n_heads),
        in_specs=[pl.BlockSpec((T, D), lambda t, hd: (t, 0)),
                  pl.BlockSpec((D, qd), lambda t, hd: (0, hd)),
                  pl.BlockSpec((1, 2, N_KEYS, qd // 2), lambda t, hd: (hd, 0, 0, 0))],
        out_specs=[ospec] * 4,
        out_shape=[jax.ShapeDtypeStruct((n_heads, N_KEYS, S), F32)] * 4,
        compiler_params=_params(("arbitrary", "arbitrary"), 48),
        name="peer_router",
    )(h, w_q, subkeys)


def _gelu(x):
    return 0.5 * x * (1.0 + lax.erf(x * (0.5 ** 0.5)))


PEER_GI = 8
PEER_GJ = 64


def _peer_expert_kernel(h_ref, u_ref, v_ref, rank1_ref, cnt0_ref, e0_ref, e1_ref, o_ref, w_ref):
    ib, jb = pl.program_id(1), pl.program_id(2)
    gi, gj, D = u_ref.shape
    n_heads, _, T = rank1_ref.shape
    i0 = pl.multiple_of(ib * gi, gi)
    j0 = pl.multiple_of(jb * gj, gj)

    @pl.when((ib == 0) & (jb == 0))
    def _():
        o_ref[...] = jnp.zeros(o_ref.shape, F32)

    u = u_ref[...].reshape(gi * gj, D)
    act = _gelu(lax.dot_general(u, h_ref[...], NT_DIMS, preferred_element_type=F32))
    for lc in range(T // LANES):
        lsl = slice(lc * LANES, (lc + 1) * LANES)
        c0 = [cnt0_ref[hd, pl.ds(i0, gi), lsl] for hd in range(n_heads)]
        e0 = [e0_ref[hd, pl.ds(i0, gi), lsl] for hd in range(n_heads)]
        for ii in range(gi):
            rsl = slice(ii * gj, (ii + 1) * gj)
            g = jnp.zeros((gj, LANES), F32)
            for hd in range(n_heads):
                r1 = rank1_ref[hd, pl.ds(j0, gj), lsl]
                e1 = e1_ref[hd, pl.ds(j0, gj), lsl]
                g = g + jnp.where(r1 < c0[hd][ii:ii + 1], e1 * e0[hd][ii:ii + 1], 0.0)
            w_ref[rsl, lsl] = (g * act[rsl, lsl]).astype(w_ref.dtype)
    v = v_ref[...].reshape(gi * gj, D)
    o_ref[...] += lax.dot_general(w_ref[...], v, TN_DIMS, preferred_element_type=F32)


def peer_experts(h, u, v, tables):
    S, D = h.shape
    n_heads = tables[0].shape[0]
    T = _tile(S, 512)
    u3 = u.reshape(N_KEYS, N_KEYS, D)
    v3 = v.reshape(N_KEYS, N_KEYS, D)
    wspec = pl.BlockSpec((PEER_GI, PEER_GJ, D), lambda t, i, j: (i, j, 0))
    once = pl.Buffered(1)
    tspec = pl.BlockSpec((n_heads, N_KEYS, T), lambda t, i, j: (0, 0, t), pipeline_mode=once)
    return pl.pallas_call(
        _peer_expert_kernel,
        grid=(S // T, N_KEYS // PEER_GI, N_KEYS // PEER_GJ),
        in_specs=[pl.BlockSpec((T, D), lambda t, i, j: (t, 0), pipeline_mode=once),
                  wspec, wspec, tspec, tspec, tspec, tspec],
        out_specs=pl.BlockSpec((T, D), lambda t, i, j: (t, 0)),
        out_shape=jax.ShapeDtypeStruct((S, D), F32),
        scratch_shapes=[pltpu.VMEM((PEER_GI * PEER_GJ, T), BF16)],
        compiler_params=_params(("arbitrary", "arbitrary", "arbitrary"), 56),
        name="peer_experts",
    )(h, u3, v3, *tables)


def _final_kernel(x_ref, y_ref, gate_ref, g_ref, o_ref, *, normalize):
    x = x_ref[...] + gate_ref[...] * y_ref[...]
    if normalize:
        x = x * lax.rsqrt(jnp.mean(x * x, axis=-1, keepdims=True) + NORM_EPS) * g_ref[...]
    o_ref[...] = x


def final_norm(x, y, mod, gate_idx, g, normalize):
    S, D = x.shape
    tm = _tile(S, 256)
    blk = pl.BlockSpec((tm, D), lambda i: (i, 0))
    return pl.pallas_call(
        functools.partial(_final_kernel, normalize=normalize),
        grid=(S // tm,),
        in_specs=[blk, blk, pl.BlockSpec((1, D), lambda i: (0, gate_idx)), pl.BlockSpec((1, D), lambda i: (0, 0))],
        out_specs=blk,
        out_shape=jax.ShapeDtypeStruct((S, D), F32),
        compiler_params=_params(("arbitrary",), 48),
        name="final_norm",
    )(x, y, mod, g.reshape(1, D))


def kernel(x, c, positions, norm1_g, norm2_g, w_ada, b_ada, w_in, lam_q1, lam_k1, lam_q2, lam_k2,
           diff_subln_g, dil_out_g, w_out, peer_wq, peer_subkeys, peer_u, peer_v, final_g):
    B, S, D = x.shape
    depth = w_ada.shape[0]
    assert B == 1, "one sequence per call"
    diff_width = D // 2
    diff_heads = diff_width // (2 * HEAD_DIM)
    dil_heads = (D - diff_width) // HEAD_DIM
    xs = x.reshape(S, D)
    tables = rope_tables(positions.reshape(S))
    for l in range(depth):
        lam_init = 0.8 - 0.6 * math.exp(-0.3 * l)
        mod = ada_mod(c, w_ada[l], b_ada[l])
        h = norm_mod(xs, norm1_g[l], mod, 0, 1)
        dil_width = D - diff_width
        proj = in_proj(h, w_in[l].astype(BF16), tables)
        w_v = jnp.concatenate([w_in[l][:, 2 * diff_width:3 * diff_width], w_in[l][:, 3 * diff_width + 2 * dil_width:]],
                              axis=1)
        v_t = v_proj_t(w_v.T.astype(BF16), h)
        lam_rows = jnp.stack([lam_q1[l], lam_k1[l], lam_q2[l], lam_k2[l]])
        y_diff = diff_attn(proj, v_t, lam_rows, diff_subln_g[l], diff_heads, lam_init)
        y_dil = dil_attn(proj, v_t, dil_out_g[l], dil_heads, 2 * diff_width, diff_width)
        xs, h = out_proj(y_diff, y_dil, w_out[l].astype(BF16), xs, mod, 2, norm2_g[l], 3, 4)
        gates = peer_router(h, peer_wq[l].astype(BF16), peer_subkeys[l])
        y = peer_experts(h, peer_u[l].astype(BF16), peer_v[l].astype(BF16), gates)
        xs = final_norm(xs, y, mod, 5, final_g, normalize=(l == depth - 1))
    return xs.reshape(B, S, D)
```

```python
import functools
import math

import jax
import jax.numpy as jnp
from jax import lax
from jax.experimental import pallas as pl
from jax.experimental.pallas import tpu as pltpu

F32 = jnp.float32
BF16 = jnp.bfloat16

HEAD_DIM = 128
ROT_DIM = HEAD_DIM // 4
ROT_HALF = ROT_DIM // 2
ROPE_THETA = 500000.0
DILATED_CONFIGS = ((128, 1), (512, 4), (2048, 16))
DIL_BACK = max(w for w, _ in DILATED_CONFIGS)
PEER_HEADS = 8
N_KEYS = 128
PEER_TOPK = 16
NORM_EPS = 1e-6
NEG_INF = -1e30
QK_SCALE = HEAD_DIM ** -0.5 * math.log2(math.e)
DIFF_ROW_SPLIT = 2
LANES = 128
MIB = 1024 * 1024

NT_DIMS = (((1,), (1,)), ((), ()))
TN_DIMS = (((0,), (0,)), ((), ()))


def _params(semantics, vmem_mib):
    return pltpu.CompilerParams(dimension_semantics=semantics, vmem_limit_bytes=vmem_mib * MIB)


def _tile(n, pref):
    t = min(n, pref)
    assert n % t == 0, (n, pref)
    return t


def _ada_kernel(c_ref, w_ref, b_ref, o_ref):
    c = c_ref[...]
    ca = c * jax.nn.sigmoid(c)
    for j in range(o_ref.shape[1] // LANES):
        sl = slice(j * LANES, (j + 1) * LANES)
        o_ref[:, sl] = jnp.sum(w_ref[:, sl] * ca, axis=0, keepdims=True) + b_ref[:, sl]


def ada_mod(c, w_ada, b_ada):
    D, N = w_ada.shape
    tn = _tile(N, 1024)
    cb = jnp.broadcast_to(c.reshape(D, 1), (D, LANES))
    return pl.pallas_call(
        _ada_kernel,
        grid=(N // tn,),
        in_specs=[pl.BlockSpec((D, LANES), lambda j: (0, 0)),
                  pl.BlockSpec((D, tn), lambda j: (0, j)),
                  pl.BlockSpec((1, tn), lambda j: (0, j))],
        out_specs=pl.BlockSpec((1, tn), lambda j: (0, j)),
        out_shape=jax.ShapeDtypeStruct((1, N), F32),
        compiler_params=_params(("arbitrary",), 48),
        name="ada_mod",
    )(cb, w_ada, b_ada.reshape(1, N))


def _rope_kernel(pos_ref, freq_ref, c_ref, s1_ref, s2_ref):
    ang = pos_ref[...].astype(F32) * freq_ref[...]
    lane = lax.broadcasted_iota(jnp.int32, ang.shape, 1)
    cs = jnp.cos(ang)
    sn = jnp.sin(ang)
    c_ref[...] = jnp.where(lane < ROT_DIM, cs, 1.0)
    s1_ref[...] = jnp.where(lane < ROT_HALF, -sn, 0.0)
    s2_ref[...] = jnp.where((lane >= ROT_HALF) & (lane < ROT_DIM), sn, 0.0)


def rope_tables(positions):
    S = positions.shape[0]
    ts = _tile(S, 1024)
    inv_freq = jnp.power(ROPE_THETA, -jnp.arange(0, ROT_DIM, 2, dtype=F32) / ROT_DIM)
    freq = jnp.tile(inv_freq, LANES // ROT_HALF).reshape(1, LANES)
    spec = pl.BlockSpec((ts, LANES), lambda i: (i, 0))
    return pl.pallas_call(
        _rope_kernel,
        grid=(S // ts,),
        in_specs=[pl.BlockSpec((ts, 1), lambda i: (i, 0)), pl.BlockSpec((1, LANES), lambda i: (0, 0))],
        out_specs=[spec, spec, spec],
        out_shape=[jax.ShapeDtypeStruct((S, LANES), F32)] * 3,
        compiler_params=_params(("arbitrary",), 32),
        name="rope_tables",
    )(positions.reshape(S, 1), freq)


def _rms_mod(x, g, scale, shift):
    y = x * lax.rsqrt(jnp.mean(x * x, axis=-1, keepdims=True) + NORM_EPS) * g
    return y * (1 + scale) + shift


def _norm_mod_kernel(x_ref, g_ref, sh_ref, sc_ref, o_ref):
    o_ref[...] = _rms_mod(x_ref[...], g_ref[...], sc_ref[...], sh_ref[...]).astype(o_ref.dtype)


def norm_mod(x, g, mod, shift_idx, scale_idx):
    S, D = x.shape
    tm = _tile(S, 256)
    return pl.pallas_call(
        _norm_mod_kernel,
        grid=(S // tm,),
        in_specs=[pl.BlockSpec((tm, D), lambda i: (i, 0)),
                  pl.BlockSpec((1, D), lambda i: (0, 0)),
                  pl.BlockSpec((1, D), lambda i: (0, shift_idx)),
                  pl.BlockSpec((1, D), lambda i: (0, scale_idx))],
        out_specs=pl.BlockSpec((tm, D), lambda i: (i, 0)),
        out_shape=jax.ShapeDtypeStruct((S, D), BF16),
        compiler_params=_params(("arbitrary",), 32),
        name="norm_mod",
    )(x, g.reshape(1, D), mod, mod)


def _in_proj_kernel(h_ref, w_ref, c_ref, s1_ref, s2_ref, o_ref, *, per_sec):
    tn = o_ref.shape[1]
    acc = jnp.dot(h_ref[...], w_ref[...], preferred_element_type=F32)
    section = pl.program_id(1) // per_sec
    q_scale = jnp.where((section == 0) | (section == 2), QK_SCALE, 1.0).astype(F32)
    cs, s1, s2 = c_ref[...], s1_ref[...], s2_ref[...]
    for j in range(tn // HEAD_DIM):
        sl = slice(j * HEAD_DIM, (j + 1) * HEAD_DIM)
        t = acc[:, sl]
        r = (t * cs + pltpu.roll(t, HEAD_DIM - ROT_HALF, 1) * s1 + pltpu.roll(t, ROT_HALF, 1) * s2)
        o_ref[:, sl] = (r * q_scale).astype(o_ref.dtype)


def in_proj(h, w_in, tables):
    S, D = h.shape
    sec_width = w_in.shape[1] // 6
    tm = _tile(S, 1024)
    tn = _tile(sec_width, 512)
    per_sec = sec_width // tn
    src = lambda j: j + jnp.where(j >= 2 * per_sec, per_sec, 0)
    tspec = pl.BlockSpec((tm, LANES), lambda i, j: (i, 0))
    return pl.pallas_call(
        functools.partial(_in_proj_kernel, per_sec=per_sec),
        grid=(S // tm, 4 * per_sec),
        in_specs=[pl.BlockSpec((tm, D), lambda i, j: (i, 0)),
                  pl.BlockSpec((D, tn), lambda i, j: (0, src(j))),
                  tspec, tspec, tspec],
        out_specs=pl.BlockSpec((tm, tn), lambda i, j: (i, j)),
        out_shape=jax.ShapeDtypeStruct((S, 4 * sec_width), BF16),
        compiler_params=_params(("arbitrary", "arbitrary"), 48),
        name="in_proj",
    )(h, w_in, *tables)


def _v_proj_t_kernel(w_ref, h_ref, o_ref, wt_ref):
    @pl.when(pl.program_id(1) == 0)
    def _():
        wt_ref[...] = w_ref[...].T

    o_ref[...] = lax.dot_general(wt_ref[...], h_ref[...], NT_DIMS, preferred_element_type=F32).astype(o_ref.dtype)


def v_proj_t(h, w_in):
    S, D = h.shape
    sec_width = w_in.shape[1] // 6
    tn = _tile(sec_width, 512)
    ts = _tile(S, 1024)
    per_sec = sec_width // tn
    src = lambda j: j + jnp.where(j >= per_sec, 4 * per_sec, 2 * per_sec)
    return pl.pallas_call(
        _v_proj_t_kernel,
        grid=(2 * per_sec, S // ts),
        in_specs=[pl.BlockSpec((D, tn), lambda j, i: (0, src(j))), pl.BlockSpec((ts, D), lambda j, i: (i, 0))],
        out_specs=pl.BlockSpec((tn, ts), lambda j, i: (j, i)),
        out_shape=jax.ShapeDtypeStruct((2 * sec_width, S), BF16),
        scratch_shapes=[pltpu.VMEM((tn, D), BF16)],
        compiler_params=_params(("arbitrary", "arbitrary"), 48),
        name="v_proj_t",
    )(w_in, h)


def _diff_attn_kernel(lam_ref, q_ref, k_ref, vt_ref, g_ref, o_ref,
                      m1, l1, a1, m2, l2, a2, *, tq, lam_init):
    i = pl.program_id(1)
    stats = ((m1, l1, a1), (m2, l2, a2))
    for m, l, a in stats:
        m[...] = jnp.full(m.shape, -jnp.inf, F32)
        l[...] = jnp.zeros(l.shape, F32)
        a[...] = jnp.zeros(a.shape, F32)

    def chunk(kv, masked):
        k0 = pl.multiple_of(kv * tq, tq)
        vt = vt_ref[:, pl.ds(k0, tq)]

        def scores(c):
            sl = slice(c * HEAD_DIM, (c + 1) * HEAD_DIM)
            return lax.dot_general(k_ref[pl.ds(k0, tq), sl], q_ref[:, sl], NT_DIMS, preferred_element_type=F32)

        s_next = scores(0)
        for c, (m, l, a) in enumerate(stats):
            s = s_next
            if c == 0:
                s_next = scores(1)
            if masked:
                keep = (lax.broadcasted_iota(jnp.int32, (tq, tq), 0) <= lax.broadcasted_iota(jnp.int32, (tq, tq), 1))
                s = jnp.where(keep, s, NEG_INF)
            m_prev = m[...]
            m_new = jnp.maximum(m_prev, jnp.max(s, axis=0, keepdims=True))
            alpha = jnp.exp2(m_prev - m_new)
            p = jnp.exp2(s - m_new)
            l[...] = alpha * l[...] + jnp.sum(p, axis=0, keepdims=True)
            a[...] = alpha * a[...] + jnp.dot(vt, p.astype(vt.dtype), preferred_element_type=F32)
            m[...] = m_new

    def body(kv, carry):
        chunk(kv, False)
        return carry

    lax.fori_loop(0, i, body, 0)
    chunk(i, True)

    lp = lam_ref[...]
    lam = (jnp.exp(jnp.sum(lp[0:1] * lp[1:2], axis=-1, keepdims=True))
           - jnp.exp(jnp.sum(lp[2:3] * lp[3:4], axis=-1, keepdims=True)) + lam_init)
    o = (a1[...] / l1[...] - lam * (a2[...] / l2[...])).T
    y = o * lax.rsqrt(jnp.mean(o * o, axis=-1, keepdims=True) + NORM_EPS) * g_ref[...]
    o_ref[...] = (y * (1 - lam_init)).astype(o_ref.dtype)


def diff_attn(proj, v_t, lam_rows, subln_g, n_heads, lam_init):
    S = proj.shape[0]
    tq = _tile(S, 512)
    hw = 2 * HEAD_DIM
    return pl.pallas_call(
        functools.partial(_diff_attn_kernel, tq=tq, lam_init=lam_init),
        grid=(n_heads, S // tq),
        in_specs=[pl.BlockSpec((4, HEAD_DIM), lambda h, i: (0, 0)),
                  pl.BlockSpec((tq, hw), lambda h, i: (i, h)),
                  pl.BlockSpec((S, hw), lambda h, i: (0, n_heads + h)),
                  pl.BlockSpec((hw, S), lambda h, i: (h, 0)),
                  pl.BlockSpec((1, hw), lambda h, i: (0, 0))],
        out_specs=pl.BlockSpec((tq, hw), lambda h, i: (i, h)),
        out_shape=jax.ShapeDtypeStruct((S, n_heads * hw), BF16),
        scratch_shapes=[pltpu.VMEM((1, tq), F32), pltpu.VMEM((1, tq), F32), pltpu.VMEM((hw, tq), F32),
                        pltpu.VMEM((1, tq), F32), pltpu.VMEM((1, tq), F32), pltpu.VMEM((hw, tq), F32)],
        compiler_params=_params(("arbitrary", "arbitrary"), 48),
        name="diff_attn",
    )(lam_rows, proj, proj, v_t, subln_g.reshape(1, hw))


def _dil_bias(delta):
    count = jnp.zeros(delta.shape, F32)
    for window, dilation in DILATED_CONFIGS:
        ok = (delta >= 0) & (delta <= window) & ((delta & (dilation - 1)) == 0)
        count = count + jnp.where(ok, 1.0, 0.0)
    return jnp.where(count > 0, jnp.log2(jnp.maximum(count, 1.0)), NEG_INF)


DIL_HEADS_PER_STEP = 2


def _dil_attn_kernel(q_ref, k_ref, vt_ref, g_ref, o_ref, bias_ref, *, tq, win):
    i = pl.program_id(1)

    @pl.when((pl.program_id(0) == 0) & (i == 0))
    def _():
        kj = lax.broadcasted_iota(jnp.int32, bias_ref.shape, 0)
        qi = lax.broadcasted_iota(jnp.int32, bias_ref.shape, 1)
        bias_ref[...] = _dil_bias(qi + DIL_BACK - kj)

    start = pl.multiple_of(jnp.maximum(i * tq - DIL_BACK, 0), tq)
    row0 = pl.multiple_of(jnp.maximum(DIL_BACK - i * tq, 0), tq)
    bias = bias_ref[pl.ds(row0, win), :]
    heads = q_ref.shape[1] // HEAD_DIM

    def scores(hd):
        sl = slice(hd * HEAD_DIM, (hd + 1) * HEAD_DIM)
        return lax.dot_general(k_ref[pl.ds(start, win), sl], q_ref[:, sl], NT_DIMS,
                               preferred_element_type=F32) + bias

    s_next = scores(0)
    for hd in range(heads):
        sl = slice(hd * HEAD_DIM, (hd + 1) * HEAD_DIM)
        s = s_next
        if hd + 1 < heads:
            s_next = scores(hd + 1)
        p = jnp.exp2(s - jnp.max(s, axis=0, keepdims=True))
        vtw = vt_ref[sl, pl.ds(start, win)]
        o_t = jnp.dot(vtw, p.astype(vtw.dtype), preferred_element_type=F32) / jnp.sum(p, axis=0, keepdims=True)
        o = o_t.T
        y = o * lax.rsqrt(jnp.mean(o * o, axis=-1, keepdims=True) + NORM_EPS) * g_ref[:, sl]
        o_ref[:, sl] = y.astype(o_ref.dtype)


def dil_attn(proj, v_t, out_g, n_heads, col0, row0):
    S = proj.shape[0]
    for window, dilation in DILATED_CONFIGS:
        assert window % dilation == 0 and dilation & (dilation - 1) == 0
    tq = _tile(S, 256)
    win = DIL_BACK + tq
    hp = DIL_HEADS_PER_STEP
    hw = hp * HEAD_DIM
    assert S >= win and DIL_BACK % tq == 0 and win % LANES == 0
    assert n_heads % hp == 0 and col0 % hw == 0 and row0 % hw == 0
    c0 = col0 // hw
    r0 = row0 // hw
    ng = n_heads // hp
    return pl.pallas_call(
        functools.partial(_dil_attn_kernel, tq=tq, win=win),
        grid=(ng, S // tq),
        in_specs=[pl.BlockSpec((tq, hw), lambda h, i: (i, c0 + h)),
                  pl.BlockSpec((S, hw), lambda h, i: (0, c0 + ng + h)),
                  pl.BlockSpec((hw, S), lambda h, i: (r0 + h, 0)),
                  pl.BlockSpec((1, hw), lambda h, i: (0, h))],
        out_specs=pl.BlockSpec((tq, hw), lambda h, i: (i, h)),
        out_shape=jax.ShapeDtypeStruct((S, n_heads * HEAD_DIM), BF16),
        scratch_shapes=[pltpu.VMEM((win + DIL_BACK, tq), F32)],
        compiler_params=_params(("arbitrary", "arbitrary"), 48),
        name="dil_attn",
    )(proj, proj, v_t, out_g.reshape(1, n_heads * HEAD_DIM))


def _out_proj_kernel(ya_ref, yb_ref, wa_ref, wb_ref, x_ref, gate_ref, x1_ref):
    acc = (jnp.dot(ya_ref[...], wa_ref[...], preferred_element_type=F32)
           + jnp.dot(yb_ref[...], wb_ref[...], preferred_element_type=F32))
    x1_ref[...] = x_ref[...] + gate_ref[...] * acc


def out_proj(ya, yb, w_out, x, mod, gate_idx):
    S, D = x.shape
    kh = ya.shape[1]
    assert yb.shape[1] == kh and w_out.shape[0] == 2 * kh
    tm = _tile(S, 1024)
    tn = _tile(D, 512)
    nj = D // tn
    return pl.pallas_call(
        _out_proj_kernel,
        grid=(S // tm, nj),
        in_specs=[pl.BlockSpec((tm, kh), lambda i, j: (i, 0)),
                  pl.BlockSpec((tm, kh), lambda i, j: (i, 0)),
                  pl.BlockSpec((kh, tn), lambda i, j: (0, j)),
                  pl.BlockSpec((kh, tn), lambda i, j: (1, j)),
                  pl.BlockSpec((tm, tn), lambda i, j: (i, j)),
                  pl.BlockSpec((1, tn), lambda i, j: (0, gate_idx * nj + j))],
        out_specs=pl.BlockSpec((tm, tn), lambda i, j: (i, j)),
        out_shape=jax.ShapeDtypeStruct((S, D), F32),
        compiler_params=_params(("arbitrary", "arbitrary"), 48),
        name="out_proj",
    )(ya, yb, w_out, w_out, x, mod)


def _top16(x):
    rows = lax.broadcasted_iota(jnp.int32, x.shape, 0)
    rank = jnp.full(x.shape, PEER_TOPK, jnp.int32)
    vals = []
    for r in range(PEER_TOPK):
        m = jnp.max(x, axis=0, keepdims=True)
        idx = jnp.min(jnp.where(x == m, rows, x.shape[0]), axis=0, keepdims=True)
        hit = rows == idx
        rank = jnp.where(hit, r, rank)
        x = jnp.where(hit, -jnp.inf, x)
        vals.append(m)
    return jnp.concatenate(vals, axis=0), rank


def _peer_router_kernel(h_ref, wq_ref, sk_ref, rank1_ref, cnt0_ref, e0_ref, e1_ref):
    q = jnp.dot(h_ref[...], wq_ref[...], preferred_element_type=F32)
    half = q.shape[1] // 2
    s = [lax.dot_general(sk_ref[0, p], q[:, p * half:(p + 1) * half], NT_DIMS,
                         precision=lax.Precision.HIGHEST, preferred_element_type=F32)
         for p in range(2)]
    a, rank0 = _top16(s[0])
    b, rank1 = _top16(s[1])
    n_q = [PEER_TOPK // (p + 1) for p in range(PEER_TOPK)]
    pad = -sum(n_q) % 8
    cand = jnp.concatenate([a[p:p + 1] + b[:n_q[p]] for p in range(PEER_TOPK)]
                           + [jnp.full((pad, a.shape[1]), -jnp.inf, F32)], axis=0)
    fin, sel_rank = _top16(cand)
    sel = jnp.where(sel_rank < PEER_TOPK, 1.0, 0.0)
    starts = [sum(n_q[:p]) for p in range(PEER_TOPK)]
    cnt = jnp.concatenate([jnp.sum(sel[starts[p]:starts[p] + n_q[p]], axis=0, keepdims=True)
                           for p in range(PEER_TOPK)], axis=0)
    cnt0 = jnp.zeros(s[0].shape, F32)
    for p in range(PEER_TOPK):
        cnt0 = jnp.where(rank0 == p, cnt[p:p + 1], cnt0)
    z = jnp.sum(jnp.exp(fin - fin[0:1]), axis=0, keepdims=True)
    rank1_ref[0] = rank1.astype(F32).astype(BF16)
    cnt0_ref[0] = cnt0.astype(BF16)
    e0_ref[0] = jnp.exp(s[0] - a[0:1]).astype(BF16)
    e1_ref[0] = (jnp.exp(s[1] - b[0:1]) / z).astype(BF16)


def peer_router(h, w_q, subkeys):
    S, D = h.shape
    n_heads = subkeys.shape[0]
    qd = w_q.shape[1] // n_heads
    T = _tile(S, 512)
    ospec = pl.BlockSpec((1, N_KEYS, T), lambda t, hd: (hd, 0, t))
    return pl.pallas_call(
        _peer_router_kernel,
        grid=(S // T, n_heads),
        in_specs=[pl.BlockSpec((T, D), lambda t, hd: (t, 0)),
                  pl.BlockSpec((D, qd), lambda t, hd: (0, hd)),
                  pl.BlockSpec((1, 2, N_KEYS, qd // 2), lambda t, hd: (hd, 0, 0, 0))],
        out_specs=[ospec] * 4,
        out_shape=[jax.ShapeDtypeStruct((n_heads, N_KEYS, S), BF16)] * 4,
        compiler_params=_params(("arbitrary", "arbitrary"), 48),
        name="peer_router",
    )(h, w_q, subkeys)


def _gelu(x):
    return 0.5 * x * (1.0 + lax.erf(x * (0.5 ** 0.5)))


PEER_GI = 8
PEER_GJ = 64


def _peer_expert_kernel(h_ref, u_ref, v_ref, rank1_ref, cnt0_ref, e0_ref, e1_ref, o_ref, w_ref):
    ib, jb = pl.program_id(1), pl.program_id(2)
    gi, gj, D = u_ref.shape
    n_heads, _, T = rank1_ref.shape
    i0 = pl.multiple_of(ib * gi, gi)
    j0 = pl.multiple_of(jb * gj, gj)

    @pl.when((ib == 0) & (jb == 0))
    def _():
        o_ref[...] = jnp.zeros(o_ref.shape, F32)

    u = u_ref[...].reshape(gi * gj, D)
    act = _gelu(lax.dot_general(u, h_ref[...], NT_DIMS, preferred_element_type=F32))
    for lc in range(T // LANES):
        lsl = slice(lc * LANES, (lc + 1) * LANES)
        c0 = [cnt0_ref[hd, pl.ds(i0, gi), lsl] for hd in range(n_heads)]
        e0 = [e0_ref[hd, pl.ds(i0, gi), lsl] for hd in range(n_heads)]
        for ii in range(gi):
            rsl = slice(ii * gj, (ii + 1) * gj)
            g = jnp.zeros((gj, LANES), BF16)
            for hd in range(n_heads):
                r1 = rank1_ref[hd, pl.ds(j0, gj), lsl]
                e1 = e1_ref[hd, pl.ds(j0, gj), lsl]
                g = g + jnp.where(r1 < c0[hd][ii:ii + 1], e1 * e0[hd][ii:ii + 1], jnp.zeros((), BF16))
            w_ref[rsl, lsl] = (g.astype(F32) * act[rsl, lsl]).astype(w_ref.dtype)
    v = v_ref[...].reshape(gi * gj, D)
    o_ref[...] += lax.dot_general(w_ref[...], v, TN_DIMS, preferred_element_type=F32)


def peer_experts(h, u, v, tables):
    S, D = h.shape
    n_heads = tables[0].shape[0]
    T = _tile(S, 512)
    u3 = u.reshape(N_KEYS, N_KEYS, D)
    v3 = v.reshape(N_KEYS, N_KEYS, D)
    wspec = pl.BlockSpec((PEER_GI, PEER_GJ, D), lambda t, i, j: (i, j, 0))
    once = pl.Buffered(1)
    tspec = pl.BlockSpec((n_heads, N_KEYS, T), lambda t, i, j: (0, 0, t), pipeline_mode=once)
    return pl.pallas_call(
        _peer_expert_kernel,
        grid=(S // T, N_KEYS // PEER_GI, N_KEYS // PEER_GJ),
        in_specs=[pl.BlockSpec((T, D), lambda t, i, j: (t, 0), pipeline_mode=once),
                  wspec, wspec, tspec, tspec, tspec, tspec],
        out_specs=pl.BlockSpec((T, D), lambda t, i, j: (t, 0)),
        out_shape=jax.ShapeDtypeStruct((S, D), F32),
        scratch_shapes=[pltpu.VMEM((PEER_GI * PEER_GJ, T), BF16)],
        compiler_params=_params(("arbitrary", "arbitrary", "arbitrary"), 56),
        name="peer_experts",
    )(h, u3, v3, *tables)


def _final_kernel(x_ref, y_ref, gate_ref, g_ref, o_ref, *, normalize):
    x = x_ref[...] + gate_ref[...] * y_ref[...]
    if normalize:
        x = x * lax.rsqrt(jnp.mean(x * x, axis=-1, keepdims=True) + NORM_EPS) * g_ref[...]
    o_ref[...] = x


def final_norm(x, y, mod, gate_idx, g, normalize):
    S, D = x.shape
    tm = _tile(S, 256)
    blk = pl.BlockSpec((tm, D), lambda i: (i, 0))
    return pl.pallas_call(
        functools.partial(_final_kernel, normalize=normalize),
        grid=(S // tm,),
        in_specs=[blk, blk, pl.BlockSpec((1, D), lambda i: (0, gate_idx)), pl.BlockSpec((1, D), lambda i: (0, 0))],
        out_specs=blk,
        out_shape=jax.ShapeDtypeStruct((S, D), F32),
        compiler_params=_params(("arbitrary",), 48),
        name="final_norm",
    )(x, y, mod, g.reshape(1, D))


def kernel(x, c, positions, norm1_g, norm2_g, w_ada, b_ada, w_in, lam_q1, lam_k1, lam_q2, lam_k2,
           diff_subln_g, dil_out_g, w_out, peer_wq, peer_subkeys, peer_u, peer_v, final_g):
    B, S, D = x.shape
    depth = w_ada.shape[0]
    assert B == 1, "one sequence per call"
    diff_width = D // 2
    diff_heads = diff_width // (2 * HEAD_DIM)
    dil_heads = (D - diff_width) // HEAD_DIM
    xs = x.reshape(S, D)
    tables = rope_tables(positions.reshape(S))
    for l in range(depth):
        lam_init = 0.8 - 0.6 * math.exp(-0.3 * l)
        mod = ada_mod(c, w_ada[l], b_ada[l])
        h = norm_mod(xs, norm1_g[l], mod, 0, 1)
        w_in_l = w_in[l].astype(BF16)
        proj = in_proj(h, w_in_l, tables)
        v_t = v_proj_t(h, w_in_l)
        lam_rows = jnp.stack([lam_q1[l], lam_k1[l], lam_q2[l], lam_k2[l]])
        y_diff = diff_attn(proj, v_t, lam_rows, diff_subln_g[l], diff_heads, lam_init)
        y_dil = dil_attn(proj, v_t, dil_out_g[l], dil_heads, 2 * diff_width, diff_width)
        xs = out_proj(y_diff, y_dil, w_out[l].astype(BF16), xs, mod, 2)
        h = norm_mod(xs, norm2_g[l], mod, 3, 4)
        gates = peer_router(h, peer_wq[l].astype(BF16), peer_subkeys[l])
        y = peer_experts(h, peer_u[l].astype(BF16), peer_v[l].astype(BF16), gates)
        xs = final_norm(xs, y, mod, 5, final_g, normalize=(l == depth - 1))
    return xs.reshape(B, S, D)
```

```python
import functools
import math

import jax
import jax.numpy as jnp
from jax import lax
from jax.experimental import pallas as pl
from jax.experimental.pallas import tpu as pltpu

F32 = jnp.float32
BF16 = jnp.bfloat16
F8 = jnp.float8_e4m3fn
FP8_MAX = float(jnp.finfo(F8).max)
FP8_TINY = 1e-30

HEAD_DIM = 128
ROT_DIM = HEAD_DIM // 4
ROT_HALF = ROT_DIM // 2
ROPE_THETA = 500000.0
DILATED_CONFIGS = ((128, 1), (512, 4), (2048, 16))
DIL_BACK = max(w for w, _ in DILATED_CONFIGS)
PEER_HEADS = 8
N_KEYS = 128
PEER_TOPK = 16
NORM_EPS = 1e-6
NEG_INF = -1e30
QK_SCALE = HEAD_DIM ** -0.5 * math.log2(math.e)
DIFF_ROW_SPLIT = 2
LANES = 128
MIB = 1024 * 1024

NT_DIMS = (((1,), (1,)), ((), ()))
TN_DIMS = (((0,), (0,)), ((), ()))


def _params(semantics, vmem_mib):
    return pltpu.CompilerParams(dimension_semantics=semantics, vmem_limit_bytes=vmem_mib * MIB)


def _tile(n, pref):
    t = min(n, pref)
    assert n % t == 0, (n, pref)
    return t


def _ada_kernel(c_ref, w_ref, b_ref, o_ref):
    c = c_ref[...]
    ca = c * jax.nn.sigmoid(c)
    for j in range(o_ref.shape[1] // LANES):
        sl = slice(j * LANES, (j + 1) * LANES)
        o_ref[:, sl] = jnp.sum(w_ref[:, sl] * ca, axis=0, keepdims=True) + b_ref[:, sl]


def ada_mod(c, w_ada, b_ada):
    D, N = w_ada.shape
    tn = _tile(N, 1024)
    cb = jnp.broadcast_to(c.reshape(D, 1), (D, LANES))
    return pl.pallas_call(
        _ada_kernel,
        grid=(N // tn,),
        in_specs=[pl.BlockSpec((D, LANES), lambda j: (0, 0)),
                  pl.BlockSpec((D, tn), lambda j: (0, j)),
                  pl.BlockSpec((1, tn), lambda j: (0, j))],
        out_specs=pl.BlockSpec((1, tn), lambda j: (0, j)),
        out_shape=jax.ShapeDtypeStruct((1, N), F32),
        compiler_params=_params(("arbitrary",), 48),
        name="ada_mod",
    )(cb, w_ada, b_ada.reshape(1, N))


def _rope_kernel(pos_ref, freq_ref, c_ref, s1_ref, s2_ref):
    ang = pos_ref[...].astype(F32) * freq_ref[...]
    lane = lax.broadcasted_iota(jnp.int32, ang.shape, 1)
    cs = jnp.cos(ang)
    sn = jnp.sin(ang)
    c_ref[...] = jnp.where(lane < ROT_DIM, cs, 1.0)
    s1_ref[...] = jnp.where(lane < ROT_HALF, -sn, 0.0)
    s2_ref[...] = jnp.where((lane >= ROT_HALF) & (lane < ROT_DIM), sn, 0.0)


def rope_tables(positions):
    S = positions.shape[0]
    ts = _tile(S, 1024)
    inv_freq = jnp.power(ROPE_THETA, -jnp.arange(0, ROT_DIM, 2, dtype=F32) / ROT_DIM)
    freq = jnp.tile(inv_freq, LANES // ROT_HALF).reshape(1, LANES)
    spec = pl.BlockSpec((ts, LANES), lambda i: (i, 0))
    return pl.pallas_call(
        _rope_kernel,
        grid=(S // ts,),
        in_specs=[pl.BlockSpec((ts, 1), lambda i: (i, 0)), pl.BlockSpec((1, LANES), lambda i: (0, 0))],
        out_specs=[spec, spec, spec],
        out_shape=[jax.ShapeDtypeStruct((S, LANES), F32)] * 3,
        compiler_params=_params(("arbitrary",), 32),
        name="rope_tables",
    )(positions.reshape(S, 1), freq)


def _rms_mod(x, g, scale, shift):
    y = x * lax.rsqrt(jnp.mean(x * x, axis=-1, keepdims=True) + NORM_EPS) * g
    return y * (1 + scale) + shift


def _norm_mod_kernel(x_ref, g_ref, sh_ref, sc_ref, o_ref):
    o_ref[...] = _rms_mod(x_ref[...], g_ref[...], sc_ref[...], sh_ref[...]).astype(o_ref.dtype)


def norm_mod(x, g, mod, shift_idx, scale_idx):
    S, D = x.shape
    tm = _tile(S, 256)
    return pl.pallas_call(
        _norm_mod_kernel,
        grid=(S // tm,),
        in_specs=[pl.BlockSpec((tm, D), lambda i: (i, 0)),
                  pl.BlockSpec((1, D), lambda i: (0, 0)),
                  pl.BlockSpec((1, D), lambda i: (0, shift_idx)),
                  pl.BlockSpec((1, D), lambda i: (0, scale_idx))],
        out_specs=pl.BlockSpec((tm, D), lambda i: (i, 0)),
        out_shape=jax.ShapeDtypeStruct((S, D), BF16),
        compiler_params=_params(("arbitrary",), 32),
        name="norm_mod",
    )(x, g.reshape(1, D), mod, mod)


def _norm_mod_q_kernel(x_ref, g_ref, sh_ref, sc_ref, o_ref, q_ref, s_ref):
    y = _rms_mod(x_ref[...], g_ref[...], sc_ref[...], sh_ref[...])
    o_ref[...] = y.astype(o_ref.dtype)
    scale = jnp.maximum(jnp.max(jnp.abs(y), axis=-1, keepdims=True), FP8_TINY) / FP8_MAX
    q_ref[...] = (y / scale).astype(q_ref.dtype)
    s_ref[...] = scale


def norm_mod_q(x, g, mod, shift_idx, scale_idx):
    S, D = x.shape
    tm = _tile(S, 256)
    blk = pl.BlockSpec((tm, D), lambda i: (i, 0))
    return pl.pallas_call(
        _norm_mod_q_kernel,
        grid=(S // tm,),
        in_specs=[blk,
                  pl.BlockSpec((1, D), lambda i: (0, 0)),
                  pl.BlockSpec((1, D), lambda i: (0, shift_idx)),
                  pl.BlockSpec((1, D), lambda i: (0, scale_idx))],
        out_specs=[blk, blk, pl.BlockSpec((tm, 1), lambda i: (i, 0))],
        out_shape=[jax.ShapeDtypeStruct((S, D), BF16), jax.ShapeDtypeStruct((S, D), F8),
                   jax.ShapeDtypeStruct((S, 1), F32)],
        compiler_params=_params(("arbitrary",), 32),
        name="norm_mod_q",
    )(x, g.reshape(1, D), mod, mod)


def _in_proj_kernel(h_ref, w_ref, c_ref, s1_ref, s2_ref, o_ref, *, per_sec):
    tn = o_ref.shape[1]
    acc = jnp.dot(h_ref[...], w_ref[...], preferred_element_type=F32)
    section = pl.program_id(1) // per_sec
    q_scale = jnp.where((section == 0) | (section == 2), QK_SCALE, 1.0).astype(F32)
    cs, s1, s2 = c_ref[...], s1_ref[...], s2_ref[...]
    for j in range(tn // HEAD_DIM):
        sl = slice(j * HEAD_DIM, (j + 1) * HEAD_DIM)
        t = acc[:, sl]
        r = (t * cs + pltpu.roll(t, HEAD_DIM - ROT_HALF, 1) * s1 + pltpu.roll(t, ROT_HALF, 1) * s2)
        o_ref[:, sl] = (r * q_scale).astype(o_ref.dtype)


def in_proj(h, w_in, tables):
    S, D = h.shape
    sec_width = w_in.shape[1] // 6
    tm = _tile(S, 1024)
    tn = _tile(sec_width, 512)
    per_sec = sec_width // tn
    src = lambda j: j + jnp.where(j >= 2 * per_sec, per_sec, 0)
    tspec = pl.BlockSpec((tm, LANES), lambda i, j: (i, 0))
    return pl.pallas_call(
        functools.partial(_in_proj_kernel, per_sec=per_sec),
        grid=(S // tm, 4 * per_sec),
        in_specs=[pl.BlockSpec((tm, D), lambda i, j: (i, 0)),
                  pl.BlockSpec((D, tn), lambda i, j: (0, src(j))),
                  tspec, tspec, tspec],
        out_specs=pl.BlockSpec((tm, tn), lambda i, j: (i, j)),
        out_shape=jax.ShapeDtypeStruct((S, 4 * sec_width), BF16),
        compiler_params=_params(("arbitrary", "arbitrary"), 48),
        name="in_proj",
    )(h, w_in, *tables)


def _v_proj_t_kernel(w_ref, h_ref, o_ref, wt_ref):
    @pl.when(pl.program_id(1) == 0)
    def _():
        wt_ref[...] = w_ref[...].T

    o_ref[...] = lax.dot_general(wt_ref[...], h_ref[...], NT_DIMS, preferred_element_type=F32).astype(o_ref.dtype)


def v_proj_t(h, w_in):
    S, D = h.shape
    sec_width = w_in.shape[1] // 6
    tn = _tile(sec_width, 512)
    ts = _tile(S, 1024)
    per_sec = sec_width // tn
    src = lambda j: j + jnp.where(j >= per_sec, 4 * per_sec, 2 * per_sec)
    return pl.pallas_call(
        _v_proj_t_kernel,
        grid=(2 * per_sec, S // ts),
        in_specs=[pl.BlockSpec((D, tn), lambda j, i: (0, src(j))), pl.BlockSpec((ts, D), lambda j, i: (i, 0))],
        out_specs=pl.BlockSpec((tn, ts), lambda j, i: (j, i)),
        out_shape=jax.ShapeDtypeStruct((2 * sec_width, S), BF16),
        scratch_shapes=[pltpu.VMEM((tn, D), BF16)],
        compiler_params=_params(("arbitrary", "arbitrary"), 48),
        name="v_proj_t",
    )(w_in, h)


def _diff_attn_kernel(lam_ref, q_ref, k_ref, vt_ref, g_ref, o_ref,
                      m1, l1, a1, m2, l2, a2, *, tq, lam_init):
    i = pl.program_id(1)
    stats = ((m1, l1, a1), (m2, l2, a2))
    for m, l, a in stats:
        m[...] = jnp.full(m.shape, -jnp.inf, F32)
        l[...] = jnp.zeros(l.shape, F32)
        a[...] = jnp.zeros(a.shape, F32)

    def chunk(kv, masked):
        k0 = pl.multiple_of(kv * tq, tq)
        vt = vt_ref[:, pl.ds(k0, tq)]

        def scores(c):
            sl = slice(c * HEAD_DIM, (c + 1) * HEAD_DIM)
            return lax.dot_general(k_ref[pl.ds(k0, tq), sl], q_ref[:, sl], NT_DIMS, preferred_element_type=F32)

        s_next = scores(0)
        for c, (m, l, a) in enumerate(stats):
            s = s_next
            if c == 0:
                s_next = scores(1)
            if masked:
                keep = (lax.broadcasted_iota(jnp.int32, (tq, tq), 0) <= lax.broadcasted_iota(jnp.int32, (tq, tq), 1))
                s = jnp.where(keep, s, NEG_INF)
            m_prev = m[...]
            m_new = jnp.maximum(m_prev, jnp.max(s, axis=0, keepdims=True))
            alpha = jnp.exp2(m_prev - m_new)
            p = jnp.exp2(s - m_new)
            l[...] = alpha * l[...] + jnp.sum(p, axis=0, keepdims=True)
            a[...] = alpha * a[...] + jnp.dot(vt, p.astype(vt.dtype), preferred_element_type=F32)
            m[...] = m_new

    def body(kv, carry):
        chunk(kv, False)
        return carry

    lax.fori_loop(0, i, body, 0)
    chunk(i, True)

    lp = lam_ref[...]
    lam = (jnp.exp(jnp.sum(lp[0:1] * lp[1:2], axis=-1, keepdims=True))
           - jnp.exp(jnp.sum(lp[2:3] * lp[3:4], axis=-1, keepdims=True)) + lam_init)
    o = (a1[...] / l1[...] - lam * (a2[...] / l2[...])).T
    y = o * lax.rsqrt(jnp.mean(o * o, axis=-1, keepdims=True) + NORM_EPS) * g_ref[...]
    o_ref[...] = (y * (1 - lam_init)).astype(o_ref.dtype)


def diff_attn(proj, v_t, lam_rows, subln_g, n_heads, lam_init):
    S = proj.shape[0]
    tq = _tile(S, 512)
    hw = 2 * HEAD_DIM
    return pl.pallas_call(
        functools.partial(_diff_attn_kernel, tq=tq, lam_init=lam_init),
        grid=(n_heads, S // tq),
        in_specs=[pl.BlockSpec((4, HEAD_DIM), lambda h, i: (0, 0)),
                  pl.BlockSpec((tq, hw), lambda h, i: (i, h)),
                  pl.BlockSpec((S, hw), lambda h, i: (0, n_heads + h)),
                  pl.BlockSpec((hw, S), lambda h, i: (h, 0)),
                  pl.BlockSpec((1, hw), lambda h, i: (0, 0))],
        out_specs=pl.BlockSpec((tq, hw), lambda h, i: (i, h)),
        out_shape=jax.ShapeDtypeStruct((S, n_heads * hw), BF16),
        scratch_shapes=[pltpu.VMEM((1, tq), F32), pltpu.VMEM((1, tq), F32), pltpu.VMEM((hw, tq), F32),
                        pltpu.VMEM((1, tq), F32), pltpu.VMEM((1, tq), F32), pltpu.VMEM((hw, tq), F32)],
        compiler_params=_params(("arbitrary", "arbitrary"), 48),
        name="diff_attn",
    )(lam_rows, proj, proj, v_t, subln_g.reshape(1, hw))


def _dil_bias(delta):
    count = jnp.zeros(delta.shape, F32)
    for window, dilation in DILATED_CONFIGS:
        ok = (delta >= 0) & (delta <= window) & ((delta & (dilation - 1)) == 0)
        count = count + jnp.where(ok, 1.0, 0.0)
    return jnp.where(count > 0, jnp.log2(jnp.maximum(count, 1.0)), NEG_INF)


DIL_HEADS_PER_STEP = 2


def _dil_attn_kernel(q_ref, k_ref, vt_ref, g_ref, o_ref, bias_ref, *, tq, win):
    i = pl.program_id(1)

    @pl.when((pl.program_id(0) == 0) & (i == 0))
    def _():
        kj = lax.broadcasted_iota(jnp.int32, bias_ref.shape, 0)
        qi = lax.broadcasted_iota(jnp.int32, bias_ref.shape, 1)
        bias_ref[...] = _dil_bias(qi + DIL_BACK - kj)

    start = pl.multiple_of(jnp.maximum(i * tq - DIL_BACK, 0), tq)
    row0 = pl.multiple_of(jnp.maximum(DIL_BACK - i * tq, 0), tq)
    bias = bias_ref[pl.ds(row0, win), :]
    heads = q_ref.shape[1] // HEAD_DIM

    def scores(hd):
        sl = slice(hd * HEAD_DIM, (hd + 1) * HEAD_DIM)
        return lax.dot_general(k_ref[pl.ds(start, win), sl], q_ref[:, sl], NT_DIMS,
                               preferred_element_type=F32) + bias

    s_next = scores(0)
    for hd in range(heads):
        sl = slice(hd * HEAD_DIM, (hd + 1) * HEAD_DIM)
        s = s_next
        if hd + 1 < heads:
            s_next = scores(hd + 1)
        p = jnp.exp2(s - jnp.max(s, axis=0, keepdims=True))
        vtw = vt_ref[sl, pl.ds(start, win)]
        o_t = jnp.dot(vtw, p.astype(vtw.dtype), preferred_element_type=F32) / jnp.sum(p, axis=0, keepdims=True)
        o = o_t.T
        y = o * lax.rsqrt(jnp.mean(o * o, axis=-1, keepdims=True) + NORM_EPS) * g_ref[:, sl]
        o_ref[:, sl] = y.astype(o_ref.dtype)


def dil_attn(proj, v_t, out_g, n_heads, col0, row0):
    S = proj.shape[0]
    for window, dilation in DILATED_CONFIGS:
        assert window % dilation == 0 and dilation & (dilation - 1) == 0
    tq = _tile(S, 256)
    win = DIL_BACK + tq
    hp = DIL_HEADS_PER_STEP
    hw = hp * HEAD_DIM
    assert S >= win and DIL_BACK % tq == 0 and win % LANES == 0
    assert n_heads % hp == 0 and col0 % hw == 0 and row0 % hw == 0
    c0 = col0 // hw
    r0 = row0 // hw
    ng = n_heads // hp
    return pl.pallas_call(
        functools.partial(_dil_attn_kernel, tq=tq, win=win),
        grid=(ng, S // tq),
        in_specs=[pl.BlockSpec((tq, hw), lambda h, i: (i, c0 + h)),
                  pl.BlockSpec((S, hw), lambda h, i: (0, c0 + ng + h)),
                  pl.BlockSpec((hw, S), lambda h, i: (r0 + h, 0)),
                  pl.BlockSpec((1, hw), lambda h, i: (0, h))],
        out_specs=pl.BlockSpec((tq, hw), lambda h, i: (i, h)),
        out_shape=jax.ShapeDtypeStruct((S, n_heads * HEAD_DIM), BF16),
        scratch_shapes=[pltpu.VMEM((win + DIL_BACK, tq), F32)],
        compiler_params=_params(("arbitrary", "arbitrary"), 48),
        name="dil_attn",
    )(proj, proj, v_t, out_g.reshape(1, n_heads * HEAD_DIM))


def _out_proj_kernel(ya_ref, yb_ref, wa_ref, wb_ref, x_ref, gate_ref, x1_ref):
    acc = (jnp.dot(ya_ref[...], wa_ref[...], preferred_element_type=F32)
           + jnp.dot(yb_ref[...], wb_ref[...], preferred_element_type=F32))
    x1_ref[...] = x_ref[...] + gate_ref[...] * acc


def out_proj(ya, yb, w_out, x, mod, gate_idx):
    S, D = x.shape
    kh = ya.shape[1]
    assert yb.shape[1] == kh and w_out.shape[0] == 2 * kh
    tm = _tile(S, 1024)
    tn = _tile(D, 512)
    nj = D // tn
    return pl.pallas_call(
        _out_proj_kernel,
        grid=(S // tm, nj),
        in_specs=[pl.BlockSpec((tm, kh), lambda i, j: (i, 0)),
                  pl.BlockSpec((tm, kh), lambda i, j: (i, 0)),
                  pl.BlockSpec((kh, tn), lambda i, j: (0, j)),
                  pl.BlockSpec((kh, tn), lambda i, j: (1, j)),
                  pl.BlockSpec((tm, tn), lambda i, j: (i, j)),
                  pl.BlockSpec((1, tn), lambda i, j: (0, gate_idx * nj + j))],
        out_specs=pl.BlockSpec((tm, tn), lambda i, j: (i, j)),
        out_shape=jax.ShapeDtypeStruct((S, D), F32),
        compiler_params=_params(("arbitrary", "arbitrary"), 48),
        name="out_proj",
    )(ya, yb, w_out, w_out, x, mod)


def _top16(x):
    rows = lax.broadcasted_iota(jnp.int32, x.shape, 0)
    rank = jnp.full(x.shape, PEER_TOPK, jnp.int32)
    vals = []
    for r in range(PEER_TOPK):
        m = jnp.max(x, axis=0, keepdims=True)
        idx = jnp.min(jnp.where(x == m, rows, x.shape[0]), axis=0, keepdims=True)
        hit = rows == idx
        rank = jnp.where(hit, r, rank)
        x = jnp.where(hit, -jnp.inf, x)
        vals.append(m)
    return jnp.concatenate(vals, axis=0), rank


def _peer_router_kernel(h_ref, wq_ref, sk_ref, rank1_ref, cnt0_ref, e0_ref, e1_ref):
    q = jnp.dot(h_ref[...], wq_ref[...], preferred_element_type=F32)
    half = q.shape[1] // 2
    s = [lax.dot_general(sk_ref[0, p], q[:, p * half:(p + 1) * half], NT_DIMS,
                         precision=lax.Precision.HIGHEST, preferred_element_type=F32)
         for p in range(2)]
    a, rank0 = _top16(s[0])
    b, rank1 = _top16(s[1])
    n_q = [PEER_TOPK // (p + 1) for p in range(PEER_TOPK)]
    pad = -sum(n_q) % 8
    cand = jnp.concatenate([a[p:p + 1] + b[:n_q[p]] for p in range(PEER_TOPK)]
                           + [jnp.full((pad, a.shape[1]), -jnp.inf, F32)], axis=0)
    fin, sel_rank = _top16(cand)
    sel = jnp.where(sel_rank < PEER_TOPK, 1.0, 0.0)
    starts = [sum(n_q[:p]) for p in range(PEER_TOPK)]
    cnt = jnp.concatenate([jnp.sum(sel[starts[p]:starts[p] + n_q[p]], axis=0, keepdims=True)
                           for p in range(PEER_TOPK)], axis=0)
    cnt0 = jnp.zeros(s[0].shape, F32)
    for p in range(PEER_TOPK):
        cnt0 = jnp.where(rank0 == p, cnt[p:p + 1], cnt0)
    z = jnp.sum(jnp.exp(fin - fin[0:1]), axis=0, keepdims=True)
    rank1_ref[0] = rank1.astype(F32).astype(BF16)
    cnt0_ref[0] = cnt0.astype(BF16)
    e0_ref[0] = jnp.exp(s[0] - a[0:1]).astype(BF16)
    e1_ref[0] = (jnp.exp(s[1] - b[0:1]) / z).astype(BF16)


def peer_router(h, w_q, subkeys):
    S, D = h.shape
    n_heads = subkeys.shape[0]
    qd = w_q.shape[1] // n_heads
    T = _tile(S, 512)
    ospec = pl.BlockSpec((1, N_KEYS, T), lambda t, hd: (hd, 0, t))
    return pl.pallas_call(
        _peer_router_kernel,
        grid=(S // T, n_heads),
        in_specs=[pl.BlockSpec((T, D), lambda t, hd: (t, 0)),
                  pl.BlockSpec((D, qd), lambda t, hd: (0, hd)),
                  pl.BlockSpec((1, 2, N_KEYS, qd // 2), lambda t, hd: (hd, 0, 0, 0))],
        out_specs=[ospec] * 4,
        out_shape=[jax.ShapeDtypeStruct((n_heads, N_KEYS, S), BF16)] * 4,
        compiler_params=_params(("arbitrary", "arbitrary"), 48),
        name="peer_router",
    )(h, w_q, subkeys)


def _gelu(x):
    return 0.5 * x * (1.0 + lax.erf(x * (0.5 ** 0.5)))


PEER_GI = 8
PEER_GJ = 64


def _peer_expert_kernel(qs_ref, h_ref, hs_ref, u_ref, v_ref, rank1_ref, cnt0_ref, e0_ref, e1_ref, o_ref,
                        wf_ref, w8_ref):
    ib, jb = pl.program_id(1), pl.program_id(2)
    gi, gj, D = u_ref.shape
    n_heads, _, T = rank1_ref.shape
    i0 = pl.multiple_of(ib * gi, gi)
    j0 = pl.multiple_of(jb * gj, gj)

    @pl.when((ib == 0) & (jb == 0))
    def _():
        o_ref[...] = jnp.zeros(o_ref.shape, F32)

    u = u_ref[...].reshape(gi * gj, D)
    a = lax.dot_general(u, h_ref[...], NT_DIMS, preferred_element_type=F32)
    act = _gelu(a * (hs_ref[...] * qs_ref[0]))
    w_max = jnp.zeros((gj, LANES), F32)
    for lc in range(T // LANES):
        lsl = slice(lc * LANES, (lc + 1) * LANES)
        c0 = [cnt0_ref[hd, pl.ds(i0, gi), lsl] for hd in range(n_heads)]
        e0 = [e0_ref[hd, pl.ds(i0, gi), lsl] for hd in range(n_heads)]
        for ii in range(gi):
            rsl = slice(ii * gj, (ii + 1) * gj)
            g = jnp.zeros((gj, LANES), BF16)
            for hd in range(n_heads):
                r1 = rank1_ref[hd, pl.ds(j0, gj), lsl]
                e1 = e1_ref[hd, pl.ds(j0, gj), lsl]
                g = g + jnp.where(r1 < c0[hd][ii:ii + 1], e1 * e0[hd][ii:ii + 1], jnp.zeros((), BF16))
            w = g.astype(F32) * act[rsl, lsl]
            wf_ref[rsl, lsl] = w
            w_max = jnp.maximum(w_max, jnp.abs(w))
    w_amax = jnp.maximum(jnp.max(w_max, axis=(0, 1), keepdims=True), FP8_TINY)
    w8_ref[...] = (wf_ref[...] * (FP8_MAX / w_amax)).astype(w8_ref.dtype)
    v = v_ref[...].reshape(gi * gj, D)
    y = lax.dot_general(w8_ref[...], v, TN_DIMS, preferred_element_type=F32)
    o_ref[...] += y * (w_amax * (qs_ref[1] / FP8_MAX))


def _fp8_tensor(w):
    scale = jnp.maximum(jnp.max(jnp.abs(w)), FP8_TINY) / FP8_MAX
    return (w / scale).astype(F8), scale


def peer_experts(h8, h_scale, u, v, tables):
    S, D = h8.shape
    n_heads = tables[0].shape[0]
    T = _tile(S, 512)
    u8, su = _fp8_tensor(u)
    v8, sv = _fp8_tensor(v)
    u3 = u8.reshape(N_KEYS, N_KEYS, D)
    v3 = v8.reshape(N_KEYS, N_KEYS, D)
    wspec = pl.BlockSpec((PEER_GI, PEER_GJ, D), lambda t, i, j: (i, j, 0))
    once = pl.Buffered(1)
    tspec = pl.BlockSpec((n_heads, N_KEYS, T), lambda t, i, j: (0, 0, t), pipeline_mode=once)
    return pl.pallas_call(
        _peer_expert_kernel,
        grid=(S // T, N_KEYS // PEER_GI, N_KEYS // PEER_GJ),
        in_specs=[pl.BlockSpec(memory_space=pltpu.SMEM),
                  pl.BlockSpec((T, D), lambda t, i, j: (t, 0), pipeline_mode=once),
                  pl.BlockSpec((1, T), lambda t, i, j: (0, t)),
                  wspec, wspec, tspec, tspec, tspec, tspec],
        out_specs=pl.BlockSpec((T, D), lambda t, i, j: (t, 0)),
        out_shape=jax.ShapeDtypeStruct((S, D), F32),
        scratch_shapes=[pltpu.VMEM((PEER_GI * PEER_GJ, T), F32), pltpu.VMEM((PEER_GI * PEER_GJ, T), F8)],
        compiler_params=_params(("arbitrary", "arbitrary", "arbitrary"), 56),
        name="peer_experts",
    )(jnp.stack([su, sv]), h8, h_scale.reshape(1, S), u3, v3, *tables)


def _final_kernel(x_ref, y_ref, gate_ref, g_ref, o_ref, *, normalize):
    x = x_ref[...] + gate_ref[...] * y_ref[...]
    if normalize:
        x = x * lax.rsqrt(jnp.mean(x * x, axis=-1, keepdims=True) + NORM_EPS) * g_ref[...]
    o_ref[...] = x


def final_norm(x, y, mod, gate_idx, g, normalize):
    S, D = x.shape
    tm = _tile(S, 256)
    blk = pl.BlockSpec((tm, D), lambda i: (i, 0))
    return pl.pallas_call(
        functools.partial(_final_kernel, normalize=normalize),
        grid=(S // tm,),
        in_specs=[blk, blk, pl.BlockSpec((1, D), lambda i: (0, gate_idx)), pl.BlockSpec((1, D), lambda i: (0, 0))],
        out_specs=blk,
        out_shape=jax.ShapeDtypeStruct((S, D), F32),
        compiler_params=_params(("arbitrary",), 48),
        name="final_norm",
    )(x, y, mod, g.reshape(1, D))


def kernel(x, c, positions, norm1_g, norm2_g, w_ada, b_ada, w_in, lam_q1, lam_k1, lam_q2, lam_k2,
           diff_subln_g, dil_out_g, w_out, peer_wq, peer_subkeys, peer_u, peer_v, final_g):
    B, S, D = x.shape
    depth = w_ada.shape[0]
    assert B == 1, "one sequence per call"
    diff_width = D // 2
    diff_heads = diff_width // (2 * HEAD_DIM)
    dil_heads = (D - diff_width) // HEAD_DIM
    xs = x.reshape(S, D)
    tables = rope_tables(positions.reshape(S))
    for l in range(depth):
        lam_init = 0.8 - 0.6 * math.exp(-0.3 * l)
        mod = ada_mod(c, w_ada[l], b_ada[l])
        h = norm_mod(xs, norm1_g[l], mod, 0, 1)
        w_in_l = w_in[l].astype(BF16)
        proj = in_proj(h, w_in_l, tables)
        v_t = v_proj_t(h, w_in_l)
        lam_rows = jnp.stack([lam_q1[l], lam_k1[l], lam_q2[l], lam_k2[l]])
        y_diff = diff_attn(proj, v_t, lam_rows, diff_subln_g[l], diff_heads, lam_init)
        y_dil = dil_attn(proj, v_t, dil_out_g[l], dil_heads, 2 * diff_width, diff_width)
        xs = out_proj(y_diff, y_dil, w_out[l].astype(BF16), xs, mod, 2)
        h, h8, h_scale = norm_mod_q(xs, norm2_g[l], mod, 3, 4)
        gates = peer_router(h, peer_wq[l].astype(BF16), peer_subkeys[l])
        y = peer_experts(h8, h_scale, peer_u[l], peer_v[l], gates)
        xs = final_norm(xs, y, mod, 5, final_g, normalize=(l == depth - 1))
    return xs.reshape(B, S, D)
```

```python
import functools
import math

import jax
import jax.numpy as jnp
from jax import lax
from jax.experimental import pallas as pl
from jax.experimental.pallas import tpu as pltpu

F32 = jnp.float32
BF16 = jnp.bfloat16
F8 = jnp.float8_e4m3fn
FP8_MAX = float(jnp.finfo(F8).max)
FP8_TINY = 1e-30

HEAD_DIM = 128
ROT_DIM = HEAD_DIM // 4
ROT_HALF = ROT_DIM // 2
ROPE_THETA = 500000.0
DILATED_CONFIGS = ((128, 1), (512, 4), (2048, 16))
DIL_BACK = max(w for w, _ in DILATED_CONFIGS)
PEER_HEADS = 8
N_KEYS = 128
PEER_TOPK = 16
NORM_EPS = 1e-6
NEG_INF = -1e30
QK_SCALE = HEAD_DIM ** -0.5 * math.log2(math.e)
DIFF_ROW_SPLIT = 2
LANES = 128
MIB = 1024 * 1024

NT_DIMS = (((1,), (1,)), ((), ()))
TN_DIMS = (((0,), (0,)), ((), ()))


def _params(semantics, vmem_mib):
    return pltpu.CompilerParams(dimension_semantics=semantics, vmem_limit_bytes=vmem_mib * MIB)


def _tile(n, pref):
    t = min(n, pref)
    assert n % t == 0, (n, pref)
    return t


def _ada_kernel(c_ref, w_ref, b_ref, o_ref):
    c = c_ref[...]
    ca = c * jax.nn.sigmoid(c)
    for j in range(o_ref.shape[1] // LANES):
        sl = slice(j * LANES, (j + 1) * LANES)
        o_ref[:, sl] = jnp.sum(w_ref[:, sl] * ca, axis=0, keepdims=True) + b_ref[:, sl]


def ada_mod(c, w_ada, b_ada):
    D, N = w_ada.shape
    tn = _tile(N, 1024)
    cb = jnp.broadcast_to(c.reshape(D, 1), (D, LANES))
    return pl.pallas_call(
        _ada_kernel,
        grid=(N // tn,),
        in_specs=[pl.BlockSpec((D, LANES), lambda j: (0, 0)),
                  pl.BlockSpec((D, tn), lambda j: (0, j)),
                  pl.BlockSpec((1, tn), lambda j: (0, j))],
        out_specs=pl.BlockSpec((1, tn), lambda j: (0, j)),
        out_shape=jax.ShapeDtypeStruct((1, N), F32),
        compiler_params=_params(("arbitrary",), 48),
        name="ada_mod",
    )(cb, w_ada, b_ada.reshape(1, N))


def _rope_kernel(pos_ref, freq_ref, c_ref, s1_ref, s2_ref):
    ang = pos_ref[...].astype(F32) * freq_ref[...]
    lane = lax.broadcasted_iota(jnp.int32, ang.shape, 1)
    cs = jnp.cos(ang)
    sn = jnp.sin(ang)
    c_ref[...] = jnp.where(lane < ROT_DIM, cs, 1.0)
    s1_ref[...] = jnp.where(lane < ROT_HALF, -sn, 0.0)
    s2_ref[...] = jnp.where((lane >= ROT_HALF) & (lane < ROT_DIM), sn, 0.0)


def rope_tables(positions):
    S = positions.shape[0]
    ts = _tile(S, 1024)
    inv_freq = jnp.power(ROPE_THETA, -jnp.arange(0, ROT_DIM, 2, dtype=F32) / ROT_DIM)
    freq = jnp.tile(inv_freq, LANES // ROT_HALF).reshape(1, LANES)
    spec = pl.BlockSpec((ts, LANES), lambda i: (i, 0))
    return pl.pallas_call(
        _rope_kernel,
        grid=(S // ts,),
        in_specs=[pl.BlockSpec((ts, 1), lambda i: (i, 0)), pl.BlockSpec((1, LANES), lambda i: (0, 0))],
        out_specs=[spec, spec, spec],
        out_shape=[jax.ShapeDtypeStruct((S, LANES), F32)] * 3,
        compiler_params=_params(("arbitrary",), 32),
        name="rope_tables",
    )(positions.reshape(S, 1), freq)


def _rms_mod(x, g, scale, shift):
    y = x * lax.rsqrt(jnp.mean(x * x, axis=-1, keepdims=True) + NORM_EPS) * g
    return y * (1 + scale) + shift


def _norm_mod_kernel(x_ref, g_ref, sh_ref, sc_ref, o_ref):
    o_ref[...] = _rms_mod(x_ref[...], g_ref[...], sc_ref[...], sh_ref[...]).astype(o_ref.dtype)


def norm_mod(x, g, mod, shift_idx, scale_idx):
    S, D = x.shape
    tm = _tile(S, 256)
    return pl.pallas_call(
        _norm_mod_kernel,
        grid=(S // tm,),
        in_specs=[pl.BlockSpec((tm, D), lambda i: (i, 0)),
                  pl.BlockSpec((1, D), lambda i: (0, 0)),
                  pl.BlockSpec((1, D), lambda i: (0, shift_idx)),
                  pl.BlockSpec((1, D), lambda i: (0, scale_idx))],
        out_specs=pl.BlockSpec((tm, D), lambda i: (i, 0)),
        out_shape=jax.ShapeDtypeStruct((S, D), BF16),
        compiler_params=_params(("arbitrary",), 32),
        name="norm_mod",
    )(x, g.reshape(1, D), mod, mod)


def _norm_mod_q_kernel(x_ref, g_ref, sh_ref, sc_ref, o_ref, q_ref, s_ref):
    y = _rms_mod(x_ref[...], g_ref[...], sc_ref[...], sh_ref[...])
    o_ref[...] = y.astype(o_ref.dtype)
    scale = jnp.maximum(jnp.max(jnp.abs(y), axis=-1, keepdims=True), FP8_TINY) / FP8_MAX
    q_ref[...] = (y / scale).astype(q_ref.dtype)
    s_ref[...] = scale


def norm_mod_q(x, g, mod, shift_idx, scale_idx):
    S, D = x.shape
    tm = _tile(S, 256)
    blk = pl.BlockSpec((tm, D), lambda i: (i, 0))
    return pl.pallas_call(
        _norm_mod_q_kernel,
        grid=(S // tm,),
        in_specs=[blk,
                  pl.BlockSpec((1, D), lambda i: (0, 0)),
                  pl.BlockSpec((1, D), lambda i: (0, shift_idx)),
                  pl.BlockSpec((1, D), lambda i: (0, scale_idx))],
        out_specs=[blk, blk, pl.BlockSpec((tm, 1), lambda i: (i, 0))],
        out_shape=[jax.ShapeDtypeStruct((S, D), BF16), jax.ShapeDtypeStruct((S, D), F8),
                   jax.ShapeDtypeStruct((S, 1), F32)],
        compiler_params=_params(("arbitrary",), 32),
        name="norm_mod_q",
    )(x, g.reshape(1, D), mod, mod)


def _in_proj_kernel(h_ref, w_ref, c_ref, s1_ref, s2_ref, o_ref, *, per_sec):
    tn = o_ref.shape[1]
    acc = jnp.dot(h_ref[...], w_ref[...], preferred_element_type=F32)
    section = pl.program_id(1) // per_sec
    q_scale = jnp.where((section == 0) | (section == 2), QK_SCALE, 1.0).astype(F32)
    cs, s1, s2 = c_ref[...], s1_ref[...], s2_ref[...]
    for j in range(tn // HEAD_DIM):
        sl = slice(j * HEAD_DIM, (j + 1) * HEAD_DIM)
        t = acc[:, sl]
        r = (t * cs + pltpu.roll(t, HEAD_DIM - ROT_HALF, 1) * s1 + pltpu.roll(t, ROT_HALF, 1) * s2)
        o_ref[:, sl] = (r * q_scale).astype(o_ref.dtype)


def in_proj(h, w_in, tables):
    S, D = h.shape
    sec_width = w_in.shape[1] // 6
    tm = _tile(S, 1024)
    tn = _tile(sec_width, 512)
    per_sec = sec_width // tn
    src = lambda j: j + jnp.where(j >= 2 * per_sec, per_sec, 0)
    tspec = pl.BlockSpec((tm, LANES), lambda i, j: (i, 0))
    return pl.pallas_call(
        functools.partial(_in_proj_kernel, per_sec=per_sec),
        grid=(S // tm, 4 * per_sec),
        in_specs=[pl.BlockSpec((tm, D), lambda i, j: (i, 0)),
                  pl.BlockSpec((D, tn), lambda i, j: (0, src(j))),
                  tspec, tspec, tspec],
        out_specs=pl.BlockSpec((tm, tn), lambda i, j: (i, j)),
        out_shape=jax.ShapeDtypeStruct((S, 4 * sec_width), BF16),
        compiler_params=_params(("arbitrary", "arbitrary"), 48),
        name="in_proj",
    )(h, w_in, *tables)


def _v_proj_t_kernel(w_ref, h_ref, o_ref, wt_ref):
    @pl.when(pl.program_id(1) == 0)
    def _():
        wt_ref[...] = w_ref[...].T

    o_ref[...] = lax.dot_general(wt_ref[...], h_ref[...], NT_DIMS, preferred_element_type=F32).astype(o_ref.dtype)


def v_proj_t(h, w_in):
    S, D = h.shape
    sec_width = w_in.shape[1] // 6
    tn = _tile(sec_width, 512)
    ts = _tile(S, 1024)
    per_sec = sec_width // tn
    src = lambda j: j + jnp.where(j >= per_sec, 4 * per_sec, 2 * per_sec)
    return pl.pallas_call(
        _v_proj_t_kernel,
        grid=(2 * per_sec, S // ts),
        in_specs=[pl.BlockSpec((D, tn), lambda j, i: (0, src(j))), pl.BlockSpec((ts, D), lambda j, i: (i, 0))],
        out_specs=pl.BlockSpec((tn, ts), lambda j, i: (j, i)),
        out_shape=jax.ShapeDtypeStruct((2 * sec_width, S), BF16),
        scratch_shapes=[pltpu.VMEM((tn, D), BF16)],
        compiler_params=_params(("arbitrary", "arbitrary"), 48),
        name="v_proj_t",
    )(w_in, h)


def _diff_attn_kernel(lam_ref, q_ref, k_ref, vt_ref, g_ref, o_ref,
                      m1, l1, a1, m2, l2, a2, *, tq, lam_init):
    i = pl.program_id(1)
    stats = ((m1, l1, a1), (m2, l2, a2))
    for m, l, a in stats:
        m[...] = jnp.full(m.shape, -jnp.inf, F32)
        l[...] = jnp.zeros(l.shape, F32)
        a[...] = jnp.zeros(a.shape, F32)

    def chunk(kv, masked):
        k0 = pl.multiple_of(kv * tq, tq)
        vt = vt_ref[:, pl.ds(k0, tq)]

        def scores(c):
            sl = slice(c * HEAD_DIM, (c + 1) * HEAD_DIM)
            return lax.dot_general(k_ref[pl.ds(k0, tq), sl], q_ref[:, sl], NT_DIMS, preferred_element_type=F32)

        s_next = scores(0)
        for c, (m, l, a) in enumerate(stats):
            s = s_next
            if c == 0:
                s_next = scores(1)
            if masked:
                keep = (lax.broadcasted_iota(jnp.int32, (tq, tq), 0) <= lax.broadcasted_iota(jnp.int32, (tq, tq), 1))
                s = jnp.where(keep, s, NEG_INF)
            m_prev = m[...]
            m_new = jnp.maximum(m_prev, jnp.max(s, axis=0, keepdims=True))
            alpha = jnp.exp2(m_prev - m_new)
            p = jnp.exp2(s - m_new)
            l[...] = alpha * l[...] + jnp.sum(p, axis=0, keepdims=True)
            a[...] = alpha * a[...] + jnp.dot(vt, p.astype(vt.dtype), preferred_element_type=F32)
            m[...] = m_new

    def body(kv, carry):
        chunk(kv, False)
        return carry

    lax.fori_loop(0, i, body, 0)
    chunk(i, True)

    lp = lam_ref[...]
    lam = (jnp.exp(jnp.sum(lp[0:1] * lp[1:2], axis=-1, keepdims=True))
           - jnp.exp(jnp.sum(lp[2:3] * lp[3:4], axis=-1, keepdims=True)) + lam_init)
    o = (a1[...] / l1[...] - lam * (a2[...] / l2[...])).T
    y = o * lax.rsqrt(jnp.mean(o * o, axis=-1, keepdims=True) + NORM_EPS) * g_ref[...]
    o_ref[...] = (y * (1 - lam_init)).astype(o_ref.dtype)


def diff_attn(proj, v_t, lam_rows, subln_g, n_heads, lam_init):
    S = proj.shape[0]
    tq = _tile(S, 512)
    hw = 2 * HEAD_DIM
    return pl.pallas_call(
        functools.partial(_diff_attn_kernel, tq=tq, lam_init=lam_init),
        grid=(n_heads, S // tq),
        in_specs=[pl.BlockSpec((4, HEAD_DIM), lambda h, i: (0, 0)),
                  pl.BlockSpec((tq, hw), lambda h, i: (i, h)),
                  pl.BlockSpec((S, hw), lambda h, i: (0, n_heads + h)),
                  pl.BlockSpec((hw, S), lambda h, i: (h, 0)),
                  pl.BlockSpec((1, hw), lambda h, i: (0, 0))],
        out_specs=pl.BlockSpec((tq, hw), lambda h, i: (i, h)),
        out_shape=jax.ShapeDtypeStruct((S, n_heads * hw), BF16),
        scratch_shapes=[pltpu.VMEM((1, tq), F32), pltpu.VMEM((1, tq), F32), pltpu.VMEM((hw, tq), F32),
                        pltpu.VMEM((1, tq), F32), pltpu.VMEM((1, tq), F32), pltpu.VMEM((hw, tq), F32)],
        compiler_params=_params(("arbitrary", "arbitrary"), 48),
        name="diff_attn",
    )(lam_rows, proj, proj, v_t, subln_g.reshape(1, hw))


def _dil_bias(delta):
    count = jnp.zeros(delta.shape, F32)
    for window, dilation in DILATED_CONFIGS:
        ok = (delta >= 0) & (delta <= window) & ((delta & (dilation - 1)) == 0)
        count = count + jnp.where(ok, 1.0, 0.0)
    return jnp.where(count > 0, jnp.log2(jnp.maximum(count, 1.0)), NEG_INF)


DIL_HEADS_PER_STEP = 2


def _dil_attn_kernel(q_ref, k_ref, vt_ref, g_ref, o_ref, bias_ref, *, tq, win):
    i = pl.program_id(1)

    @pl.when((pl.program_id(0) == 0) & (i == 0))
    def _():
        kj = lax.broadcasted_iota(jnp.int32, bias_ref.shape, 0)
        qi = lax.broadcasted_iota(jnp.int32, bias_ref.shape, 1)
        bias_ref[...] = _dil_bias(qi + DIL_BACK - kj)

    start = pl.multiple_of(jnp.maximum(i * tq - DIL_BACK, 0), tq)
    row0 = pl.multiple_of(jnp.maximum(DIL_BACK - i * tq, 0), tq)
    bias = bias_ref[pl.ds(row0, win), :]
    heads = q_ref.shape[1] // HEAD_DIM

    def scores(hd):
        sl = slice(hd * HEAD_DIM, (hd + 1) * HEAD_DIM)
        return lax.dot_general(k_ref[pl.ds(start, win), sl], q_ref[:, sl], NT_DIMS,
                               preferred_element_type=F32) + bias

    s_next = scores(0)
    for hd in range(heads):
        sl = slice(hd * HEAD_DIM, (hd + 1) * HEAD_DIM)
        s = s_next
        if hd + 1 < heads:
            s_next = scores(hd + 1)
        p = jnp.exp2(s - jnp.max(s, axis=0, keepdims=True))
        vtw = vt_ref[sl, pl.ds(start, win)]
        o_t = jnp.dot(vtw, p.astype(vtw.dtype), preferred_element_type=F32) / jnp.sum(p, axis=0, keepdims=True)
        o = o_t.T
        y = o * lax.rsqrt(jnp.mean(o * o, axis=-1, keepdims=True) + NORM_EPS) * g_ref[:, sl]
        o_ref[:, sl] = y.astype(o_ref.dtype)


def dil_attn(proj, v_t, out_g, n_heads, col0, row0):
    S = proj.shape[0]
    for window, dilation in DILATED_CONFIGS:
        assert window % dilation == 0 and dilation & (dilation - 1) == 0
    tq = _tile(S, 256)
    win = DIL_BACK + tq
    hp = DIL_HEADS_PER_STEP
    hw = hp * HEAD_DIM
    assert S >= win and DIL_BACK % tq == 0 and win % LANES == 0
    assert n_heads % hp == 0 and col0 % hw == 0 and row0 % hw == 0
    c0 = col0 // hw
    r0 = row0 // hw
    ng = n_heads // hp
    return pl.pallas_call(
        functools.partial(_dil_attn_kernel, tq=tq, win=win),
        grid=(ng, S // tq),
        in_specs=[pl.BlockSpec((tq, hw), lambda h, i: (i, c0 + h)),
                  pl.BlockSpec((S, hw), lambda h, i: (0, c0 + ng + h)),
                  pl.BlockSpec((hw, S), lambda h, i: (r0 + h, 0)),
                  pl.BlockSpec((1, hw), lambda h, i: (0, h))],
        out_specs=pl.BlockSpec((tq, hw), lambda h, i: (i, h)),
        out_shape=jax.ShapeDtypeStruct((S, n_heads * HEAD_DIM), BF16),
        scratch_shapes=[pltpu.VMEM((win + DIL_BACK, tq), F32)],
        compiler_params=_params(("arbitrary", "arbitrary"), 48),
        name="dil_attn",
    )(proj, proj, v_t, out_g.reshape(1, n_heads * HEAD_DIM))


def _out_proj_kernel(ya_ref, yb_ref, wa_ref, wb_ref, x_ref, gate_ref, x1_ref):
    acc = (jnp.dot(ya_ref[...], wa_ref[...], preferred_element_type=F32)
           + jnp.dot(yb_ref[...], wb_ref[...], preferred_element_type=F32))
    x1_ref[...] = x_ref[...] + gate_ref[...] * acc


def out_proj(ya, yb, w_out, x, mod, gate_idx):
    S, D = x.shape
    kh = ya.shape[1]
    assert yb.shape[1] == kh and w_out.shape[0] == 2 * kh
    tm = _tile(S, 1024)
    tn = _tile(D, 512)
    nj = D // tn
    return pl.pallas_call(
        _out_proj_kernel,
        grid=(S // tm, nj),
        in_specs=[pl.BlockSpec((tm, kh), lambda i, j: (i, 0)),
                  pl.BlockSpec((tm, kh), lambda i, j: (i, 0)),
                  pl.BlockSpec((kh, tn), lambda i, j: (0, j)),
                  pl.BlockSpec((kh, tn), lambda i, j: (1, j)),
                  pl.BlockSpec((tm, tn), lambda i, j: (i, j)),
                  pl.BlockSpec((1, tn), lambda i, j: (0, gate_idx * nj + j))],
        out_specs=pl.BlockSpec((tm, tn), lambda i, j: (i, j)),
        out_shape=jax.ShapeDtypeStruct((S, D), F32),
        compiler_params=_params(("arbitrary", "arbitrary"), 48),
        name="out_proj",
    )(ya, yb, w_out, w_out, x, mod)


def _top16(x):
    rows = lax.broadcasted_iota(jnp.int32, x.shape, 0)
    rank = jnp.full(x.shape, PEER_TOPK, jnp.int32)
    vals = []
    for r in range(PEER_TOPK):
        m = jnp.max(x, axis=0, keepdims=True)
        idx = jnp.min(jnp.where(x == m, rows, x.shape[0]), axis=0, keepdims=True)
        hit = rows == idx
        rank = jnp.where(hit, r, rank)
        x = jnp.where(hit, -jnp.inf, x)
        vals.append(m)
    return jnp.concatenate(vals, axis=0), rank


def _peer_router_kernel(h_ref, wq_ref, sk_ref, rank1_ref, cnt0_ref, e0_ref, e1_ref):
    q = jnp.dot(h_ref[...], wq_ref[...], preferred_element_type=F32)
    half = q.shape[1] // 2
    s = [lax.dot_general(sk_ref[0, p], q[:, p * half:(p + 1) * half], NT_DIMS,
                         precision=lax.Precision.HIGHEST, preferred_element_type=F32)
         for p in range(2)]
    a, rank0 = _top16(s[0])
    b, rank1 = _top16(s[1])
    n_q = [PEER_TOPK // (p + 1) for p in range(PEER_TOPK)]
    pad = -sum(n_q) % 8
    cand = jnp.concatenate([a[p:p + 1] + b[:n_q[p]] for p in range(PEER_TOPK)]
                           + [jnp.full((pad, a.shape[1]), -jnp.inf, F32)], axis=0)
    fin, sel_rank = _top16(cand)
    sel = jnp.where(sel_rank < PEER_TOPK, 1.0, 0.0)
    starts = [sum(n_q[:p]) for p in range(PEER_TOPK)]
    cnt = jnp.concatenate([jnp.sum(sel[starts[p]:starts[p] + n_q[p]], axis=0, keepdims=True)
                           for p in range(PEER_TOPK)], axis=0)
    cnt0 = jnp.zeros(s[0].shape, F32)
    for p in range(PEER_TOPK):
        cnt0 = jnp.where(rank0 == p, cnt[p:p + 1], cnt0)
    z = jnp.sum(jnp.exp(fin - fin[0:1]), axis=0, keepdims=True)
    rank1_ref[0] = rank1.astype(F32).astype(BF16)
    cnt0_ref[0] = cnt0.astype(BF16)
    e0_ref[0] = jnp.exp(s[0] - a[0:1]).astype(BF16)
    e1_ref[0] = (jnp.exp(s[1] - b[0:1]) / z).astype(BF16)


def peer_router(h, w_q, subkeys):
    S, D = h.shape
    n_heads = subkeys.shape[0]
    qd = w_q.shape[1] // n_heads
    T = _tile(S, 512)
    ospec = pl.BlockSpec((1, N_KEYS, T), lambda t, hd: (hd, 0, t))
    return pl.pallas_call(
        _peer_router_kernel,
        grid=(S // T, n_heads),
        in_specs=[pl.BlockSpec((T, D), lambda t, hd: (t, 0)),
                  pl.BlockSpec((D, qd), lambda t, hd: (0, hd)),
                  pl.BlockSpec((1, 2, N_KEYS, qd // 2), lambda t, hd: (hd, 0, 0, 0))],
        out_specs=[ospec] * 4,
        out_shape=[jax.ShapeDtypeStruct((n_heads, N_KEYS, S), BF16)] * 4,
        compiler_params=_params(("arbitrary", "arbitrary"), 48),
        name="peer_router",
    )(h, w_q, subkeys)


def _gelu(x):
    return 0.5 * x * (1.0 + lax.erf(x * (0.5 ** 0.5)))


PEER_GI = 8
PEER_GJ = 64
PEER_CHUNKS = 4


def _peer_expert_kernel(qs_ref, h_ref, hs_ref, u_ref, v_ref, rank1_ref, cnt0_ref, e0_ref, e1_ref, o_ref,
                        wf_ref, aa_ref, ab_ref, w8a_ref, w8b_ref, wsa_ref, wsb_ref, *, nj):
    e = pl.program_id(1)
    ne = pl.num_programs(1) - 2
    gi, gj, D = u_ref.shape
    n_heads, _, T = rank1_ref.shape
    eb = gi * gj
    kc = D // PEER_CHUNKS
    assert T == PEER_CHUNKS * LANES
    ew = jnp.clip(e - 1, 0, ne - 1)
    i0 = pl.multiple_of((ew // nj) * gi, gi)
    j0 = pl.multiple_of((ew % nj) * gj, gj)

    @pl.when(e == 0)
    def _():
        o_ref[...] = jnp.zeros(o_ref.shape, F32)
        ab_ref[...] = jnp.zeros(ab_ref.shape, F32)
        w8a_ref[...] = jnp.zeros(w8a_ref.shape, w8a_ref.dtype)
        wsa_ref[...] = jnp.zeros(wsa_ref.shape, F32)

    def step(a_next, a_cur, w8_cur, ws_cur, w8_prev, ws_prev):
        y_scale = ws_prev[0:1, 0:1] * (qs_ref[1] / FP8_MAX)

        def piece(c, w_max):
            cols = pl.ds(pl.multiple_of(c * kc, kc), kc)
            part = lax.dot_general(u_ref[:, :, cols].reshape(eb, kc), h_ref[:, cols], NT_DIMS,
                                   preferred_element_type=F32)
            a_next[...] = jnp.where(c == 0, part, a_next[...] + part)
            y = lax.dot_general(w8_prev[...], v_ref[:, :, cols].reshape(eb, kc), TN_DIMS,
                                preferred_element_type=F32)
            o_ref[:, cols] += y * y_scale
            lsl = pl.ds(pl.multiple_of(c * LANES, LANES), LANES)
            act = _gelu(a_cur[:, lsl] * (hs_ref[:, lsl] * qs_ref[0]))
            c0 = [cnt0_ref[hd, pl.ds(i0, gi), lsl] for hd in range(n_heads)]
            e0 = [e0_ref[hd, pl.ds(i0, gi), lsl] for hd in range(n_heads)]
            for ii in range(gi):
                rsl = slice(ii * gj, (ii + 1) * gj)
                g = jnp.zeros((gj, LANES), BF16)
                for hd in range(n_heads):
                    r1 = rank1_ref[hd, pl.ds(j0, gj), lsl]
                    e1 = e1_ref[hd, pl.ds(j0, gj), lsl]
                    g = g + jnp.where(r1 < c0[hd][ii:ii + 1], e1 * e0[hd][ii:ii + 1], jnp.zeros((), BF16))
                w = g.astype(F32) * act[rsl]
                wf_ref[rsl, lsl] = w
                w_max = jnp.maximum(w_max, jnp.abs(w))
            return w_max

        w_max = lax.fori_loop(0, PEER_CHUNKS, piece, jnp.zeros((gj, LANES), F32))
        w_amax = jnp.maximum(jnp.max(w_max, axis=(0, 1), keepdims=True), FP8_TINY)
        w8_cur[...] = (wf_ref[...] * (FP8_MAX / w_amax)).astype(w8_cur.dtype)
        ws_cur[...] = jnp.broadcast_to(w_amax, ws_cur.shape)

    pl.when(e % 2 == 0)(lambda: step(aa_ref, ab_ref, w8b_ref, wsb_ref, w8a_ref, wsa_ref))
    pl.when(e % 2 == 1)(lambda: step(ab_ref, aa_ref, w8a_ref, wsa_ref, w8b_ref, wsb_ref))


def _fp8_tensor(w):
    scale = jnp.maximum(jnp.max(jnp.abs(w)), FP8_TINY) / FP8_MAX
    return (w / scale).astype(F8), scale


def peer_experts(h8, h_scale, u, v, tables):
    S, D = h8.shape
    n_heads = tables[0].shape[0]
    T = _tile(S, 512)
    u8, su = _fp8_tensor(u)
    v8, sv = _fp8_tensor(v)
    u3 = u8.reshape(N_KEYS, N_KEYS, D)
    v3 = v8.reshape(N_KEYS, N_KEYS, D)
    ni, nj = N_KEYS // PEER_GI, N_KEYS // PEER_GJ
    ne = ni * nj
    eb = PEER_GI * PEER_GJ

    def u_idx(t, e):
        eu = jnp.minimum(e, ne - 1)
        return eu // nj, eu % nj, 0

    def v_idx(t, e):
        ev = jnp.maximum(e - 2, 0)
        return ev // nj, ev % nj, 0

    once = pl.Buffered(1)
    tspec = pl.BlockSpec((n_heads, N_KEYS, T), lambda t, e: (0, 0, t), pipeline_mode=once)
    a_buf = pltpu.VMEM((eb, T), F32)
    w8_buf = pltpu.VMEM((eb, T), F8)
    ws_buf = pltpu.VMEM((8, LANES), F32)
    return pl.pallas_call(
        functools.partial(_peer_expert_kernel, nj=nj),
        grid=(S // T, ne + 2),
        in_specs=[pl.BlockSpec(memory_space=pltpu.SMEM),
                  pl.BlockSpec((T, D), lambda t, e: (t, 0), pipeline_mode=once),
                  pl.BlockSpec((1, T), lambda t, e: (0, t)),
                  pl.BlockSpec((PEER_GI, PEER_GJ, D), u_idx),
                  pl.BlockSpec((PEER_GI, PEER_GJ, D), v_idx),
                  tspec, tspec, tspec, tspec],
        out_specs=pl.BlockSpec((T, D), lambda t, e: (t, 0)),
        out_shape=jax.ShapeDtypeStruct((S, D), F32),
        scratch_shapes=[a_buf, a_buf, a_buf, w8_buf, w8_buf, ws_buf, ws_buf],
        compiler_params=_params(("arbitrary", "arbitrary"), 56),
        name="peer_experts",
    )(jnp.stack([su, sv]), h8, h_scale.reshape(1, S), u3, v3, *tables)


def _final_kernel(x_ref, y_ref, gate_ref, g_ref, o_ref, *, normalize):
    x = x_ref[...] + gate_ref[...] * y_ref[...]
    if normalize:
        x = x * lax.rsqrt(jnp.mean(x * x, axis=-1, keepdims=True) + NORM_EPS) * g_ref[...]
    o_ref[...] = x


def final_norm(x, y, mod, gate_idx, g, normalize):
    S, D = x.shape
    tm = _tile(S, 256)
    blk = pl.BlockSpec((tm, D), lambda i: (i, 0))
    return pl.pallas_call(
        functools.partial(_final_kernel, normalize=normalize),
        grid=(S // tm,),
        in_specs=[blk, blk, pl.BlockSpec((1, D), lambda i: (0, gate_idx)), pl.BlockSpec((1, D), lambda i: (0, 0))],
        out_specs=blk,
        out_shape=jax.ShapeDtypeStruct((S, D), F32),
        compiler_params=_params(("arbitrary",), 48),
        name="final_norm",
    )(x, y, mod, g.reshape(1, D))


def kernel(x, c, positions, norm1_g, norm2_g, w_ada, b_ada, w_in, lam_q1, lam_k1, lam_q2, lam_k2,
           diff_subln_g, dil_out_g, w_out, peer_wq, peer_subkeys, peer_u, peer_v, final_g):
    B, S, D = x.shape
    depth = w_ada.shape[0]
    assert B == 1, "one sequence per call"
    diff_width = D // 2
    diff_heads = diff_width // (2 * HEAD_DIM)
    dil_heads = (D - diff_width) // HEAD_DIM
    xs = x.reshape(S, D)
    tables = rope_tables(positions.reshape(S))
    for l in range(depth):
        lam_init = 0.8 - 0.6 * math.exp(-0.3 * l)
        mod = ada_mod(c, w_ada[l], b_ada[l])
        h = norm_mod(xs, norm1_g[l], mod, 0, 1)
        w_in_l = w_in[l].astype(BF16)
        proj = in_proj(h, w_in_l, tables)
        v_t = v_proj_t(h, w_in_l)
        lam_rows = jnp.stack([lam_q1[l], lam_k1[l], lam_q2[l], lam_k2[l]])
        y_diff = diff_attn(proj, v_t, lam_rows, diff_subln_g[l], diff_heads, lam_init)
        y_dil = dil_attn(proj, v_t, dil_out_g[l], dil_heads, 2 * diff_width, diff_width)
        xs = out_proj(y_diff, y_dil, w_out[l].astype(BF16), xs, mod, 2)
        h, h8, h_scale = norm_mod_q(xs, norm2_g[l], mod, 3, 4)
        gates = peer_router(h, peer_wq[l].astype(BF16), peer_subkeys[l])
        y = peer_experts(h8, h_scale, peer_u[l], peer_v[l], gates)
        xs = final_norm(xs, y, mod, 5, final_g, normalize=(l == depth - 1))
    return xs.reshape(B, S, D)
```

```python
import functools
import math

import jax
import jax.numpy as jnp
from jax import lax
from jax.experimental import pallas as pl
from jax.experimental.pallas import tpu as pltpu

F32 = jnp.float32
BF16 = jnp.bfloat16
F8 = jnp.float8_e4m3fn
FP8_MAX = float(jnp.finfo(F8).max)
FP8_TINY = 1e-30

HEAD_DIM = 128
ROT_DIM = HEAD_DIM // 4
ROT_HALF = ROT_DIM // 2
ROPE_THETA = 500000.0
DILATED_CONFIGS = ((128, 1), (512, 4), (2048, 16))
DIL_BACK = max(w for w, _ in DILATED_CONFIGS)
PEER_HEADS = 8
N_KEYS = 128
PEER_TOPK = 16
NORM_EPS = 1e-6
NEG_INF = -1e30
QK_SCALE = HEAD_DIM ** -0.5 * math.log2(math.e)
DIFF_ROW_SPLIT = 2
LANES = 128
MIB = 1024 * 1024

NT_DIMS = (((1,), (1,)), ((), ()))
TN_DIMS = (((0,), (0,)), ((), ()))


def _params(semantics, vmem_mib):
    return pltpu.CompilerParams(dimension_semantics=semantics, vmem_limit_bytes=vmem_mib * MIB)


def _tile(n, pref):
    t = min(n, pref)
    assert n % t == 0, (n, pref)
    return t


def _ada_kernel(c_ref, w_ref, b_ref, o_ref):
    c = c_ref[...]
    ca = c * jax.nn.sigmoid(c)
    for j in range(o_ref.shape[1] // LANES):
        sl = slice(j * LANES, (j + 1) * LANES)
        o_ref[:, sl] = jnp.sum(w_ref[:, sl] * ca, axis=0, keepdims=True) + b_ref[:, sl]


def ada_mod(c, w_ada, b_ada):
    D, N = w_ada.shape
    tn = _tile(N, 1024)
    cb = jnp.broadcast_to(c.reshape(D, 1), (D, LANES))
    return pl.pallas_call(
        _ada_kernel,
        grid=(N // tn,),
        in_specs=[pl.BlockSpec((D, LANES), lambda j: (0, 0)),
                  pl.BlockSpec((D, tn), lambda j: (0, j)),
                  pl.BlockSpec((1, tn), lambda j: (0, j))],
        out_specs=pl.BlockSpec((1, tn), lambda j: (0, j)),
        out_shape=jax.ShapeDtypeStruct((1, N), F32),
        compiler_params=_params(("arbitrary",), 48),
        name="ada_mod",
    )(cb, w_ada, b_ada.reshape(1, N))


def _rope_kernel(pos_ref, freq_ref, c_ref, s1_ref, s2_ref):
    ang = pos_ref[...].astype(F32) * freq_ref[...]
    lane = lax.broadcasted_iota(jnp.int32, ang.shape, 1)
    cs = jnp.cos(ang)
    sn = jnp.sin(ang)
    c_ref[...] = jnp.where(lane < ROT_DIM, cs, 1.0)
    s1_ref[...] = jnp.where(lane < ROT_HALF, -sn, 0.0)
    s2_ref[...] = jnp.where((lane >= ROT_HALF) & (lane < ROT_DIM), sn, 0.0)


def rope_tables(positions):
    S = positions.shape[0]
    ts = _tile(S, 1024)
    inv_freq = jnp.power(ROPE_THETA, -jnp.arange(0, ROT_DIM, 2, dtype=F32) / ROT_DIM)
    freq = jnp.tile(inv_freq, LANES // ROT_HALF).reshape(1, LANES)
    spec = pl.BlockSpec((ts, LANES), lambda i: (i, 0))
    return pl.pallas_call(
        _rope_kernel,
        grid=(S // ts,),
        in_specs=[pl.BlockSpec((ts, 1), lambda i: (i, 0)), pl.BlockSpec((1, LANES), lambda i: (0, 0))],
        out_specs=[spec, spec, spec],
        out_shape=[jax.ShapeDtypeStruct((S, LANES), F32)] * 3,
        compiler_params=_params(("arbitrary",), 32),
        name="rope_tables",
    )(positions.reshape(S, 1), freq)


def _rms_mod(x, g, scale, shift):
    y = x * lax.rsqrt(jnp.mean(x * x, axis=-1, keepdims=True) + NORM_EPS) * g
    return y * (1 + scale) + shift


def _norm_mod_kernel(x_ref, g_ref, sh_ref, sc_ref, o_ref):
    o_ref[...] = _rms_mod(x_ref[...], g_ref[...], sc_ref[...], sh_ref[...]).astype(o_ref.dtype)


def norm_mod(x, g, mod, shift_idx, scale_idx):
    S, D = x.shape
    tm = _tile(S, 256)
    return pl.pallas_call(
        _norm_mod_kernel,
        grid=(S // tm,),
        in_specs=[pl.BlockSpec((tm, D), lambda i: (i, 0)),
                  pl.BlockSpec((1, D), lambda i: (0, 0)),
                  pl.BlockSpec((1, D), lambda i: (0, shift_idx)),
                  pl.BlockSpec((1, D), lambda i: (0, scale_idx))],
        out_specs=pl.BlockSpec((tm, D), lambda i: (i, 0)),
        out_shape=jax.ShapeDtypeStruct((S, D), BF16),
        compiler_params=_params(("arbitrary",), 32),
        name="norm_mod",
    )(x, g.reshape(1, D), mod, mod)


def _norm_mod_q_kernel(x_ref, g_ref, sh_ref, sc_ref, o_ref, q_ref, s_ref):
    y = _rms_mod(x_ref[...], g_ref[...], sc_ref[...], sh_ref[...])
    o_ref[...] = y.astype(o_ref.dtype)
    scale = jnp.maximum(jnp.max(jnp.abs(y), axis=-1, keepdims=True), FP8_TINY) / FP8_MAX
    q_ref[...] = (y / scale).astype(q_ref.dtype)
    s_ref[...] = scale


def norm_mod_q(x, g, mod, shift_idx, scale_idx):
    S, D = x.shape
    tm = _tile(S, 256)
    blk = pl.BlockSpec((tm, D), lambda i: (i, 0))
    return pl.pallas_call(
        _norm_mod_q_kernel,
        grid=(S // tm,),
        in_specs=[blk,
                  pl.BlockSpec((1, D), lambda i: (0, 0)),
                  pl.BlockSpec((1, D), lambda i: (0, shift_idx)),
                  pl.BlockSpec((1, D), lambda i: (0, scale_idx))],
        out_specs=[blk, blk, pl.BlockSpec((tm, 1), lambda i: (i, 0))],
        out_shape=[jax.ShapeDtypeStruct((S, D), BF16), jax.ShapeDtypeStruct((S, D), F8),
                   jax.ShapeDtypeStruct((S, 1), F32)],
        compiler_params=_params(("arbitrary",), 32),
        name="norm_mod_q",
    )(x, g.reshape(1, D), mod, mod)


def _in_proj_kernel(h_ref, w_ref, c_ref, s1_ref, s2_ref, o_ref, *, per_sec):
    tn = o_ref.shape[1]
    acc = jnp.dot(h_ref[...], w_ref[...].astype(h_ref.dtype), preferred_element_type=F32)
    section = pl.program_id(1) // per_sec
    q_scale = jnp.where((section == 0) | (section == 2), QK_SCALE, 1.0).astype(F32)
    cs, s1, s2 = c_ref[...], s1_ref[...], s2_ref[...]
    for j in range(tn // HEAD_DIM):
        sl = slice(j * HEAD_DIM, (j + 1) * HEAD_DIM)
        t = acc[:, sl]
        r = (t * cs + pltpu.roll(t, HEAD_DIM - ROT_HALF, 1) * s1 + pltpu.roll(t, ROT_HALF, 1) * s2)
        o_ref[:, sl] = (r * q_scale).astype(o_ref.dtype)


def in_proj(h, w_in, tables):
    S, D = h.shape
    sec_width = w_in.shape[1] // 6
    tm = _tile(S, 1024)
    tn = _tile(sec_width, 512)
    per_sec = sec_width // tn
    src = lambda j: j + jnp.where(j >= 2 * per_sec, per_sec, 0)
    tspec = pl.BlockSpec((tm, LANES), lambda i, j: (i, 0))
    return pl.pallas_call(
        functools.partial(_in_proj_kernel, per_sec=per_sec),
        grid=(S // tm, 4 * per_sec),
        in_specs=[pl.BlockSpec((tm, D), lambda i, j: (i, 0)),
                  pl.BlockSpec((D, tn), lambda i, j: (0, src(j))),
                  tspec, tspec, tspec],
        out_specs=pl.BlockSpec((tm, tn), lambda i, j: (i, j)),
        out_shape=jax.ShapeDtypeStruct((S, 4 * sec_width), BF16),
        compiler_params=_params(("arbitrary", "arbitrary"), 48),
        name="in_proj",
    )(h, w_in, *tables)


def _v_proj_t_kernel(w_ref, h_ref, o_ref, wt_ref):
    @pl.when(pl.program_id(1) == 0)
    def _():
        wt_ref[...] = w_ref[...].astype(wt_ref.dtype).T

    o_ref[...] = lax.dot_general(wt_ref[...], h_ref[...], NT_DIMS, preferred_element_type=F32).astype(o_ref.dtype)


def v_proj_t(h, w_in):
    S, D = h.shape
    sec_width = w_in.shape[1] // 6
    tn = _tile(sec_width, 512)
    ts = _tile(S, 1024)
    per_sec = sec_width // tn
    src = lambda j: j + jnp.where(j >= per_sec, 4 * per_sec, 2 * per_sec)
    return pl.pallas_call(
        _v_proj_t_kernel,
        grid=(2 * per_sec, S // ts),
        in_specs=[pl.BlockSpec((D, tn), lambda j, i: (0, src(j))), pl.BlockSpec((ts, D), lambda j, i: (i, 0))],
        out_specs=pl.BlockSpec((tn, ts), lambda j, i: (j, i)),
        out_shape=jax.ShapeDtypeStruct((2 * sec_width, S), BF16),
        scratch_shapes=[pltpu.VMEM((tn, D), BF16)],
        compiler_params=_params(("arbitrary", "arbitrary"), 48),
        name="v_proj_t",
    )(w_in, h)


def _diff_attn_kernel(lam_ref, q_ref, k_ref, vt_ref, g_ref, o_ref,
                      m1, l1, a1, m2, l2, a2, *, tq, lam_init):
    i = pl.program_id(1)
    stats = ((m1, l1, a1), (m2, l2, a2))
    for m, l, a in stats:
        m[...] = jnp.full(m.shape, -jnp.inf, F32)
        l[...] = jnp.zeros(l.shape, F32)
        a[...] = jnp.zeros(a.shape, F32)

    def chunk(kv, masked):
        k0 = pl.multiple_of(kv * tq, tq)
        vt = vt_ref[:, pl.ds(k0, tq)]

        def scores(c):
            sl = slice(c * HEAD_DIM, (c + 1) * HEAD_DIM)
            return lax.dot_general(k_ref[pl.ds(k0, tq), sl], q_ref[:, sl], NT_DIMS, preferred_element_type=F32)

        s_next = scores(0)
        for c, (m, l, a) in enumerate(stats):
            s = s_next
            if c == 0:
                s_next = scores(1)
            if masked:
                keep = (lax.broadcasted_iota(jnp.int32, (tq, tq), 0) <= lax.broadcasted_iota(jnp.int32, (tq, tq), 1))
                s = jnp.where(keep, s, NEG_INF)
            m_prev = m[...]
            m_new = jnp.maximum(m_prev, jnp.max(s, axis=0, keepdims=True))
            alpha = jnp.exp2(m_prev - m_new)
            p = jnp.exp2(s - m_new)
            l[...] = alpha * l[...] + jnp.sum(p, axis=0, keepdims=True)
            a[...] = alpha * a[...] + jnp.dot(vt, p.astype(vt.dtype), preferred_element_type=F32)
            m[...] = m_new

    def body(kv, carry):
        chunk(kv, False)
        return carry

    lax.fori_loop(0, i, body, 0)
    chunk(i, True)

    lp = lam_ref[...]
    lam = (jnp.exp(jnp.sum(lp[0:1] * lp[1:2], axis=-1, keepdims=True))
           - jnp.exp(jnp.sum(lp[2:3] * lp[3:4], axis=-1, keepdims=True)) + lam_init)
    o = (a1[...] / l1[...] - lam * (a2[...] / l2[...])).T
    y = o * lax.rsqrt(jnp.mean(o * o, axis=-1, keepdims=True) + NORM_EPS) * g_ref[...]
    o_ref[...] = (y * (1 - lam_init)).astype(o_ref.dtype)


def diff_attn(proj, v_t, lam_rows, subln_g, n_heads, lam_init):
    S = proj.shape[0]
    tq = _tile(S, 512)
    hw = 2 * HEAD_DIM
    return pl.pallas_call(
        functools.partial(_diff_attn_kernel, tq=tq, lam_init=lam_init),
        grid=(n_heads, S // tq),
        in_specs=[pl.BlockSpec((4, HEAD_DIM), lambda h, i: (0, 0)),
                  pl.BlockSpec((tq, hw), lambda h, i: (i, h)),
                  pl.BlockSpec((S, hw), lambda h, i: (0, n_heads + h)),
                  pl.BlockSpec((hw, S), lambda h, i: (h, 0)),
                  pl.BlockSpec((1, hw), lambda h, i: (0, 0))],
        out_specs=pl.BlockSpec((tq, hw), lambda h, i: (i, h)),
        out_shape=jax.ShapeDtypeStruct((S, n_heads * hw), BF16),
        scratch_shapes=[pltpu.VMEM((1, tq), F32), pltpu.VMEM((1, tq), F32), pltpu.VMEM((hw, tq), F32),
                        pltpu.VMEM((1, tq), F32), pltpu.VMEM((1, tq), F32), pltpu.VMEM((hw, tq), F32)],
        compiler_params=_params(("arbitrary", "arbitrary"), 48),
        name="diff_attn",
    )(lam_rows, proj, proj, v_t, subln_g.reshape(1, hw))


def _dil_bias(delta):
    count = jnp.zeros(delta.shape, F32)
    for window, dilation in DILATED_CONFIGS:
        ok = (delta >= 0) & (delta <= window) & ((delta & (dilation - 1)) == 0)
        count = count + jnp.where(ok, 1.0, 0.0)
    return jnp.where(count > 0, jnp.log2(jnp.maximum(count, 1.0)), NEG_INF)


DIL_HEADS_PER_STEP = 2


def _dil_attn_kernel(q_ref, k_ref, vt_ref, g_ref, o_ref, bias_ref, *, tq, win):
    i = pl.program_id(1)

    @pl.when((pl.program_id(0) == 0) & (i == 0))
    def _():
        kj = lax.broadcasted_iota(jnp.int32, bias_ref.shape, 0)
        qi = lax.broadcasted_iota(jnp.int32, bias_ref.shape, 1)
        bias_ref[...] = _dil_bias(qi + DIL_BACK - kj)

    start = pl.multiple_of(jnp.maximum(i * tq - DIL_BACK, 0), tq)
    row0 = pl.multiple_of(jnp.maximum(DIL_BACK - i * tq, 0), tq)
    bias = bias_ref[pl.ds(row0, win), :]
    heads = q_ref.shape[1] // HEAD_DIM

    def scores(hd):
        sl = slice(hd * HEAD_DIM, (hd + 1) * HEAD_DIM)
        return lax.dot_general(k_ref[pl.ds(start, win), sl], q_ref[:, sl], NT_DIMS,
                               preferred_element_type=F32) + bias

    s_next = scores(0)
    for hd in range(heads):
        sl = slice(hd * HEAD_DIM, (hd + 1) * HEAD_DIM)
        s = s_next
        if hd + 1 < heads:
            s_next = scores(hd + 1)
        p = jnp.exp2(s - jnp.max(s, axis=0, keepdims=True))
        vtw = vt_ref[sl, pl.ds(start, win)]
        o_t = jnp.dot(vtw, p.astype(vtw.dtype), preferred_element_type=F32) / jnp.sum(p, axis=0, keepdims=True)
        o = o_t.T
        y = o * lax.rsqrt(jnp.mean(o * o, axis=-1, keepdims=True) + NORM_EPS) * g_ref[:, sl]
        o_ref[:, sl] = y.astype(o_ref.dtype)


def dil_attn(proj, v_t, out_g, n_heads, col0, row0):
    S = proj.shape[0]
    for window, dilation in DILATED_CONFIGS:
        assert window % dilation == 0 and dilation & (dilation - 1) == 0
    tq = _tile(S, 256)
    win = DIL_BACK + tq
    hp = DIL_HEADS_PER_STEP
    hw = hp * HEAD_DIM
    assert S >= win and DIL_BACK % tq == 0 and win % LANES == 0
    assert n_heads % hp == 0 and col0 % hw == 0 and row0 % hw == 0
    c0 = col0 // hw
    r0 = row0 // hw
    ng = n_heads // hp
    return pl.pallas_call(
        functools.partial(_dil_attn_kernel, tq=tq, win=win),
        grid=(ng, S // tq),
        in_specs=[pl.BlockSpec((tq, hw), lambda h, i: (i, c0 + h)),
                  pl.BlockSpec((S, hw), lambda h, i: (0, c0 + ng + h)),
                  pl.BlockSpec((hw, S), lambda h, i: (r0 + h, 0)),
                  pl.BlockSpec((1, hw), lambda h, i: (0, h))],
        out_specs=pl.BlockSpec((tq, hw), lambda h, i: (i, h)),
        out_shape=jax.ShapeDtypeStruct((S, n_heads * HEAD_DIM), BF16),
        scratch_shapes=[pltpu.VMEM((win + DIL_BACK, tq), F32)],
        compiler_params=_params(("arbitrary", "arbitrary"), 48),
        name="dil_attn",
    )(proj, proj, v_t, out_g.reshape(1, n_heads * HEAD_DIM))


def _out_proj_kernel(ya_ref, yb_ref, wa_ref, wb_ref, x_ref, gate_ref, x1_ref):
    acc = (jnp.dot(ya_ref[...], wa_ref[...].astype(ya_ref.dtype), preferred_element_type=F32)
           + jnp.dot(yb_ref[...], wb_ref[...].astype(yb_ref.dtype), preferred_element_type=F32))
    x1_ref[...] = x_ref[...] + gate_ref[...] * acc


def out_proj(ya, yb, w_out, x, mod, gate_idx):
    S, D = x.shape
    kh = ya.shape[1]
    assert yb.shape[1] == kh and w_out.shape[0] == 2 * kh
    tm = _tile(S, 1024)
    tn = _tile(D, 512)
    nj = D // tn
    return pl.pallas_call(
        _out_proj_kernel,
        grid=(S // tm, nj),
        in_specs=[pl.BlockSpec((tm, kh), lambda i, j: (i, 0)),
                  pl.BlockSpec((tm, kh), lambda i, j: (i, 0)),
                  pl.BlockSpec((kh, tn), lambda i, j: (0, j)),
                  pl.BlockSpec((kh, tn), lambda i, j: (1, j)),
                  pl.BlockSpec((tm, tn), lambda i, j: (i, j)),
                  pl.BlockSpec((1, tn), lambda i, j: (0, gate_idx * nj + j))],
        out_specs=pl.BlockSpec((tm, tn), lambda i, j: (i, j)),
        out_shape=jax.ShapeDtypeStruct((S, D), F32),
        compiler_params=_params(("arbitrary", "arbitrary"), 48),
        name="out_proj",
    )(ya, yb, w_out, w_out, x, mod)


def _top16(x):
    rows = lax.broadcasted_iota(jnp.int32, x.shape, 0)
    rank = jnp.full(x.shape, PEER_TOPK, jnp.int32)
    vals = []
    for r in range(PEER_TOPK):
        m = jnp.max(x, axis=0, keepdims=True)
        idx = jnp.min(jnp.where(x == m, rows, x.shape[0]), axis=0, keepdims=True)
        hit = rows == idx
        rank = jnp.where(hit, r, rank)
        x = jnp.where(hit, -jnp.inf, x)
        vals.append(m)
    return jnp.concatenate(vals, axis=0), rank


def _peer_router_kernel(h_ref, wq_ref, sk_ref, rank1_ref, cnt0_ref, e0_ref, e1_ref):
    q = jnp.dot(h_ref[...], wq_ref[...].astype(h_ref.dtype), preferred_element_type=F32)
    half = q.shape[1] // 2
    s = [lax.dot_general(sk_ref[0, p], q[:, p * half:(p + 1) * half], NT_DIMS,
                         precision=lax.Precision.HIGHEST, preferred_element_type=F32)
         for p in range(2)]
    a, rank0 = _top16(s[0])
    b, rank1 = _top16(s[1])
    n_q = [PEER_TOPK // (p + 1) for p in range(PEER_TOPK)]
    pad = -sum(n_q) % 8
    cand = jnp.concatenate([a[p:p + 1] + b[:n_q[p]] for p in range(PEER_TOPK)]
                           + [jnp.full((pad, a.shape[1]), -jnp.inf, F32)], axis=0)
    fin, sel_rank = _top16(cand)
    sel = jnp.where(sel_rank < PEER_TOPK, 1.0, 0.0)
    starts = [sum(n_q[:p]) for p in range(PEER_TOPK)]
    cnt = jnp.concatenate([jnp.sum(sel[starts[p]:starts[p] + n_q[p]], axis=0, keepdims=True)
                           for p in range(PEER_TOPK)], axis=0)
    cnt0 = jnp.zeros(s[0].shape, F32)
    for p in range(PEER_TOPK):
        cnt0 = jnp.where(rank0 == p, cnt[p:p + 1], cnt0)
    z = jnp.sum(jnp.exp(fin - fin[0:1]), axis=0, keepdims=True)
    rank1_ref[0] = rank1.astype(F32).astype(BF16)
    cnt0_ref[0] = cnt0.astype(BF16)
    e0_ref[0] = jnp.exp(s[0] - a[0:1]).astype(BF16)
    e1_ref[0] = (jnp.exp(s[1] - b[0:1]) / z).astype(BF16)


def peer_router(h, w_q, subkeys):
    S, D = h.shape
    n_heads = subkeys.shape[0]
    qd = w_q.shape[1] // n_heads
    T = _tile(S, 512)
    ospec = pl.BlockSpec((1, N_KEYS, T), lambda t, hd: (hd, 0, t))
    return pl.pallas_call(
        _peer_router_kernel,
        grid=(S // T, n_heads),
        in_specs=[pl.BlockSpec((T, D), lambda t, hd: (t, 0)),
                  pl.BlockSpec((D, qd), lambda t, hd: (0, hd)),
                  pl.BlockSpec((1, 2, N_KEYS, qd // 2), lambda t, hd: (hd, 0, 0, 0))],
        out_specs=[ospec] * 4,
        out_shape=[jax.ShapeDtypeStruct((n_heads, N_KEYS, S), BF16)] * 4,
        compiler_params=_params(("arbitrary", "arbitrary"), 48),
        name="peer_router",
    )(h, w_q, subkeys)


def _gelu(x):
    return 0.5 * x * (1.0 + lax.erf(x * (0.5 ** 0.5)))


PEER_GI = 8
PEER_GJ = 64
PEER_CHUNKS = 4


def _peer_expert_kernel(h_ref, hs_ref, u_ref, us_ref, v_ref, vs_ref, rank1_ref, cnt0_ref, e0_ref, e1_ref, o_ref,
                        wf_ref, aa_ref, ab_ref, w8a_ref, w8b_ref, wsa_ref, wsb_ref, *, nj):
    e = pl.program_id(1)
    ne = pl.num_programs(1) - 2
    gi, gj, D = u_ref.shape
    n_heads, _, T = rank1_ref.shape
    eb = gi * gj
    kc = D // PEER_CHUNKS
    assert T == PEER_CHUNKS * LANES
    ew = jnp.clip(e - 1, 0, ne - 1)
    i0 = pl.multiple_of((ew // nj) * gi, gi)
    j0 = pl.multiple_of((ew % nj) * gj, gj)

    @pl.when(e == 0)
    def _():
        o_ref[...] = jnp.zeros(o_ref.shape, F32)
        ab_ref[...] = jnp.zeros(ab_ref.shape, F32)
        w8a_ref[...] = jnp.zeros(w8a_ref.shape, w8a_ref.dtype)
        wsa_ref[...] = jnp.zeros(wsa_ref.shape, F32)

    def step(a_next, a_cur, w8_cur, ws_cur, w8_prev, ws_prev):
        y_scale = ws_prev[0:1, 0:1] * (1.0 / FP8_MAX)
        u_scale = us_ref[...].reshape(eb, LANES)
        v_scale = vs_ref[...].reshape(eb, LANES)

        def piece(c, w_max):
            cols = pl.ds(pl.multiple_of(c * kc, kc), kc)
            part = lax.dot_general(u_ref[:, :, cols].reshape(eb, kc), h_ref[:, cols], NT_DIMS,
                                   preferred_element_type=F32)
            a_next[...] = jnp.where(c == 0, part, a_next[...] + part)
            y = lax.dot_general(w8_prev[...], v_ref[:, :, cols].reshape(eb, kc), TN_DIMS,
                                preferred_element_type=F32)
            o_ref[:, cols] += y * y_scale
            lsl = pl.ds(pl.multiple_of(c * LANES, LANES), LANES)
            act = _gelu(a_cur[:, lsl] * (u_scale * hs_ref[:, lsl])) * v_scale
            c0 = [cnt0_ref[hd, pl.ds(i0, gi), lsl] for hd in range(n_heads)]
            e0 = [e0_ref[hd, pl.ds(i0, gi), lsl] for hd in range(n_heads)]
            for ii in range(gi):
                rsl = slice(ii * gj, (ii + 1) * gj)
                g = jnp.zeros((gj, LANES), BF16)
                for hd in range(n_heads):
                    r1 = rank1_ref[hd, pl.ds(j0, gj), lsl]
                    e1 = e1_ref[hd, pl.ds(j0, gj), lsl]
                    g = g + jnp.where(r1 < c0[hd][ii:ii + 1], e1 * e0[hd][ii:ii + 1], jnp.zeros((), BF16))
                w = g.astype(F32) * act[rsl]
                wf_ref[rsl, lsl] = w
                w_max = jnp.maximum(w_max, jnp.abs(w))
            return w_max

        w_max = lax.fori_loop(0, PEER_CHUNKS, piece, jnp.zeros((gj, LANES), F32))
        w_amax = jnp.maximum(jnp.max(w_max, axis=(0, 1), keepdims=True), FP8_TINY)
        w8_cur[...] = (wf_ref[...] * (FP8_MAX / w_amax)).astype(w8_cur.dtype)
        ws_cur[...] = jnp.broadcast_to(w_amax, ws_cur.shape)

    pl.when(e % 2 == 0)(lambda: step(aa_ref, ab_ref, w8b_ref, wsb_ref, w8a_ref, wsa_ref))
    pl.when(e % 2 == 1)(lambda: step(ab_ref, aa_ref, w8a_ref, wsa_ref, w8b_ref, wsb_ref))


def _quant_rows_kernel(w_ref, q_ref, s_ref):
    w = w_ref[...]
    scale = jnp.maximum(jnp.max(jnp.abs(w), axis=-1, keepdims=True), FP8_TINY) / FP8_MAX
    q_ref[...] = (w / scale).astype(q_ref.dtype)
    s_ref[...] = jnp.broadcast_to(scale, s_ref.shape)


def quant_rows(w):
    E, D = w.shape
    te = _tile(E, 512)
    return pl.pallas_call(
        _quant_rows_kernel,
        grid=(E // te,),
        in_specs=[pl.BlockSpec((te, D), lambda i: (i, 0))],
        out_specs=[pl.BlockSpec((te, D), lambda i: (i, 0)), pl.BlockSpec((te, LANES), lambda i: (i, 0))],
        out_shape=[jax.ShapeDtypeStruct((E, D), F8), jax.ShapeDtypeStruct((E, LANES), F32)],
        compiler_params=_params(("arbitrary",), 32),
        name="quant_rows",
    )(w)


def peer_experts(h8, h_scale, u, v, tables):
    S, D = h8.shape
    n_heads = tables[0].shape[0]
    T = _tile(S, 512)
    u8, su = quant_rows(u)
    v8, sv = quant_rows(v)
    u3 = u8.reshape(N_KEYS, N_KEYS, D)
    v3 = v8.reshape(N_KEYS, N_KEYS, D)
    su3 = su.reshape(N_KEYS, N_KEYS, LANES)
    sv3 = sv.reshape(N_KEYS, N_KEYS, LANES)
    ni, nj = N_KEYS // PEER_GI, N_KEYS // PEER_GJ
    ne = ni * nj
    eb = PEER_GI * PEER_GJ

    def u_idx(t, e):
        eu = jnp.minimum(e, ne - 1)
        return eu // nj, eu % nj, 0

    def w_idx(t, e):
        ew = jnp.clip(e - 1, 0, ne - 1)
        return ew // nj, ew % nj, 0

    def v_idx(t, e):
        ev = jnp.maximum(e - 2, 0)
        return ev // nj, ev % nj, 0

    once = pl.Buffered(1)
    tspec = pl.BlockSpec((n_heads, N_KEYS, T), lambda t, e: (0, 0, t), pipeline_mode=once)
    a_buf = pltpu.VMEM((eb, T), F32)
    w8_buf = pltpu.VMEM((eb, T), F8)
    ws_buf = pltpu.VMEM((8, LANES), F32)
    return pl.pallas_call(
        functools.partial(_peer_expert_kernel, nj=nj),
        grid=(S // T, ne + 2),
        in_specs=[pl.BlockSpec((T, D), lambda t, e: (t, 0), pipeline_mode=once),
                  pl.BlockSpec((1, T), lambda t, e: (0, t)),
                  pl.BlockSpec((PEER_GI, PEER_GJ, D), u_idx),
                  pl.BlockSpec((PEER_GI, PEER_GJ, LANES), w_idx),
                  pl.BlockSpec((PEER_GI, PEER_GJ, D), v_idx),
                  pl.BlockSpec((PEER_GI, PEER_GJ, LANES), w_idx),
                  tspec, tspec, tspec, tspec],
        out_specs=pl.BlockSpec((T, D), lambda t, e: (t, 0)),
        out_shape=jax.ShapeDtypeStruct((S, D), F32),
        scratch_shapes=[a_buf, a_buf, a_buf, w8_buf, w8_buf, ws_buf, ws_buf],
        compiler_params=_params(("arbitrary", "arbitrary"), 56),
        name="peer_experts",
    )(h8, h_scale.reshape(1, S), u3, su3, v3, sv3, *tables)


def _final_kernel(x_ref, y_ref, gate_ref, g_ref, o_ref, *, normalize):
    x = x_ref[...] + gate_ref[...] * y_ref[...]
    if normalize:
        x = x * lax.rsqrt(jnp.mean(x * x, axis=-1, keepdims=True) + NORM_EPS) * g_ref[...]
    o_ref[...] = x


def final_norm(x, y, mod, gate_idx, g, normalize):
    S, D = x.shape
    tm = _tile(S, 256)
    blk = pl.BlockSpec((tm, D), lambda i: (i, 0))
    return pl.pallas_call(
        functools.partial(_final_kernel, normalize=normalize),
        grid=(S // tm,),
        in_specs=[blk, blk, pl.BlockSpec((1, D), lambda i: (0, gate_idx)), pl.BlockSpec((1, D), lambda i: (0, 0))],
        out_specs=blk,
        out_shape=jax.ShapeDtypeStruct((S, D), F32),
        compiler_params=_params(("arbitrary",), 48),
        name="final_norm",
    )(x, y, mod, g.reshape(1, D))


def kernel(x, c, positions, norm1_g, norm2_g, w_ada, b_ada, w_in, lam_q1, lam_k1, lam_q2, lam_k2,
           diff_subln_g, dil_out_g, w_out, peer_wq, peer_subkeys, peer_u, peer_v, final_g):
    B, S, D = x.shape
    depth = w_ada.shape[0]
    assert B == 1, "one sequence per call"
    diff_width = D // 2
    diff_heads = diff_width // (2 * HEAD_DIM)
    dil_heads = (D - diff_width) // HEAD_DIM
    xs = x.reshape(S, D)
    tables = rope_tables(positions.reshape(S))
    for l in range(depth):
        lam_init = 0.8 - 0.6 * math.exp(-0.3 * l)
        mod = ada_mod(c, w_ada[l], b_ada[l])
        h = norm_mod(xs, norm1_g[l], mod, 0, 1)
        proj = in_proj(h, w_in[l], tables)
        v_t = v_proj_t(h, w_in[l])
        lam_rows = jnp.stack([lam_q1[l], lam_k1[l], lam_q2[l], lam_k2[l]])
        y_diff = diff_attn(proj, v_t, lam_rows, diff_subln_g[l], diff_heads, lam_init)
        y_dil = dil_attn(proj, v_t, dil_out_g[l], dil_heads, 2 * diff_width, diff_width)
        xs = out_proj(y_diff, y_dil, w_out[l], xs, mod, 2)
        h, h8, h_scale = norm_mod_q(xs, norm2_g[l], mod, 3, 4)
        gates = peer_router(h, peer_wq[l], peer_subkeys[l])
        y = peer_experts(h8, h_scale, peer_u[l], peer_v[l], gates)
        xs = final_norm(xs, y, mod, 5, final_g, normalize=(l == depth - 1))
    return xs.reshape(B, S, D)
```

```python
import functools
import math

import jax
import jax.numpy as jnp
from jax import lax
from jax.experimental import pallas as pl
from jax.experimental.pallas import tpu as pltpu

F32 = jnp.float32
BF16 = jnp.bfloat16
F8 = jnp.float8_e4m3fn
FP8_MAX = float(jnp.finfo(F8).max)
FP8_TINY = 1e-30

HEAD_DIM = 128
ROT_DIM = HEAD_DIM // 4
ROT_HALF = ROT_DIM // 2
ROPE_THETA = 500000.0
DILATED_CONFIGS = ((128, 1), (512, 4), (2048, 16))
DIL_BACK = max(w for w, _ in DILATED_CONFIGS)
PEER_HEADS = 8
N_KEYS = 128
PEER_TOPK = 16
NORM_EPS = 1e-6
NEG_INF = -1e30
QK_SCALE = HEAD_DIM ** -0.5 * math.log2(math.e)
DIFF_ROW_SPLIT = 2
LANES = 128
MIB = 1024 * 1024

NT_DIMS = (((1,), (1,)), ((), ()))
TN_DIMS = (((0,), (0,)), ((), ()))


def _params(semantics, vmem_mib):
    return pltpu.CompilerParams(dimension_semantics=semantics, vmem_limit_bytes=vmem_mib * MIB)


def _tile(n, pref):
    t = min(n, pref)
    assert n % t == 0, (n, pref)
    return t


def _ada_kernel(c_ref, w_ref, b_ref, o_ref):
    c = c_ref[...]
    ca = c * jax.nn.sigmoid(c)
    for j in range(o_ref.shape[1] // LANES):
        sl = slice(j * LANES, (j + 1) * LANES)
        o_ref[:, sl] = jnp.sum(w_ref[:, sl] * ca, axis=0, keepdims=True) + b_ref[:, sl]


def ada_mod(c, w_ada, b_ada):
    D, N = w_ada.shape
    tn = _tile(N, 1024)
    cb = jnp.broadcast_to(c.reshape(D, 1), (D, LANES))
    return pl.pallas_call(
        _ada_kernel,
        grid=(N // tn,),
        in_specs=[pl.BlockSpec((D, LANES), lambda j: (0, 0)),
                  pl.BlockSpec((D, tn), lambda j: (0, j)),
                  pl.BlockSpec((1, tn), lambda j: (0, j))],
        out_specs=pl.BlockSpec((1, tn), lambda j: (0, j)),
        out_shape=jax.ShapeDtypeStruct((1, N), F32),
        compiler_params=_params(("arbitrary",), 48),
        name="ada_mod",
    )(cb, w_ada, b_ada.reshape(1, N))


def _rope_kernel(pos_ref, freq_ref, c_ref, s1_ref, s2_ref):
    ang = pos_ref[...].astype(F32) * freq_ref[...]
    lane = lax.broadcasted_iota(jnp.int32, ang.shape, 1)
    cs = jnp.cos(ang)
    sn = jnp.sin(ang)
    c_ref[...] = jnp.where(lane < ROT_DIM, cs, 1.0)
    s1_ref[...] = jnp.where(lane < ROT_HALF, -sn, 0.0)
    s2_ref[...] = jnp.where((lane >= ROT_HALF) & (lane < ROT_DIM), sn, 0.0)


def rope_tables(positions):
    S = positions.shape[0]
    ts = _tile(S, 1024)
    inv_freq = jnp.power(ROPE_THETA, -jnp.arange(0, ROT_DIM, 2, dtype=F32) / ROT_DIM)
    freq = jnp.tile(inv_freq, LANES // ROT_HALF).reshape(1, LANES)
    spec = pl.BlockSpec((ts, LANES), lambda i: (i, 0))
    return pl.pallas_call(
        _rope_kernel,
        grid=(S // ts,),
        in_specs=[pl.BlockSpec((ts, 1), lambda i: (i, 0)), pl.BlockSpec((1, LANES), lambda i: (0, 0))],
        out_specs=[spec, spec, spec],
        out_shape=[jax.ShapeDtypeStruct((S, LANES), F32)] * 3,
        compiler_params=_params(("arbitrary",), 32),
        name="rope_tables",
    )(positions.reshape(S, 1), freq)


def _rms_mod(x, g, scale, shift):
    y = x * lax.rsqrt(jnp.mean(x * x, axis=-1, keepdims=True) + NORM_EPS) * g
    return y * (1 + scale) + shift


def _norm_mod_kernel(x_ref, g_ref, sh_ref, sc_ref, o_ref):
    o_ref[...] = _rms_mod(x_ref[...], g_ref[...], sc_ref[...], sh_ref[...]).astype(o_ref.dtype)


def norm_mod(x, g, mod, shift_idx, scale_idx):
    S, D = x.shape
    tm = _tile(S, 256)
    return pl.pallas_call(
        _norm_mod_kernel,
        grid=(S // tm,),
        in_specs=[pl.BlockSpec((tm, D), lambda i: (i, 0)),
                  pl.BlockSpec((1, D), lambda i: (0, 0)),
                  pl.BlockSpec((1, D), lambda i: (0, shift_idx)),
                  pl.BlockSpec((1, D), lambda i: (0, scale_idx))],
        out_specs=pl.BlockSpec((tm, D), lambda i: (i, 0)),
        out_shape=jax.ShapeDtypeStruct((S, D), BF16),
        compiler_params=_params(("arbitrary",), 32),
        name="norm_mod",
    )(x, g.reshape(1, D), mod, mod)


def _norm_mod_q_kernel(x_ref, g_ref, sh_ref, sc_ref, o_ref, q_ref, s_ref, n_ref):
    y = _rms_mod(x_ref[...], g_ref[...], sc_ref[...], sh_ref[...])
    o_ref[...] = y.astype(o_ref.dtype)
    scale = jnp.maximum(jnp.max(jnp.abs(y), axis=-1, keepdims=True), FP8_TINY) / FP8_MAX
    q_ref[...] = (y / scale).astype(q_ref.dtype)
    s_ref[...] = scale
    n_ref[...] = jnp.sqrt(jnp.sum(y * y, axis=-1, keepdims=True))


def norm_mod_q(x, g, mod, shift_idx, scale_idx):
    S, D = x.shape
    tm = _tile(S, 256)
    blk = pl.BlockSpec((tm, D), lambda i: (i, 0))
    return pl.pallas_call(
        _norm_mod_q_kernel,
        grid=(S // tm,),
        in_specs=[blk,
                  pl.BlockSpec((1, D), lambda i: (0, 0)),
                  pl.BlockSpec((1, D), lambda i: (0, shift_idx)),
                  pl.BlockSpec((1, D), lambda i: (0, scale_idx))],
        out_specs=[blk, blk, pl.BlockSpec((tm, 1), lambda i: (i, 0)), pl.BlockSpec((tm, 1), lambda i: (i, 0))],
        out_shape=[jax.ShapeDtypeStruct((S, D), BF16), jax.ShapeDtypeStruct((S, D), F8),
                   jax.ShapeDtypeStruct((S, 1), F32), jax.ShapeDtypeStruct((S, 1), F32)],
        compiler_params=_params(("arbitrary",), 32),
        name="norm_mod_q",
    )(x, g.reshape(1, D), mod, mod)


def _in_proj_kernel(h_ref, w_ref, c_ref, s1_ref, s2_ref, o_ref, *, per_sec):
    tn = o_ref.shape[1]
    acc = jnp.dot(h_ref[...], w_ref[...].astype(h_ref.dtype), preferred_element_type=F32)
    section = pl.program_id(1) // per_sec
    q_scale = jnp.where((section == 0) | (section == 2), QK_SCALE, 1.0).astype(F32)
    cs, s1, s2 = c_ref[...], s1_ref[...], s2_ref[...]
    for j in range(tn // HEAD_DIM):
        sl = slice(j * HEAD_DIM, (j + 1) * HEAD_DIM)
        t = acc[:, sl]
        r = (t * cs + pltpu.roll(t, HEAD_DIM - ROT_HALF, 1) * s1 + pltpu.roll(t, ROT_HALF, 1) * s2)
        o_ref[:, sl] = (r * q_scale).astype(o_ref.dtype)


def in_proj(h, w_in, tables):
    S, D = h.shape
    sec_width = w_in.shape[1] // 6
    tm = _tile(S, 1024)
    tn = _tile(sec_width, 512)
    per_sec = sec_width // tn
    src = lambda j: j + jnp.where(j >= 2 * per_sec, per_sec, 0)
    tspec = pl.BlockSpec((tm, LANES), lambda i, j: (i, 0))
    return pl.pallas_call(
        functools.partial(_in_proj_kernel, per_sec=per_sec),
        grid=(S // tm, 4 * per_sec),
        in_specs=[pl.BlockSpec((tm, D), lambda i, j: (i, 0)),
                  pl.BlockSpec((D, tn), lambda i, j: (0, src(j))),
                  tspec, tspec, tspec],
        out_specs=pl.BlockSpec((tm, tn), lambda i, j: (i, j)),
        out_shape=jax.ShapeDtypeStruct((S, 4 * sec_width), BF16),
        compiler_params=_params(("arbitrary", "arbitrary"), 48),
        name="in_proj",
    )(h, w_in, *tables)


def _v_proj_t_kernel(w_ref, h_ref, o_ref, wt_ref):
    @pl.when(pl.program_id(1) == 0)
    def _():
        wt_ref[...] = w_ref[...].astype(wt_ref.dtype).T

    o_ref[...] = lax.dot_general(wt_ref[...], h_ref[...], NT_DIMS, preferred_element_type=F32).astype(o_ref.dtype)


def v_proj_t(h, w_in):
    S, D = h.shape
    sec_width = w_in.shape[1] // 6
    tn = _tile(sec_width, 512)
    ts = _tile(S, 1024)
    per_sec = sec_width // tn
    src = lambda j: j + jnp.where(j >= per_sec, 4 * per_sec, 2 * per_sec)
    return pl.pallas_call(
        _v_proj_t_kernel,
        grid=(2 * per_sec, S // ts),
        in_specs=[pl.BlockSpec((D, tn), lambda j, i: (0, src(j))), pl.BlockSpec((ts, D), lambda j, i: (i, 0))],
        out_specs=pl.BlockSpec((tn, ts), lambda j, i: (j, i)),
        out_shape=jax.ShapeDtypeStruct((2 * sec_width, S), BF16),
        scratch_shapes=[pltpu.VMEM((tn, D), BF16)],
        compiler_params=_params(("arbitrary", "arbitrary"), 48),
        name="v_proj_t",
    )(w_in, h)


def _diff_attn_kernel(lam_ref, q_ref, k_ref, vt_ref, g_ref, o_ref,
                      m1, l1, a1, m2, l2, a2, *, tq, lam_init):
    i = pl.program_id(1)
    stats = ((m1, l1, a1), (m2, l2, a2))
    for m, l, a in stats:
        m[...] = jnp.full(m.shape, -jnp.inf, F32)
        l[...] = jnp.zeros(l.shape, F32)
        a[...] = jnp.zeros(a.shape, F32)

    def chunk(kv, masked):
        k0 = pl.multiple_of(kv * tq, tq)
        vt = vt_ref[:, pl.ds(k0, tq)]

        def scores(c):
            sl = slice(c * HEAD_DIM, (c + 1) * HEAD_DIM)
            return lax.dot_general(k_ref[pl.ds(k0, tq), sl], q_ref[:, sl], NT_DIMS, preferred_element_type=F32)

        s_next = scores(0)
        for c, (m, l, a) in enumerate(stats):
            s = s_next
            if c == 0:
                s_next = scores(1)
            if masked:
                keep = (lax.broadcasted_iota(jnp.int32, (tq, tq), 0) <= lax.broadcasted_iota(jnp.int32, (tq, tq), 1))
                s = jnp.where(keep, s, NEG_INF)
            m_prev = m[...]
            m_new = jnp.maximum(m_prev, jnp.max(s, axis=0, keepdims=True))
            alpha = jnp.exp2(m_prev - m_new)
            p = jnp.exp2(s - m_new)
            l[...] = alpha * l[...] + jnp.sum(p, axis=0, keepdims=True)
            a[...] = alpha * a[...] + jnp.dot(vt, p.astype(vt.dtype), preferred_element_type=F32)
            m[...] = m_new

    def body(kv, carry):
        chunk(kv, False)
        return carry

    lax.fori_loop(0, i, body, 0)
    chunk(i, True)

    lp = lam_ref[...]
    lam = (jnp.exp(jnp.sum(lp[0:1] * lp[1:2], axis=-1, keepdims=True))
           - jnp.exp(jnp.sum(lp[2:3] * lp[3:4], axis=-1, keepdims=True)) + lam_init)
    o = (a1[...] / l1[...] - lam * (a2[...] / l2[...])).T
    y = o * lax.rsqrt(jnp.mean(o * o, axis=-1, keepdims=True) + NORM_EPS) * g_ref[...]
    o_ref[...] = (y * (1 - lam_init)).astype(o_ref.dtype)


def diff_attn(proj, v_t, lam_rows, subln_g, n_heads, lam_init):
    S = proj.shape[0]
    tq = _tile(S, 512)
    hw = 2 * HEAD_DIM
    return pl.pallas_call(
        functools.partial(_diff_attn_kernel, tq=tq, lam_init=lam_init),
        grid=(n_heads, S // tq),
        in_specs=[pl.BlockSpec((4, HEAD_DIM), lambda h, i: (0, 0)),
                  pl.BlockSpec((tq, hw), lambda h, i: (i, h)),
                  pl.BlockSpec((S, hw), lambda h, i: (0, n_heads + h)),
                  pl.BlockSpec((hw, S), lambda h, i: (h, 0)),
                  pl.BlockSpec((1, hw), lambda h, i: (0, 0))],
        out_specs=pl.BlockSpec((tq, hw), lambda h, i: (i, h)),
        out_shape=jax.ShapeDtypeStruct((S, n_heads * hw), BF16),
        scratch_shapes=[pltpu.VMEM((1, tq), F32), pltpu.VMEM((1, tq), F32), pltpu.VMEM((hw, tq), F32),
                        pltpu.VMEM((1, tq), F32), pltpu.VMEM((1, tq), F32), pltpu.VMEM((hw, tq), F32)],
        compiler_params=_params(("arbitrary", "arbitrary"), 48),
        name="diff_attn",
    )(lam_rows, proj, proj, v_t, subln_g.reshape(1, hw))


def _dil_bias(delta):
    count = jnp.zeros(delta.shape, F32)
    for window, dilation in DILATED_CONFIGS:
        ok = (delta >= 0) & (delta <= window) & ((delta & (dilation - 1)) == 0)
        count = count + jnp.where(ok, 1.0, 0.0)
    return jnp.where(count > 0, jnp.log2(jnp.maximum(count, 1.0)), NEG_INF)


DIL_HEADS_PER_STEP = 2


def _dil_attn_kernel(q_ref, k_ref, vt_ref, g_ref, o_ref, bias_ref, *, tq, win):
    i = pl.program_id(1)

    @pl.when((pl.program_id(0) == 0) & (i == 0))
    def _():
        kj = lax.broadcasted_iota(jnp.int32, bias_ref.shape, 0)
        qi = lax.broadcasted_iota(jnp.int32, bias_ref.shape, 1)
        bias_ref[...] = _dil_bias(qi + DIL_BACK - kj)

    start = pl.multiple_of(jnp.maximum(i * tq - DIL_BACK, 0), tq)
    row0 = pl.multiple_of(jnp.maximum(DIL_BACK - i * tq, 0), tq)
    bias = bias_ref[pl.ds(row0, win), :]
    heads = q_ref.shape[1] // HEAD_DIM

    def scores(hd):
        sl = slice(hd * HEAD_DIM, (hd + 1) * HEAD_DIM)
        return lax.dot_general(k_ref[pl.ds(start, win), sl], q_ref[:, sl], NT_DIMS,
                               preferred_element_type=F32) + bias

    s_next = scores(0)
    for hd in range(heads):
        sl = slice(hd * HEAD_DIM, (hd + 1) * HEAD_DIM)
        s = s_next
        if hd + 1 < heads:
            s_next = scores(hd + 1)
        p = jnp.exp2(s - jnp.max(s, axis=0, keepdims=True))
        vtw = vt_ref[sl, pl.ds(start, win)]
        o_t = jnp.dot(vtw, p.astype(vtw.dtype), preferred_element_type=F32) / jnp.sum(p, axis=0, keepdims=True)
        o = o_t.T
        y = o * lax.rsqrt(jnp.mean(o * o, axis=-1, keepdims=True) + NORM_EPS) * g_ref[:, sl]
        o_ref[:, sl] = y.astype(o_ref.dtype)


def dil_attn(proj, v_t, out_g, n_heads, col0, row0):
    S = proj.shape[0]
    for window, dilation in DILATED_CONFIGS:
        assert window % dilation == 0 and dilation & (dilation - 1) == 0
    tq = _tile(S, 256)
    win = DIL_BACK + tq
    hp = DIL_HEADS_PER_STEP
    hw = hp * HEAD_DIM
    assert S >= win and DIL_BACK % tq == 0 and win % LANES == 0
    assert n_heads % hp == 0 and col0 % hw == 0 and row0 % hw == 0
    c0 = col0 // hw
    r0 = row0 // hw
    ng = n_heads // hp
    return pl.pallas_call(
        functools.partial(_dil_attn_kernel, tq=tq, win=win),
        grid=(ng, S // tq),
        in_specs=[pl.BlockSpec((tq, hw), lambda h, i: (i, c0 + h)),
                  pl.BlockSpec((S, hw), lambda h, i: (0, c0 + ng + h)),
                  pl.BlockSpec((hw, S), lambda h, i: (r0 + h, 0)),
                  pl.BlockSpec((1, hw), lambda h, i: (0, h))],
        out_specs=pl.BlockSpec((tq, hw), lambda h, i: (i, h)),
        out_shape=jax.ShapeDtypeStruct((S, n_heads * HEAD_DIM), BF16),
        scratch_shapes=[pltpu.VMEM((win + DIL_BACK, tq), F32)],
        compiler_params=_params(("arbitrary", "arbitrary"), 48),
        name="dil_attn",
    )(proj, proj, v_t, out_g.reshape(1, n_heads * HEAD_DIM))


def _out_proj_kernel(ya_ref, yb_ref, wa_ref, wb_ref, x_ref, gate_ref, x1_ref):
    acc = (jnp.dot(ya_ref[...], wa_ref[...].astype(ya_ref.dtype), preferred_element_type=F32)
           + jnp.dot(yb_ref[...], wb_ref[...].astype(yb_ref.dtype), preferred_element_type=F32))
    x1_ref[...] = x_ref[...] + gate_ref[...] * acc


def out_proj(ya, yb, w_out, x, mod, gate_idx):
    S, D = x.shape
    kh = ya.shape[1]
    assert yb.shape[1] == kh and w_out.shape[0] == 2 * kh
    tm = _tile(S, 1024)
    tn = _tile(D, 512)
    nj = D // tn
    return pl.pallas_call(
        _out_proj_kernel,
        grid=(S // tm, nj),
        in_specs=[pl.BlockSpec((tm, kh), lambda i, j: (i, 0)),
                  pl.BlockSpec((tm, kh), lambda i, j: (i, 0)),
                  pl.BlockSpec((kh, tn), lambda i, j: (0, j)),
                  pl.BlockSpec((kh, tn), lambda i, j: (1, j)),
                  pl.BlockSpec((tm, tn), lambda i, j: (i, j)),
                  pl.BlockSpec((1, tn), lambda i, j: (0, gate_idx * nj + j))],
        out_specs=pl.BlockSpec((tm, tn), lambda i, j: (i, j)),
        out_shape=jax.ShapeDtypeStruct((S, D), F32),
        compiler_params=_params(("arbitrary", "arbitrary"), 48),
        name="out_proj",
    )(ya, yb, w_out, w_out, x, mod)


def _top16(x):
    rows = lax.broadcasted_iota(jnp.int32, x.shape, 0)
    rank = jnp.full(x.shape, PEER_TOPK, jnp.int32)
    vals = []
    for r in range(PEER_TOPK):
        m = jnp.max(x, axis=0, keepdims=True)
        idx = jnp.min(jnp.where(x == m, rows, x.shape[0]), axis=0, keepdims=True)
        hit = rows == idx
        rank = jnp.where(hit, r, rank)
        x = jnp.where(hit, -jnp.inf, x)
        vals.append(m)
    return jnp.concatenate(vals, axis=0), rank


def _peer_router_kernel(h_ref, wq_ref, sk_ref, rank1_ref, cnt0_ref, e0_ref, e1_ref):
    q = jnp.dot(h_ref[...], wq_ref[...].astype(h_ref.dtype), preferred_element_type=F32)
    half = q.shape[1] // 2
    s = [lax.dot_general(sk_ref[0, p], q[:, p * half:(p + 1) * half], NT_DIMS,
                         precision=lax.Precision.HIGHEST, preferred_element_type=F32)
         for p in range(2)]
    a, rank0 = _top16(s[0])
    b, rank1 = _top16(s[1])
    n_q = [PEER_TOPK // (p + 1) for p in range(PEER_TOPK)]
    pad = -sum(n_q) % 8
    cand = jnp.concatenate([a[p:p + 1] + b[:n_q[p]] for p in range(PEER_TOPK)]
                           + [jnp.full((pad, a.shape[1]), -jnp.inf, F32)], axis=0)
    fin, sel_rank = _top16(cand)
    sel = jnp.where(sel_rank < PEER_TOPK, 1.0, 0.0)
    starts = [sum(n_q[:p]) for p in range(PEER_TOPK)]
    cnt = jnp.concatenate([jnp.sum(sel[starts[p]:starts[p] + n_q[p]], axis=0, keepdims=True)
                           for p in range(PEER_TOPK)], axis=0)
    cnt0 = jnp.zeros(s[0].shape, F32)
    for p in range(PEER_TOPK):
        cnt0 = jnp.where(rank0 == p, cnt[p:p + 1], cnt0)
    z = jnp.sum(jnp.exp(fin - fin[0:1]), axis=0, keepdims=True)
    rank1_ref[0] = rank1.astype(F32).astype(BF16)
    cnt0_ref[0] = cnt0.astype(BF16)
    e0_ref[0] = jnp.exp(s[0] - a[0:1]).astype(BF16)
    e1_ref[0] = (jnp.exp(s[1] - b[0:1]) / z).astype(BF16)


def peer_router(h, w_q, subkeys):
    S, D = h.shape
    n_heads = subkeys.shape[0]
    qd = w_q.shape[1] // n_heads
    T = _tile(S, 512)
    ospec = pl.BlockSpec((1, N_KEYS, T), lambda t, hd: (hd, 0, t))
    return pl.pallas_call(
        _peer_router_kernel,
        grid=(S // T, n_heads),
        in_specs=[pl.BlockSpec((T, D), lambda t, hd: (t, 0)),
                  pl.BlockSpec((D, qd), lambda t, hd: (0, hd)),
                  pl.BlockSpec((1, 2, N_KEYS, qd // 2), lambda t, hd: (hd, 0, 0, 0))],
        out_specs=[ospec] * 4,
        out_shape=[jax.ShapeDtypeStruct((n_heads, N_KEYS, S), BF16)] * 4,
        compiler_params=_params(("arbitrary", "arbitrary"), 48),
        name="peer_router",
    )(h, w_q, subkeys)


def _gelu(x):
    return 0.5 * x * (1.0 + lax.erf(x * (0.5 ** 0.5)))


PEER_GI = 8
PEER_GJ = 64
PEER_CHUNKS = 4


def _peer_expert_kernel(h_ref, hs_ref, wi_ref, u_ref, us_ref, v_ref, vs_ref, rank1_ref, cnt0_ref, e0_ref, e1_ref,
                        o_ref, aa_ref, ab_ref, w8a_ref, w8b_ref, *, nj):
    e = pl.program_id(1)
    ne = pl.num_programs(1) - 2
    gi, gj, D = u_ref.shape
    n_heads, _, T = rank1_ref.shape
    eb = gi * gj
    kc = D // PEER_CHUNKS
    assert T == PEER_CHUNKS * LANES
    ew = jnp.clip(e - 1, 0, ne - 1)
    i0 = pl.multiple_of((ew // nj) * gi, gi)
    j0 = pl.multiple_of((ew % nj) * gj, gj)

    @pl.when(e == 0)
    def _():
        o_ref[...] = jnp.zeros(o_ref.shape, F32)
        ab_ref[...] = jnp.zeros(ab_ref.shape, F32)
        w8a_ref[...] = jnp.zeros(w8a_ref.shape, w8a_ref.dtype)

    def step(a_next, a_cur, w8_cur, w8_prev):
        u_scale = us_ref[...].reshape(eb, LANES)
        v_scale = vs_ref[...].reshape(eb, LANES)

        def piece(c, carry):
            cols = pl.ds(pl.multiple_of(c * kc, kc), kc)
            part = lax.dot_general(u_ref[:, :, cols].reshape(eb, kc), h_ref[:, cols], NT_DIMS,
                                   preferred_element_type=F32)
            a_next[...] = jnp.where(c == 0, part, a_next[...] + part)
            y = lax.dot_general(w8_prev[...], v_ref[:, :, cols].reshape(eb, kc), TN_DIMS,
                                preferred_element_type=F32)
            o_ref[:, cols] += y
            lsl = pl.ds(pl.multiple_of(c * LANES, LANES), LANES)
            act = _gelu(a_cur[:, lsl] * (u_scale * hs_ref[:, lsl])) * (v_scale * wi_ref[:, lsl])
            c0 = [cnt0_ref[hd, pl.ds(i0, gi), lsl] for hd in range(n_heads)]
            e0 = [e0_ref[hd, pl.ds(i0, gi), lsl] for hd in range(n_heads)]
            for ii in range(gi):
                rsl = slice(ii * gj, (ii + 1) * gj)
                g = jnp.zeros((gj, LANES), BF16)
                for hd in range(n_heads):
                    r1 = rank1_ref[hd, pl.ds(j0, gj), lsl]
                    e1 = e1_ref[hd, pl.ds(j0, gj), lsl]
                    g = g + jnp.where(r1 < c0[hd][ii:ii + 1], e1 * e0[hd][ii:ii + 1], jnp.zeros((), BF16))
                w8_cur[rsl, lsl] = (g.astype(F32) * act[rsl]).astype(w8_cur.dtype)
            return carry

        lax.fori_loop(0, PEER_CHUNKS, piece, 0)

    pl.when(e % 2 == 0)(lambda: step(aa_ref, ab_ref, w8b_ref, w8a_ref))
    pl.when(e % 2 == 1)(lambda: step(ab_ref, aa_ref, w8a_ref, w8b_ref))


def _quant_rows_kernel(w_ref, q_ref, s_ref, n_ref):
    w = w_ref[...]
    scale = jnp.maximum(jnp.max(jnp.abs(w), axis=-1, keepdims=True), FP8_TINY) / FP8_MAX
    q_ref[...] = (w / scale).astype(q_ref.dtype)
    s_ref[...] = jnp.broadcast_to(scale, s_ref.shape)
    n_ref[...] = jnp.sqrt(jnp.sum(w * w, axis=-1, keepdims=True))


def quant_rows(w):
    E, D = w.shape
    te = _tile(E, 512)
    return pl.pallas_call(
        _quant_rows_kernel,
        grid=(E // te,),
        in_specs=[pl.BlockSpec((te, D), lambda i: (i, 0))],
        out_specs=[pl.BlockSpec((te, D), lambda i: (i, 0)), pl.BlockSpec((te, LANES), lambda i: (i, 0)),
                   pl.BlockSpec((te, 1), lambda i: (i, 0))],
        out_shape=[jax.ShapeDtypeStruct((E, D), F8), jax.ShapeDtypeStruct((E, LANES), F32),
                   jax.ShapeDtypeStruct((E, 1), F32)],
        compiler_params=_params(("arbitrary",), 32),
        name="quant_rows",
    )(w)


PEER_W_MARGIN = 1.25


def peer_experts(h8, h_scale, h_norm, u, v, tables):
    S, D = h8.shape
    n_heads = tables[0].shape[0]
    T = _tile(S, 512)
    u8, su, u_norm = quant_rows(u)
    v8, sv, _ = quant_rows(v)
    w_bound = (PEER_W_MARGIN * n_heads) * h_norm * jnp.max(u_norm * sv[:, :1])
    w_scale = jnp.maximum(w_bound, FP8_TINY) / FP8_MAX
    u3 = u8.reshape(N_KEYS, N_KEYS, D)
    v3 = v8.reshape(N_KEYS, N_KEYS, D)
    su3 = su.reshape(N_KEYS, N_KEYS, LANES)
    sv3 = sv.reshape(N_KEYS, N_KEYS, LANES)
    ni, nj = N_KEYS // PEER_GI, N_KEYS // PEER_GJ
    ne = ni * nj
    eb = PEER_GI * PEER_GJ

    def u_idx(t, e):
        eu = jnp.minimum(e, ne - 1)
        return eu // nj, eu % nj, 0

    def w_idx(t, e):
        ew = jnp.clip(e - 1, 0, ne - 1)
        return ew // nj, ew % nj, 0

    def v_idx(t, e):
        ev = jnp.maximum(e - 2, 0)
        return ev // nj, ev % nj, 0

    once = pl.Buffered(1)
    tspec = pl.BlockSpec((n_heads, N_KEYS, T), lambda t, e: (0, 0, t), pipeline_mode=once)
    a_buf = pltpu.VMEM((eb, T), F32)
    w8_buf = pltpu.VMEM((eb, T), F8)
    row = pl.BlockSpec((1, T), lambda t, e: (0, t))
    y = pl.pallas_call(
        functools.partial(_peer_expert_kernel, nj=nj),
        grid=(S // T, ne + 2),
        in_specs=[pl.BlockSpec((T, D), lambda t, e: (t, 0), pipeline_mode=once),
                  row, row,
                  pl.BlockSpec((PEER_GI, PEER_GJ, D), u_idx),
                  pl.BlockSpec((PEER_GI, PEER_GJ, LANES), w_idx),
                  pl.BlockSpec((PEER_GI, PEER_GJ, D), v_idx),
                  pl.BlockSpec((PEER_GI, PEER_GJ, LANES), w_idx),
                  tspec, tspec, tspec, tspec],
        out_specs=pl.BlockSpec((T, D), lambda t, e: (t, 0)),
        out_shape=jax.ShapeDtypeStruct((S, D), F32),
        scratch_shapes=[a_buf, a_buf, w8_buf, w8_buf],
        compiler_params=_params(("arbitrary", "arbitrary"), 56),
        name="peer_experts",
    )(h8, h_scale.reshape(1, S), (1.0 / w_scale).reshape(1, S), u3, su3, v3, sv3, *tables)
    return y, w_scale


def _final_kernel(x_ref, y_ref, ys_ref, gate_ref, g_ref, o_ref, *, normalize):
    x = x_ref[...] + gate_ref[...] * (y_ref[...] * ys_ref[...])
    if normalize:
        x = x * lax.rsqrt(jnp.mean(x * x, axis=-1, keepdims=True) + NORM_EPS) * g_ref[...]
    o_ref[...] = x


def final_norm(x, y, y_scale, mod, gate_idx, g, normalize):
    S, D = x.shape
    tm = _tile(S, 256)
    blk = pl.BlockSpec((tm, D), lambda i: (i, 0))
    return pl.pallas_call(
        functools.partial(_final_kernel, normalize=normalize),
        grid=(S // tm,),
        in_specs=[blk, blk, pl.BlockSpec((tm, 1), lambda i: (i, 0)),
                  pl.BlockSpec((1, D), lambda i: (0, gate_idx)), pl.BlockSpec((1, D), lambda i: (0, 0))],
        out_specs=blk,
        out_shape=jax.ShapeDtypeStruct((S, D), F32),
        compiler_params=_params(("arbitrary",), 48),
        name="final_norm",
    )(x, y, y_scale, mod, g.reshape(1, D))


def kernel(x, c, positions, norm1_g, norm2_g, w_ada, b_ada, w_in, lam_q1, lam_k1, lam_q2, lam_k2,
           diff_subln_g, dil_out_g, w_out, peer_wq, peer_subkeys, peer_u, peer_v, final_g):
    B, S, D = x.shape
    depth = w_ada.shape[0]
    assert B == 1, "one sequence per call"
    diff_width = D // 2
    diff_heads = diff_width // (2 * HEAD_DIM)
    dil_heads = (D - diff_width) // HEAD_DIM
    xs = x.reshape(S, D)
    tables = rope_tables(positions.reshape(S))
    for l in range(depth):
        lam_init = 0.8 - 0.6 * math.exp(-0.3 * l)
        mod = ada_mod(c, w_ada[l], b_ada[l])
        h = norm_mod(xs, norm1_g[l], mod, 0, 1)
        proj = in_proj(h, w_in[l], tables)
        v_t = v_proj_t(h, w_in[l])
        lam_rows = jnp.stack([lam_q1[l], lam_k1[l], lam_q2[l], lam_k2[l]])
        y_diff = diff_attn(proj, v_t, lam_rows, diff_subln_g[l], diff_heads, lam_init)
        y_dil = dil_attn(proj, v_t, dil_out_g[l], dil_heads, 2 * diff_width, diff_width)
        xs = out_proj(y_diff, y_dil, w_out[l], xs, mod, 2)
        h, h8, h_scale, h_norm = norm_mod_q(xs, norm2_g[l], mod, 3, 4)
        gates = peer_router(h, peer_wq[l], peer_subkeys[l])
        y, y_scale = peer_experts(h8, h_scale, h_norm, peer_u[l], peer_v[l], gates)
        xs = final_norm(xs, y, y_scale, mod, 5, final_g, normalize=(l == depth - 1))
    return xs.reshape(B, S, D)
```

```python
import functools
import math

import jax
import jax.numpy as jnp
from jax import lax
from jax.experimental import pallas as pl
from jax.experimental.pallas import tpu as pltpu

F32 = jnp.float32
BF16 = jnp.bfloat16
F8 = jnp.float8_e4m3fn
FP8_MAX = float(jnp.finfo(F8).max)
FP8_TINY = 1e-30

HEAD_DIM = 128
ROT_DIM = HEAD_DIM // 4
ROT_HALF = ROT_DIM // 2
ROPE_THETA = 500000.0
DILATED_CONFIGS = ((128, 1), (512, 4), (2048, 16))
DIL_BACK = max(w for w, _ in DILATED_CONFIGS)
PEER_HEADS = 8
N_KEYS = 128
PEER_TOPK = 16
NORM_EPS = 1e-6
NEG_INF = -1e30
QK_SCALE = HEAD_DIM ** -0.5 * math.log2(math.e)
DIFF_ROW_SPLIT = 2
LANES = 128
MIB = 1024 * 1024

NT_DIMS = (((1,), (1,)), ((), ()))
TN_DIMS = (((0,), (0,)), ((), ()))


def _params(semantics, vmem_mib):
    return pltpu.CompilerParams(dimension_semantics=semantics, vmem_limit_bytes=vmem_mib * MIB)


def _tile(n, pref):
    t = min(n, pref)
    assert n % t == 0, (n, pref)
    return t


def _ada_kernel(c_ref, w_ref, b_ref, o_ref):
    c = c_ref[...]
    ca = c * jax.nn.sigmoid(c)
    for j in range(o_ref.shape[1] // LANES):
        sl = slice(j * LANES, (j + 1) * LANES)
        o_ref[:, sl] = jnp.sum(w_ref[:, sl] * ca, axis=0, keepdims=True) + b_ref[:, sl]


def ada_mod(c, w_ada, b_ada):
    D, N = w_ada.shape
    tn = _tile(N, 1024)
    cb = jnp.broadcast_to(c.reshape(D, 1), (D, LANES))
    return pl.pallas_call(
        _ada_kernel,
        grid=(N // tn,),
        in_specs=[pl.BlockSpec((D, LANES), lambda j: (0, 0)),
                  pl.BlockSpec((D, tn), lambda j: (0, j)),
                  pl.BlockSpec((1, tn), lambda j: (0, j))],
        out_specs=pl.BlockSpec((1, tn), lambda j: (0, j)),
        out_shape=jax.ShapeDtypeStruct((1, N), F32),
        compiler_params=_params(("arbitrary",), 48),
        name="ada_mod",
    )(cb, w_ada, b_ada.reshape(1, N))


def _rope_kernel(pos_ref, freq_ref, c_ref, s1_ref, s2_ref):
    ang = pos_ref[...].astype(F32) * freq_ref[...]
    lane = lax.broadcasted_iota(jnp.int32, ang.shape, 1)
    cs = jnp.cos(ang)
    sn = jnp.sin(ang)
    c_ref[...] = jnp.where(lane < ROT_DIM, cs, 1.0)
    s1_ref[...] = jnp.where(lane < ROT_HALF, -sn, 0.0)
    s2_ref[...] = jnp.where((lane >= ROT_HALF) & (lane < ROT_DIM), sn, 0.0)


def rope_tables(positions):
    S = positions.shape[0]
    ts = _tile(S, 1024)
    inv_freq = jnp.power(ROPE_THETA, -jnp.arange(0, ROT_DIM, 2, dtype=F32) / ROT_DIM)
    freq = jnp.tile(inv_freq, LANES // ROT_HALF).reshape(1, LANES)
    spec = pl.BlockSpec((ts, LANES), lambda i: (i, 0))
    return pl.pallas_call(
        _rope_kernel,
        grid=(S // ts,),
        in_specs=[pl.BlockSpec((ts, 1), lambda i: (i, 0)), pl.BlockSpec((1, LANES), lambda i: (0, 0))],
        out_specs=[spec, spec, spec],
        out_shape=[jax.ShapeDtypeStruct((S, LANES), F32)] * 3,
        compiler_params=_params(("arbitrary",), 32),
        name="rope_tables",
    )(positions.reshape(S, 1), freq)


def _rms_mod(x, g, scale, shift):
    y = x * lax.rsqrt(jnp.mean(x * x, axis=-1, keepdims=True) + NORM_EPS) * g
    return y * (1 + scale) + shift


def _norm_mod_kernel(x_ref, g_ref, sh_ref, sc_ref, o_ref):
    o_ref[...] = _rms_mod(x_ref[...], g_ref[...], sc_ref[...], sh_ref[...]).astype(o_ref.dtype)


def norm_mod(x, g, mod, shift_idx, scale_idx):
    S, D = x.shape
    tm = _tile(S, 256)
    return pl.pallas_call(
        _norm_mod_kernel,
        grid=(S // tm,),
        in_specs=[pl.BlockSpec((tm, D), lambda i: (i, 0)),
                  pl.BlockSpec((1, D), lambda i: (0, 0)),
                  pl.BlockSpec((1, D), lambda i: (0, shift_idx)),
                  pl.BlockSpec((1, D), lambda i: (0, scale_idx))],
        out_specs=pl.BlockSpec((tm, D), lambda i: (i, 0)),
        out_shape=jax.ShapeDtypeStruct((S, D), BF16),
        compiler_params=_params(("arbitrary",), 32),
        name="norm_mod",
    )(x, g.reshape(1, D), mod, mod)


def _norm_mod_q_kernel(x_ref, g_ref, sh_ref, sc_ref, o_ref, q_ref, s_ref, n_ref):
    y = _rms_mod(x_ref[...], g_ref[...], sc_ref[...], sh_ref[...])
    o_ref[...] = y.astype(o_ref.dtype)
    scale = jnp.maximum(jnp.max(jnp.abs(y), axis=-1, keepdims=True), FP8_TINY) / FP8_MAX
    q_ref[...] = (y / scale).astype(q_ref.dtype)
    s_ref[...] = scale
    n_ref[...] = jnp.sqrt(jnp.sum(y * y, axis=-1, keepdims=True))


def norm_mod_q(x, g, mod, shift_idx, scale_idx):
    S, D = x.shape
    tm = _tile(S, 256)
    blk = pl.BlockSpec((tm, D), lambda i: (i, 0))
    return pl.pallas_call(
        _norm_mod_q_kernel,
        grid=(S // tm,),
        in_specs=[blk,
                  pl.BlockSpec((1, D), lambda i: (0, 0)),
                  pl.BlockSpec((1, D), lambda i: (0, shift_idx)),
                  pl.BlockSpec((1, D), lambda i: (0, scale_idx))],
        out_specs=[blk, blk, pl.BlockSpec((tm, 1), lambda i: (i, 0)), pl.BlockSpec((tm, 1), lambda i: (i, 0))],
        out_shape=[jax.ShapeDtypeStruct((S, D), BF16), jax.ShapeDtypeStruct((S, D), F8),
                   jax.ShapeDtypeStruct((S, 1), F32), jax.ShapeDtypeStruct((S, 1), F32)],
        compiler_params=_params(("arbitrary",), 32),
        name="norm_mod_q",
    )(x, g.reshape(1, D), mod, mod)


def _in_proj_kernel(h_ref, w_ref, c_ref, s1_ref, s2_ref, o_ref, *, per_sec):
    tn = o_ref.shape[1]
    acc = jnp.dot(h_ref[...], w_ref[...].astype(h_ref.dtype), preferred_element_type=F32)
    section = pl.program_id(1) // per_sec
    q_scale = jnp.where((section == 0) | (section == 2), QK_SCALE, 1.0).astype(F32)
    cs, s1, s2 = c_ref[...], s1_ref[...], s2_ref[...]
    for j in range(tn // HEAD_DIM):
        sl = slice(j * HEAD_DIM, (j + 1) * HEAD_DIM)
        t = acc[:, sl]
        r = (t * cs + pltpu.roll(t, HEAD_DIM - ROT_HALF, 1) * s1 + pltpu.roll(t, ROT_HALF, 1) * s2)
        o_ref[:, sl] = (r * q_scale).astype(o_ref.dtype)


def in_proj(h, w_in, tables):
    S, D = h.shape
    sec_width = w_in.shape[1] // 6
    tm = _tile(S, 1024)
    tn = _tile(sec_width, 512)
    per_sec = sec_width // tn
    src = lambda j: j + jnp.where(j >= 2 * per_sec, per_sec, 0)
    tspec = pl.BlockSpec((tm, LANES), lambda i, j: (i, 0))
    return pl.pallas_call(
        functools.partial(_in_proj_kernel, per_sec=per_sec),
        grid=(S // tm, 4 * per_sec),
        in_specs=[pl.BlockSpec((tm, D), lambda i, j: (i, 0)),
                  pl.BlockSpec((D, tn), lambda i, j: (0, src(j))),
                  tspec, tspec, tspec],
        out_specs=pl.BlockSpec((tm, tn), lambda i, j: (i, j)),
        out_shape=jax.ShapeDtypeStruct((S, 4 * sec_width), BF16),
        compiler_params=_params(("arbitrary", "arbitrary"), 48),
        name="in_proj",
    )(h, w_in, *tables)


def _v_proj_t_kernel(w_ref, h_ref, o_ref, wt_ref):
    @pl.when(pl.program_id(1) == 0)
    def _():
        wt_ref[...] = w_ref[...].astype(wt_ref.dtype).T

    o_ref[...] = lax.dot_general(wt_ref[...], h_ref[...], NT_DIMS, preferred_element_type=F32).astype(o_ref.dtype)


def v_proj_t(h, w_in):
    S, D = h.shape
    sec_width = w_in.shape[1] // 6
    tn = _tile(sec_width, 512)
    ts = _tile(S, 1024)
    per_sec = sec_width // tn
    src = lambda j: j + jnp.where(j >= per_sec, 4 * per_sec, 2 * per_sec)
    return pl.pallas_call(
        _v_proj_t_kernel,
        grid=(2 * per_sec, S // ts),
        in_specs=[pl.BlockSpec((D, tn), lambda j, i: (0, src(j))), pl.BlockSpec((ts, D), lambda j, i: (i, 0))],
        out_specs=pl.BlockSpec((tn, ts), lambda j, i: (j, i)),
        out_shape=jax.ShapeDtypeStruct((2 * sec_width, S), BF16),
        scratch_shapes=[pltpu.VMEM((tn, D), BF16)],
        compiler_params=_params(("arbitrary", "arbitrary"), 48),
        name="v_proj_t",
    )(w_in, h)


def _diff_attn_kernel(lam_ref, q_ref, k_ref, vt_ref, g_ref, o_ref,
                      m1, l1, a1, m2, l2, a2, sa_ref, sb_ref, *, tq, lam_init):
    i = pl.program_id(1)
    stats = ((m1, l1, a1), (m2, l2, a2))
    for m, l, a in stats:
        m[...] = jnp.full(m.shape, -jnp.inf, F32)
        l[...] = jnp.zeros(l.shape, F32)
        a[...] = jnp.zeros(a.shape, F32)

    def scores(kv, s_ref):
        k0 = pl.multiple_of(kv * tq, tq)
        for c in range(2):
            sl = slice(c * HEAD_DIM, (c + 1) * HEAD_DIM)
            s_ref[c] = lax.dot_general(k_ref[pl.ds(k0, tq), sl], q_ref[:, sl], NT_DIMS,
                                       preferred_element_type=F32)

    def softmax_pv(kv, s_ref, masked):
        k0 = pl.multiple_of(kv * tq, tq)
        vt = vt_ref[:, pl.ds(k0, tq)]
        for c, (m, l, a) in enumerate(stats):
            s = s_ref[c]
            if masked:
                keep = (lax.broadcasted_iota(jnp.int32, (tq, tq), 0) <= lax.broadcasted_iota(jnp.int32, (tq, tq), 1))
                s = jnp.where(keep, s, NEG_INF)
            m_prev = m[...]
            m_new = jnp.maximum(m_prev, jnp.max(s, axis=0, keepdims=True))
            alpha = jnp.exp2(m_prev - m_new)
            p = jnp.exp2(s - m_new)
            l[...] = alpha * l[...] + jnp.sum(p, axis=0, keepdims=True)
            a[...] = alpha * a[...] + jnp.dot(vt, p.astype(vt.dtype), preferred_element_type=F32)
            m[...] = m_new

    scores(0, sa_ref)

    def pair(n, carry):
        scores(2 * n + 1, sb_ref)
        softmax_pv(2 * n, sa_ref, False)
        scores(2 * n + 2, sa_ref)
        softmax_pv(2 * n + 1, sb_ref, False)
        return carry

    lax.fori_loop(0, i // 2, pair, 0)

    @pl.when(i % 2 == 0)
    def _():
        softmax_pv(i, sa_ref, True)

    @pl.when(i % 2 == 1)
    def _():
        scores(i, sb_ref)
        softmax_pv(i - 1, sa_ref, False)
        softmax_pv(i, sb_ref, True)

    lp = lam_ref[...]
    lam = (jnp.exp(jnp.sum(lp[0:1] * lp[1:2], axis=-1, keepdims=True))
           - jnp.exp(jnp.sum(lp[2:3] * lp[3:4], axis=-1, keepdims=True)) + lam_init)
    o = (a1[...] / l1[...] - lam * (a2[...] / l2[...])).T
    y = o * lax.rsqrt(jnp.mean(o * o, axis=-1, keepdims=True) + NORM_EPS) * g_ref[...]
    o_ref[...] = (y * (1 - lam_init)).astype(o_ref.dtype)


def diff_attn(proj, v_t, lam_rows, subln_g, n_heads, lam_init):
    S = proj.shape[0]
    tq = _tile(S, 512)
    hw = 2 * HEAD_DIM
    return pl.pallas_call(
        functools.partial(_diff_attn_kernel, tq=tq, lam_init=lam_init),
        grid=(n_heads, S // tq),
        in_specs=[pl.BlockSpec((4, HEAD_DIM), lambda h, i: (0, 0)),
                  pl.BlockSpec((tq, hw), lambda h, i: (i, h)),
                  pl.BlockSpec((S, hw), lambda h, i: (0, n_heads + h)),
                  pl.BlockSpec((hw, S), lambda h, i: (h, 0)),
                  pl.BlockSpec((1, hw), lambda h, i: (0, 0))],
        out_specs=pl.BlockSpec((tq, hw), lambda h, i: (i, h)),
        out_shape=jax.ShapeDtypeStruct((S, n_heads * hw), BF16),
        scratch_shapes=[pltpu.VMEM((1, tq), F32), pltpu.VMEM((1, tq), F32), pltpu.VMEM((hw, tq), F32),
                        pltpu.VMEM((1, tq), F32), pltpu.VMEM((1, tq), F32), pltpu.VMEM((hw, tq), F32),
                        pltpu.VMEM((2, tq, tq), F32), pltpu.VMEM((2, tq, tq), F32)],
        compiler_params=_params(("arbitrary", "arbitrary"), 48),
        name="diff_attn",
    )(lam_rows, proj, proj, v_t, subln_g.reshape(1, hw))


def _dil_bias(delta):
    count = jnp.zeros(delta.shape, F32)
    for window, dilation in DILATED_CONFIGS:
        ok = (delta >= 0) & (delta <= window) & ((delta & (dilation - 1)) == 0)
        count = count + jnp.where(ok, 1.0, 0.0)
    return jnp.where(count > 0, jnp.log2(jnp.maximum(count, 1.0)), NEG_INF)


DIL_HEADS_PER_STEP = 2


def _dil_attn_kernel(q_ref, k_ref, vt_ref, g_ref, o_ref, bias_ref, *, tq, win):
    i = pl.program_id(1)

    @pl.when((pl.program_id(0) == 0) & (i == 0))
    def _():
        kj = lax.broadcasted_iota(jnp.int32, bias_ref.shape, 0)
        qi = lax.broadcasted_iota(jnp.int32, bias_ref.shape, 1)
        bias_ref[...] = _dil_bias(qi + DIL_BACK - kj)

    start = pl.multiple_of(jnp.maximum(i * tq - DIL_BACK, 0), tq)
    row0 = pl.multiple_of(jnp.maximum(DIL_BACK - i * tq, 0), tq)
    bias = bias_ref[pl.ds(row0, win), :]
    heads = q_ref.shape[1] // HEAD_DIM

    def scores(hd):
        sl = slice(hd * HEAD_DIM, (hd + 1) * HEAD_DIM)
        return lax.dot_general(k_ref[pl.ds(start, win), sl], q_ref[:, sl], NT_DIMS,
                               preferred_element_type=F32) + bias

    s_next = scores(0)
    for hd in range(heads):
        sl = slice(hd * HEAD_DIM, (hd + 1) * HEAD_DIM)
        s = s_next
        if hd + 1 < heads:
            s_next = scores(hd + 1)
        p = jnp.exp2(s - jnp.max(s, axis=0, keepdims=True))
        vtw = vt_ref[sl, pl.ds(start, win)]
        o_t = jnp.dot(vtw, p.astype(vtw.dtype), preferred_element_type=F32) / jnp.sum(p, axis=0, keepdims=True)
        o = o_t.T
        y = o * lax.rsqrt(jnp.mean(o * o, axis=-1, keepdims=True) + NORM_EPS) * g_ref[:, sl]
        o_ref[:, sl] = y.astype(o_ref.dtype)


def dil_attn(proj, v_t, out_g, n_heads, col0, row0):
    S = proj.shape[0]
    for window, dilation in DILATED_CONFIGS:
        assert window % dilation == 0 and dilation & (dilation - 1) == 0
    tq = _tile(S, 256)
    win = DIL_BACK + tq
    hp = DIL_HEADS_PER_STEP
    hw = hp * HEAD_DIM
    assert S >= win and DIL_BACK % tq == 0 and win % LANES == 0
    assert n_heads % hp == 0 and col0 % hw == 0 and row0 % hw == 0
    c0 = col0 // hw
    r0 = row0 // hw
    ng = n_heads // hp
    return pl.pallas_call(
        functools.partial(_dil_attn_kernel, tq=tq, win=win),
        grid=(ng, S // tq),
        in_specs=[pl.BlockSpec((tq, hw), lambda h, i: (i, c0 + h)),
                  pl.BlockSpec((S, hw), lambda h, i: (0, c0 + ng + h)),
                  pl.BlockSpec((hw, S), lambda h, i: (r0 + h, 0)),
                  pl.BlockSpec((1, hw), lambda h, i: (0, h))],
        out_specs=pl.BlockSpec((tq, hw), lambda h, i: (i, h)),
        out_shape=jax.ShapeDtypeStruct((S, n_heads * HEAD_DIM), BF16),
        scratch_shapes=[pltpu.VMEM((win + DIL_BACK, tq), F32)],
        compiler_params=_params(("arbitrary", "arbitrary"), 48),
        name="dil_attn",
    )(proj, proj, v_t, out_g.reshape(1, n_heads * HEAD_DIM))


def _out_proj_kernel(ya_ref, yb_ref, wa_ref, wb_ref, x_ref, gate_ref, x1_ref):
    acc = (jnp.dot(ya_ref[...], wa_ref[...].astype(ya_ref.dtype), preferred_element_type=F32)
           + jnp.dot(yb_ref[...], wb_ref[...].astype(yb_ref.dtype), preferred_element_type=F32))
    x1_ref[...] = x_ref[...] + gate_ref[...] * acc


def out_proj(ya, yb, w_out, x, mod, gate_idx):
    S, D = x.shape
    kh = ya.shape[1]
    assert yb.shape[1] == kh and w_out.shape[0] == 2 * kh
    tm = _tile(S, 1024)
    tn = _tile(D, 512)
    nj = D // tn
    return pl.pallas_call(
        _out_proj_kernel,
        grid=(S // tm, nj),
        in_specs=[pl.BlockSpec((tm, kh), lambda i, j: (i, 0)),
                  pl.BlockSpec((tm, kh), lambda i, j: (i, 0)),
                  pl.BlockSpec((kh, tn), lambda i, j: (0, j)),
                  pl.BlockSpec((kh, tn), lambda i, j: (1, j)),
                  pl.BlockSpec((tm, tn), lambda i, j: (i, j)),
                  pl.BlockSpec((1, tn), lambda i, j: (0, gate_idx * nj + j))],
        out_specs=pl.BlockSpec((tm, tn), lambda i, j: (i, j)),
        out_shape=jax.ShapeDtypeStruct((S, D), F32),
        compiler_params=_params(("arbitrary", "arbitrary"), 48),
        name="out_proj",
    )(ya, yb, w_out, w_out, x, mod)


def _top16(x):
    rows = lax.broadcasted_iota(jnp.int32, x.shape, 0)
    rank = jnp.full(x.shape, PEER_TOPK, jnp.int32)
    vals = []
    for r in range(PEER_TOPK):
        m = jnp.max(x, axis=0, keepdims=True)
        idx = jnp.min(jnp.where(x == m, rows, x.shape[0]), axis=0, keepdims=True)
        hit = rows == idx
        rank = jnp.where(hit, r, rank)
        x = jnp.where(hit, -jnp.inf, x)
        vals.append(m)
    return jnp.concatenate(vals, axis=0), rank


def _peer_router_kernel(h_ref, wq_ref, sk_ref, rank1_ref, cnt0_ref, e0_ref, e1_ref):
    q = jnp.dot(h_ref[...], wq_ref[...].astype(h_ref.dtype), preferred_element_type=F32)
    half = q.shape[1] // 2
    s = [lax.dot_general(sk_ref[0, p], q[:, p * half:(p + 1) * half], NT_DIMS,
                         precision=lax.Precision.HIGHEST, preferred_element_type=F32)
         for p in range(2)]
    a, rank0 = _top16(s[0])
    b, rank1 = _top16(s[1])
    n_q = [PEER_TOPK // (p + 1) for p in range(PEER_TOPK)]
    pad = -sum(n_q) % 8
    cand = jnp.concatenate([a[p:p + 1] + b[:n_q[p]] for p in range(PEER_TOPK)]
                           + [jnp.full((pad, a.shape[1]), -jnp.inf, F32)], axis=0)
    fin, sel_rank = _top16(cand)
    sel = jnp.where(sel_rank < PEER_TOPK, 1.0, 0.0)
    starts = [sum(n_q[:p]) for p in range(PEER_TOPK)]
    cnt = jnp.concatenate([jnp.sum(sel[starts[p]:starts[p] + n_q[p]], axis=0, keepdims=True)
                           for p in range(PEER_TOPK)], axis=0)
    cnt0 = jnp.zeros(s[0].shape, F32)
    for p in range(PEER_TOPK):
        cnt0 = jnp.where(rank0 == p, cnt[p:p + 1], cnt0)
    z = jnp.sum(jnp.exp(fin - fin[0:1]), axis=0, keepdims=True)
    rank1_ref[0] = rank1.astype(F32).astype(BF16)
    cnt0_ref[0] = cnt0.astype(BF16)
    e0_ref[0] = jnp.exp(s[0] - a[0:1]).astype(BF16)
    e1_ref[0] = (jnp.exp(s[1] - b[0:1]) / z).astype(BF16)


def peer_router(h, w_q, subkeys):
    S, D = h.shape
    n_heads = subkeys.shape[0]
    qd = w_q.shape[1] // n_heads
    T = _tile(S, 512)
    ospec = pl.BlockSpec((1, N_KEYS, T), lambda t, hd: (hd, 0, t))
    return pl.pallas_call(
        _peer_router_kernel,
        grid=(S // T, n_heads),
        in_specs=[pl.BlockSpec((T, D), lambda t, hd: (t, 0)),
                  pl.BlockSpec((D, qd), lambda t, hd: (0, hd)),
                  pl.BlockSpec((1, 2, N_KEYS, qd // 2), lambda t, hd: (hd, 0, 0, 0))],
        out_specs=[ospec] * 4,
        out_shape=[jax.ShapeDtypeStruct((n_heads, N_KEYS, S), BF16)] * 4,
        compiler_params=_params(("arbitrary", "arbitrary"), 48),
        name="peer_router",
    )(h, w_q, subkeys)


def _gelu(x):
    return 0.5 * x * (1.0 + lax.erf(x * (0.5 ** 0.5)))


PEER_GI = 8
PEER_GJ = 64
PEER_CHUNKS = 4


def _peer_expert_kernel(h_ref, hs_ref, wi_ref, u_ref, us_ref, v_ref, vs_ref, rank1_ref, cnt0_ref, e0_ref, e1_ref,
                        o_ref, aa_ref, ab_ref, w8a_ref, w8b_ref, *, nj):
    e = pl.program_id(1)
    ne = pl.num_programs(1) - 2
    gi, gj, D = u_ref.shape
    n_heads, _, T = rank1_ref.shape
    eb = gi * gj
    kc = D // PEER_CHUNKS
    assert T == PEER_CHUNKS * LANES
    ew = jnp.clip(e - 1, 0, ne - 1)
    i0 = pl.multiple_of((ew // nj) * gi, gi)
    j0 = pl.multiple_of((ew % nj) * gj, gj)

    @pl.when(e == 0)
    def _():
        o_ref[...] = jnp.zeros(o_ref.shape, F32)
        ab_ref[...] = jnp.zeros(ab_ref.shape, F32)
        w8a_ref[...] = jnp.zeros(w8a_ref.shape, w8a_ref.dtype)

    def step(a_next, a_cur, w8_cur, w8_prev):
        u_scale = us_ref[...].reshape(eb, LANES)
        v_scale = vs_ref[...].reshape(eb, LANES)

        def piece(c, carry):
            cols = pl.ds(pl.multiple_of(c * kc, kc), kc)
            part = lax.dot_general(u_ref[:, :, cols].reshape(eb, kc), h_ref[:, cols], NT_DIMS,
                                   preferred_element_type=F32)
            a_next[...] = jnp.where(c == 0, part, a_next[...] + part)
            y = lax.dot_general(w8_prev[...], v_ref[:, :, cols].reshape(eb, kc), TN_DIMS,
                                preferred_element_type=F32)
            o_ref[:, cols] += y
            lsl = pl.ds(pl.multiple_of(c * LANES, LANES), LANES)
            act = _gelu(a_cur[:, lsl] * (u_scale * hs_ref[:, lsl])) * (v_scale * wi_ref[:, lsl])
            c0 = [cnt0_ref[hd, pl.ds(i0, gi), lsl] for hd in range(n_heads)]
            e0 = [e0_ref[hd, pl.ds(i0, gi), lsl] for hd in range(n_heads)]
            for ii in range(gi):
                rsl = slice(ii * gj, (ii + 1) * gj)
                g = jnp.zeros((gj, LANES), BF16)
                for hd in range(n_heads):
                    r1 = rank1_ref[hd, pl.ds(j0, gj), lsl]
                    e1 = e1_ref[hd, pl.ds(j0, gj), lsl]
                    g = g + jnp.where(r1 < c0[hd][ii:ii + 1], e1 * e0[hd][ii:ii + 1], jnp.zeros((), BF16))
                w8_cur[rsl, lsl] = (g.astype(F32) * act[rsl]).astype(w8_cur.dtype)
            return carry

        lax.fori_loop(0, PEER_CHUNKS, piece, 0)

    pl.when(e % 2 == 0)(lambda: step(aa_ref, ab_ref, w8b_ref, w8a_ref))
    pl.when(e % 2 == 1)(lambda: step(ab_ref, aa_ref, w8a_ref, w8b_ref))


def _quant_rows_kernel(w_ref, q_ref, s_ref, n_ref):
    w = w_ref[...]
    scale = jnp.maximum(jnp.max(jnp.abs(w), axis=-1, keepdims=True), FP8_TINY) / FP8_MAX
    q_ref[...] = (w / scale).astype(q_ref.dtype)
    s_ref[...] = jnp.broadcast_to(scale, s_ref.shape)
    n_ref[...] = jnp.sqrt(jnp.sum(w * w, axis=-1, keepdims=True))


def quant_rows(w):
    E, D = w.shape
    te = _tile(E, 512)
    return pl.pallas_call(
        _quant_rows_kernel,
        grid=(E // te,),
        in_specs=[pl.BlockSpec((te, D), lambda i: (i, 0))],
        out_specs=[pl.BlockSpec((te, D), lambda i: (i, 0)), pl.BlockSpec((te, LANES), lambda i: (i, 0)),
                   pl.BlockSpec((te, 1), lambda i: (i, 0))],
        out_shape=[jax.ShapeDtypeStruct((E, D), F8), jax.ShapeDtypeStruct((E, LANES), F32),
                   jax.ShapeDtypeStruct((E, 1), F32)],
        compiler_params=_params(("arbitrary",), 32),
        name="quant_rows",
    )(w)


PEER_W_MARGIN = 1.25


def peer_experts(h8, h_scale, h_norm, u, v, tables):
    S, D = h8.shape
    n_heads = tables[0].shape[0]
    T = _tile(S, 512)
    u8, su, u_norm = quant_rows(u)
    v8, sv, _ = quant_rows(v)
    w_bound = (PEER_W_MARGIN * n_heads) * h_norm * jnp.max(u_norm * sv[:, :1])
    w_scale = jnp.maximum(w_bound, FP8_TINY) / FP8_MAX
    u3 = u8.reshape(N_KEYS, N_KEYS, D)
    v3 = v8.reshape(N_KEYS, N_KEYS, D)
    su3 = su.reshape(N_KEYS, N_KEYS, LANES)
    sv3 = sv.reshape(N_KEYS, N_KEYS, LANES)
    ni, nj = N_KEYS // PEER_GI, N_KEYS // PEER_GJ
    ne = ni * nj
    eb = PEER_GI * PEER_GJ

    def u_idx(t, e):
        eu = jnp.minimum(e, ne - 1)
        return eu // nj, eu % nj, 0

    def w_idx(t, e):
        ew = jnp.clip(e - 1, 0, ne - 1)
        return ew // nj, ew % nj, 0

    def v_idx(t, e):
        ev = jnp.maximum(e - 2, 0)
        return ev // nj, ev % nj, 0

    once = pl.Buffered(1)
    tspec = pl.BlockSpec((n_heads, N_KEYS, T), lambda t, e: (0, 0, t), pipeline_mode=once)
    a_buf = pltpu.VMEM((eb, T), F32)
    w8_buf = pltpu.VMEM((eb, T), F8)
    row = pl.BlockSpec((1, T), lambda t, e: (0, t))
    y = pl.pallas_call(
        functools.partial(_peer_expert_kernel, nj=nj),
        grid=(S // T, ne + 2),
        in_specs=[pl.BlockSpec((T, D), lambda t, e: (t, 0), pipeline_mode=once),
                  row, row,
                  pl.BlockSpec((PEER_GI, PEER_GJ, D), u_idx),
                  pl.BlockSpec((PEER_GI, PEER_GJ, LANES), w_idx),
                  pl.BlockSpec((PEER_GI, PEER_GJ, D), v_idx),
                  pl.BlockSpec((PEER_GI, PEER_GJ, LANES), w_idx),
                  tspec, tspec, tspec, tspec],
        out_specs=pl.BlockSpec((T, D), lambda t, e: (t, 0)),
        out_shape=jax.ShapeDtypeStruct((S, D), F32),
        scratch_shapes=[a_buf, a_buf, w8_buf, w8_buf],
        compiler_params=_params(("arbitrary", "arbitrary"), 56),
        name="peer_experts",
    )(h8, h_scale.reshape(1, S), (1.0 / w_scale).reshape(1, S), u3, su3, v3, sv3, *tables)
    return y, w_scale


def _final_kernel(x_ref, y_ref, ys_ref, gate_ref, g_ref, o_ref, *, normalize):
    x = x_ref[...] + gate_ref[...] * (y_ref[...] * ys_ref[...])
    if normalize:
        x = x * lax.rsqrt(jnp.mean(x * x, axis=-1, keepdims=True) + NORM_EPS) * g_ref[...]
    o_ref[...] = x


def final_norm(x, y, y_scale, mod, gate_idx, g, normalize):
    S, D = x.shape
    tm = _tile(S, 256)
    blk = pl.BlockSpec((tm, D), lambda i: (i, 0))
    return pl.pallas_call(
        functools.partial(_final_kernel, normalize=normalize),
        grid=(S // tm,),
        in_specs=[blk, blk, pl.BlockSpec((tm, 1), lambda i: (i, 0)),
                  pl.BlockSpec((1, D), lambda i: (0, gate_idx)), pl.BlockSpec((1, D), lambda i: (0, 0))],
        out_specs=blk,
        out_shape=jax.ShapeDtypeStruct((S, D), F32),
        compiler_params=_params(("arbitrary",), 48),
        name="final_norm",
    )(x, y, y_scale, mod, g.reshape(1, D))


def kernel(x, c, positions, norm1_g, norm2_g, w_ada, b_ada, w_in, lam_q1, lam_k1, lam_q2, lam_k2,
           diff_subln_g, dil_out_g, w_out, peer_wq, peer_subkeys, peer_u, peer_v, final_g):
    B, S, D = x.shape
    depth = w_ada.shape[0]
    assert B == 1, "one sequence per call"
    diff_width = D // 2
    diff_heads = diff_width // (2 * HEAD_DIM)
    dil_heads = (D - diff_width) // HEAD_DIM
    xs = x.reshape(S, D)
    tables = rope_tables(positions.reshape(S))
    for l in range(depth):
        lam_init = 0.8 - 0.6 * math.exp(-0.3 * l)
        mod = ada_mod(c, w_ada[l], b_ada[l])
        h = norm_mod(xs, norm1_g[l], mod, 0, 1)
        proj = in_proj(h, w_in[l], tables)
        v_t = v_proj_t(h, w_in[l])
        lam_rows = jnp.stack([lam_q1[l], lam_k1[l], lam_q2[l], lam_k2[l]])
        y_diff = diff_attn(proj, v_t, lam_rows, diff_subln_g[l], diff_heads, lam_init)
        y_dil = dil_attn(proj, v_t, dil_out_g[l], dil_heads, 2 * diff_width, diff_width)
        xs = out_proj(y_diff, y_dil, w_out[l], xs, mod, 2)
        h, h8, h_scale, h_norm = norm_mod_q(xs, norm2_g[l], mod, 3, 4)
        gates = peer_router(h, peer_wq[l], peer_subkeys[l])
        y, y_scale = peer_experts(h8, h_scale, h_norm, peer_u[l], peer_v[l], gates)
        xs = final_norm(xs, y, y_scale, mod, 5, final_g, normalize=(l == depth - 1))
    return xs.reshape(B, S, D)
```

```python
import functools
import math

import jax
import jax.numpy as jnp
from jax import lax
from jax.experimental import pallas as pl
from jax.experimental.pallas import tpu as pltpu

F32 = jnp.float32
BF16 = jnp.bfloat16
F8 = jnp.float8_e4m3fn
FP8_MAX = float(jnp.finfo(F8).max)
FP8_TINY = 1e-30

HEAD_DIM = 128
ROT_DIM = HEAD_DIM // 4
ROT_HALF = ROT_DIM // 2
ROPE_THETA = 500000.0
DILATED_CONFIGS = ((128, 1), (512, 4), (2048, 16))
DIL_BACK = max(w for w, _ in DILATED_CONFIGS)
PEER_HEADS = 8
N_KEYS = 128
PEER_TOPK = 16
NORM_EPS = 1e-6
NEG_INF = -1e30
QK_SCALE = HEAD_DIM ** -0.5 * math.log2(math.e)
DIFF_ROW_SPLIT = 2
LANES = 128
MIB = 1024 * 1024

NT_DIMS = (((1,), (1,)), ((), ()))
TN_DIMS = (((0,), (0,)), ((), ()))


def _params(semantics, vmem_mib):
    return pltpu.CompilerParams(dimension_semantics=semantics, vmem_limit_bytes=vmem_mib * MIB)


def _tile(n, pref):
    t = min(n, pref)
    assert n % t == 0, (n, pref)
    return t


def _ada_kernel(c_ref, w_ref, b_ref, o_ref):
    c = c_ref[...]
    ca = c * jax.nn.sigmoid(c)
    for j in range(o_ref.shape[1] // LANES):
        sl = slice(j * LANES, (j + 1) * LANES)
        o_ref[:, sl] = jnp.sum(w_ref[:, sl] * ca, axis=0, keepdims=True) + b_ref[:, sl]


def ada_mod(c, w_ada, b_ada):
    D, N = w_ada.shape
    tn = _tile(N, 1024)
    cb = jnp.broadcast_to(c.reshape(D, 1), (D, LANES))
    return pl.pallas_call(
        _ada_kernel,
        grid=(N // tn,),
        in_specs=[pl.BlockSpec((D, LANES), lambda j: (0, 0)),
                  pl.BlockSpec((D, tn), lambda j: (0, j)),
                  pl.BlockSpec((1, tn), lambda j: (0, j))],
        out_specs=pl.BlockSpec((1, tn), lambda j: (0, j)),
        out_shape=jax.ShapeDtypeStruct((1, N), F32),
        compiler_params=_params(("arbitrary",), 48),
        name="ada_mod",
    )(cb, w_ada, b_ada.reshape(1, N))


def _rope_kernel(pos_ref, freq_ref, c_ref, s1_ref, s2_ref):
    ang = pos_ref[...].astype(F32) * freq_ref[...]
    lane = lax.broadcasted_iota(jnp.int32, ang.shape, 1)
    cs = jnp.cos(ang)
    sn = jnp.sin(ang)
    c_ref[...] = jnp.where(lane < ROT_DIM, cs, 1.0)
    s1_ref[...] = jnp.where(lane < ROT_HALF, -sn, 0.0)
    s2_ref[...] = jnp.where((lane >= ROT_HALF) & (lane < ROT_DIM), sn, 0.0)


def rope_tables(positions):
    S = positions.shape[0]
    ts = _tile(S, 1024)
    inv_freq = jnp.power(ROPE_THETA, -jnp.arange(0, ROT_DIM, 2, dtype=F32) / ROT_DIM)
    freq = jnp.tile(inv_freq, LANES // ROT_HALF).reshape(1, LANES)
    spec = pl.BlockSpec((ts, LANES), lambda i: (i, 0))
    return pl.pallas_call(
        _rope_kernel,
        grid=(S // ts,),
        in_specs=[pl.BlockSpec((ts, 1), lambda i: (i, 0)), pl.BlockSpec((1, LANES), lambda i: (0, 0))],
        out_specs=[spec, spec, spec],
        out_shape=[jax.ShapeDtypeStruct((S, LANES), F32)] * 3,
        compiler_params=_params(("arbitrary",), 32),
        name="rope_tables",
    )(positions.reshape(S, 1), freq)


def _rms_mod(x, g, scale, shift):
    y = x * lax.rsqrt(jnp.mean(x * x, axis=-1, keepdims=True) + NORM_EPS) * g
    return y * (1 + scale) + shift


def _norm_mod_kernel(x_ref, g_ref, sh_ref, sc_ref, o_ref):
    o_ref[...] = _rms_mod(x_ref[...], g_ref[...], sc_ref[...], sh_ref[...]).astype(o_ref.dtype)


def norm_mod(x, g, mod, shift_idx, scale_idx):
    S, D = x.shape
    tm = _tile(S, 256)
    return pl.pallas_call(
        _norm_mod_kernel,
        grid=(S // tm,),
        in_specs=[pl.BlockSpec((tm, D), lambda i: (i, 0)),
                  pl.BlockSpec((1, D), lambda i: (0, 0)),
                  pl.BlockSpec((1, D), lambda i: (0, shift_idx)),
                  pl.BlockSpec((1, D), lambda i: (0, scale_idx))],
        out_specs=pl.BlockSpec((tm, D), lambda i: (i, 0)),
        out_shape=jax.ShapeDtypeStruct((S, D), BF16),
        compiler_params=_params(("arbitrary",), 32),
        name="norm_mod",
    )(x, g.reshape(1, D), mod, mod)


def _norm_mod_q_kernel(x_ref, g_ref, sh_ref, sc_ref, o_ref, q_ref, s_ref, n_ref):
    y = _rms_mod(x_ref[...], g_ref[...], sc_ref[...], sh_ref[...])
    o_ref[...] = y.astype(o_ref.dtype)
    scale = jnp.maximum(jnp.max(jnp.abs(y), axis=-1, keepdims=True), FP8_TINY) / FP8_MAX
    q_ref[...] = (y / scale).astype(q_ref.dtype)
    s_ref[...] = scale
    n_ref[...] = jnp.sqrt(jnp.sum(y * y, axis=-1, keepdims=True))


def norm_mod_q(x, g, mod, shift_idx, scale_idx):
    S, D = x.shape
    tm = _tile(S, 256)
    blk = pl.BlockSpec((tm, D), lambda i: (i, 0))
    return pl.pallas_call(
        _norm_mod_q_kernel,
        grid=(S // tm,),
        in_specs=[blk,
                  pl.BlockSpec((1, D), lambda i: (0, 0)),
                  pl.BlockSpec((1, D), lambda i: (0, shift_idx)),
                  pl.BlockSpec((1, D), lambda i: (0, scale_idx))],
        out_specs=[blk, blk, pl.BlockSpec((tm, 1), lambda i: (i, 0)), pl.BlockSpec((tm, 1), lambda i: (i, 0))],
        out_shape=[jax.ShapeDtypeStruct((S, D), BF16), jax.ShapeDtypeStruct((S, D), F8),
                   jax.ShapeDtypeStruct((S, 1), F32), jax.ShapeDtypeStruct((S, 1), F32)],
        compiler_params=_params(("arbitrary",), 32),
        name="norm_mod_q",
    )(x, g.reshape(1, D), mod, mod)


def _in_proj_kernel(h_ref, w_ref, c_ref, s1_ref, s2_ref, o_ref, *, per_sec):
    tn = o_ref.shape[1]
    acc = jnp.dot(h_ref[...], w_ref[...].astype(h_ref.dtype), preferred_element_type=F32)
    section = pl.program_id(1) // per_sec
    q_scale = jnp.where((section == 0) | (section == 2), QK_SCALE, 1.0).astype(F32)
    cs, s1, s2 = c_ref[...], s1_ref[...], s2_ref[...]
    for j in range(tn // HEAD_DIM):
        sl = slice(j * HEAD_DIM, (j + 1) * HEAD_DIM)
        t = acc[:, sl]
        r = (t * cs + pltpu.roll(t, HEAD_DIM - ROT_HALF, 1) * s1 + pltpu.roll(t, ROT_HALF, 1) * s2)
        o_ref[:, sl] = (r * q_scale).astype(o_ref.dtype)


def in_proj(h, w_in, tables):
    S, D = h.shape
    sec_width = w_in.shape[1] // 6
    tm = _tile(S, 1024)
    tn = _tile(sec_width, 512)
    per_sec = sec_width // tn
    src = lambda j: j + jnp.where(j >= 2 * per_sec, per_sec, 0)
    tspec = pl.BlockSpec((tm, LANES), lambda i, j: (i, 0))
    return pl.pallas_call(
        functools.partial(_in_proj_kernel, per_sec=per_sec),
        grid=(S // tm, 4 * per_sec),
        in_specs=[pl.BlockSpec((tm, D), lambda i, j: (i, 0)),
                  pl.BlockSpec((D, tn), lambda i, j: (0, src(j))),
                  tspec, tspec, tspec],
        out_specs=pl.BlockSpec((tm, tn), lambda i, j: (i, j)),
        out_shape=jax.ShapeDtypeStruct((S, 4 * sec_width), BF16),
        compiler_params=_params(("arbitrary", "arbitrary"), 48),
        name="in_proj",
    )(h, w_in, *tables)


def _v_proj_t_kernel(w_ref, h_ref, o_ref, wt_ref):
    @pl.when(pl.program_id(1) == 0)
    def _():
        wt_ref[...] = w_ref[...].astype(wt_ref.dtype).T

    o_ref[...] = lax.dot_general(wt_ref[...], h_ref[...], NT_DIMS, preferred_element_type=F32).astype(o_ref.dtype)


def v_proj_t(h, w_in):
    S, D = h.shape
    sec_width = w_in.shape[1] // 6
    tn = _tile(sec_width, 512)
    ts = _tile(S, 1024)
    per_sec = sec_width // tn
    src = lambda j: j + jnp.where(j >= per_sec, 4 * per_sec, 2 * per_sec)
    return pl.pallas_call(
        _v_proj_t_kernel,
        grid=(2 * per_sec, S // ts),
        in_specs=[pl.BlockSpec((D, tn), lambda j, i: (0, src(j))), pl.BlockSpec((ts, D), lambda j, i: (i, 0))],
        out_specs=pl.BlockSpec((tn, ts), lambda j, i: (j, i)),
        out_shape=jax.ShapeDtypeStruct((2 * sec_width, S), BF16),
        scratch_shapes=[pltpu.VMEM((tn, D), BF16)],
        compiler_params=_params(("arbitrary", "arbitrary"), 48),
        name="v_proj_t",
    )(w_in, h)


def _diff_attn_kernel(lam_ref, q_ref, k_ref, vt_ref, g_ref, o_ref,
                      m1, l1, a1, m2, l2, a2, sa_ref, sb_ref, *, tq, lam_init):
    i = pl.program_id(1)
    stats = ((m1, l1, a1), (m2, l2, a2))
    for m, l, a in stats:
        m[...] = jnp.full(m.shape, -jnp.inf, F32)
        l[...] = jnp.zeros(l.shape, F32)
        a[...] = jnp.zeros(a.shape, F32)

    def scores(kv, s_ref):
        k0 = pl.multiple_of(kv * tq, tq)
        for c in range(2):
            sl = slice(c * HEAD_DIM, (c + 1) * HEAD_DIM)
            s_ref[c] = lax.dot_general(k_ref[pl.ds(k0, tq), sl], q_ref[:, sl], NT_DIMS,
                                       preferred_element_type=F32)

    def softmax_pv(kv, s_ref, masked):
        k0 = pl.multiple_of(kv * tq, tq)
        vt = vt_ref[:, pl.ds(k0, tq)]
        for c, (m, l, a) in enumerate(stats):
            s = s_ref[c]
            if masked:
                keep = (lax.broadcasted_iota(jnp.int32, (tq, tq), 0) <= lax.broadcasted_iota(jnp.int32, (tq, tq), 1))
                s = jnp.where(keep, s, NEG_INF)
            m_prev = m[...]
            m_new = jnp.maximum(m_prev, jnp.max(s, axis=0, keepdims=True))
            alpha = jnp.exp2(m_prev - m_new)
            p = jnp.exp2(s - m_new)
            l[...] = alpha * l[...] + jnp.sum(p, axis=0, keepdims=True)
            a[...] = alpha * a[...] + jnp.dot(vt, p.astype(vt.dtype), preferred_element_type=F32)
            m[...] = m_new

    scores(0, sa_ref)

    def pair(n, carry):
        scores(2 * n + 1, sb_ref)
        softmax_pv(2 * n, sa_ref, False)
        scores(2 * n + 2, sa_ref)
        softmax_pv(2 * n + 1, sb_ref, False)
        return carry

    lax.fori_loop(0, i // 2, pair, 0)

    @pl.when(i % 2 == 0)
    def _():
        softmax_pv(i, sa_ref, True)

    @pl.when(i % 2 == 1)
    def _():
        scores(i, sb_ref)
        softmax_pv(i - 1, sa_ref, False)
        softmax_pv(i, sb_ref, True)

    lp = lam_ref[...]
    lam = (jnp.exp(jnp.sum(lp[0:1] * lp[1:2], axis=-1, keepdims=True))
           - jnp.exp(jnp.sum(lp[2:3] * lp[3:4], axis=-1, keepdims=True)) + lam_init)
    o = (a1[...] / l1[...] - lam * (a2[...] / l2[...])).T
    y = o * lax.rsqrt(jnp.mean(o * o, axis=-1, keepdims=True) + NORM_EPS) * g_ref[...]
    o_ref[...] = (y * (1 - lam_init)).astype(o_ref.dtype)


def diff_attn(proj, v_t, lam_rows, subln_g, n_heads, lam_init):
    S = proj.shape[0]
    tq = _tile(S, 512)
    hw = 2 * HEAD_DIM
    return pl.pallas_call(
        functools.partial(_diff_attn_kernel, tq=tq, lam_init=lam_init),
        grid=(n_heads, S // tq),
        in_specs=[pl.BlockSpec((4, HEAD_DIM), lambda h, i: (0, 0)),
                  pl.BlockSpec((tq, hw), lambda h, i: (i, h)),
                  pl.BlockSpec((S, hw), lambda h, i: (0, n_heads + h)),
                  pl.BlockSpec((hw, S), lambda h, i: (h, 0)),
                  pl.BlockSpec((1, hw), lambda h, i: (0, 0))],
        out_specs=pl.BlockSpec((tq, hw), lambda h, i: (i, h)),
        out_shape=jax.ShapeDtypeStruct((S, n_heads * hw), BF16),
        scratch_shapes=[pltpu.VMEM((1, tq), F32), pltpu.VMEM((1, tq), F32), pltpu.VMEM((hw, tq), F32),
                        pltpu.VMEM((1, tq), F32), pltpu.VMEM((1, tq), F32), pltpu.VMEM((hw, tq), F32),
                        pltpu.VMEM((2, tq, tq), F32), pltpu.VMEM((2, tq, tq), F32)],
        compiler_params=_params(("arbitrary", "arbitrary"), 48),
        name="diff_attn",
    )(lam_rows, proj, proj, v_t, subln_g.reshape(1, hw))


def _dil_bias(delta):
    count = jnp.zeros(delta.shape, F32)
    for window, dilation in DILATED_CONFIGS:
        ok = (delta >= 0) & (delta <= window) & ((delta & (dilation - 1)) == 0)
        count = count + jnp.where(ok, 1.0, 0.0)
    return jnp.where(count > 0, jnp.log2(jnp.maximum(count, 1.0)), NEG_INF)


DIL_HEADS_PER_STEP = 2


def _dil_attn_kernel(q_ref, k_ref, vt_ref, g_ref, o_ref, bias_ref, sa_ref, sb_ref, *, tq, win):
    i = pl.program_id(1)
    n_blocks = pl.num_programs(1)
    heads = o_ref.shape[1] // HEAD_DIM

    @pl.when((pl.program_id(0) == 0) & (i == 0))
    def _():
        kj = lax.broadcasted_iota(jnp.int32, bias_ref.shape, 0)
        qi = lax.broadcasted_iota(jnp.int32, bias_ref.shape, 1)
        bias_ref[...] = _dil_bias(qi + DIL_BACK - kj)

    def window(j):
        start = pl.multiple_of(jnp.maximum(j * tq - DIL_BACK, 0), tq)
        row0 = pl.multiple_of(jnp.maximum(DIL_BACK - j * tq, 0), tq)
        return start, row0

    def scores(j, s_ref):
        start, row0 = window(j)
        bias = bias_ref[pl.ds(row0, win), :]
        q0 = pl.multiple_of(j * tq, tq)
        for hd in range(heads):
            sl = slice(hd * HEAD_DIM, (hd + 1) * HEAD_DIM)
            s_ref[hd] = lax.dot_general(k_ref[pl.ds(start, win), sl], q_ref[pl.ds(q0, tq), sl], NT_DIMS,
                                        preferred_element_type=F32) + bias

    def softmax_pv(j, s_ref):
        start, _ = window(j)
        for hd in range(heads):
            sl = slice(hd * HEAD_DIM, (hd + 1) * HEAD_DIM)
            s = s_ref[hd]
            p = jnp.exp2(s - jnp.max(s, axis=0, keepdims=True))
            vtw = vt_ref[sl, pl.ds(start, win)]
            o_t = jnp.dot(vtw, p.astype(vtw.dtype), preferred_element_type=F32) / jnp.sum(p, axis=0, keepdims=True)
            o = o_t.T
            y = o * lax.rsqrt(jnp.mean(o * o, axis=-1, keepdims=True) + NORM_EPS) * g_ref[:, sl]
            o_ref[:, sl] = y.astype(o_ref.dtype)

    @pl.when(i == 0)
    def _():
        scores(0, sa_ref)

    nxt = jnp.minimum(i + 1, n_blocks - 1)

    @pl.when(i % 2 == 0)
    def _():
        scores(nxt, sb_ref)
        softmax_pv(i, sa_ref)

    @pl.when(i % 2 == 1)
    def _():
        scores(nxt, sa_ref)
        softmax_pv(i, sb_ref)


def dil_attn(proj, v_t, out_g, n_heads, col0, row0):
    S = proj.shape[0]
    for window, dilation in DILATED_CONFIGS:
        assert window % dilation == 0 and dilation & (dilation - 1) == 0
    tq = _tile(S, 256)
    win = DIL_BACK + tq
    hp = DIL_HEADS_PER_STEP
    hw = hp * HEAD_DIM
    assert S >= win and DIL_BACK % tq == 0 and win % LANES == 0
    assert n_heads % hp == 0 and col0 % hw == 0 and row0 % hw == 0
    c0 = col0 // hw
    r0 = row0 // hw
    ng = n_heads // hp
    return pl.pallas_call(
        functools.partial(_dil_attn_kernel, tq=tq, win=win),
        grid=(ng, S // tq),
        in_specs=[pl.BlockSpec((S, hw), lambda h, i: (0, c0 + h)),
                  pl.BlockSpec((S, hw), lambda h, i: (0, c0 + ng + h)),
                  pl.BlockSpec((hw, S), lambda h, i: (r0 + h, 0)),
                  pl.BlockSpec((1, hw), lambda h, i: (0, h))],
        out_specs=pl.BlockSpec((tq, hw), lambda h, i: (i, h)),
        out_shape=jax.ShapeDtypeStruct((S, n_heads * HEAD_DIM), BF16),
        scratch_shapes=[pltpu.VMEM((win + DIL_BACK, tq), F32),
                        pltpu.VMEM((hp, win, tq), F32), pltpu.VMEM((hp, win, tq), F32)],
        compiler_params=_params(("arbitrary", "arbitrary"), 56),
        name="dil_attn",
    )(proj, proj, v_t, out_g.reshape(1, n_heads * HEAD_DIM))


def _out_proj_kernel(ya_ref, yb_ref, wa_ref, wb_ref, x_ref, gate_ref, x1_ref):
    acc = (jnp.dot(ya_ref[...], wa_ref[...].astype(ya_ref.dtype), preferred_element_type=F32)
           + jnp.dot(yb_ref[...], wb_ref[...].astype(yb_ref.dtype), preferred_element_type=F32))
    x1_ref[...] = x_ref[...] + gate_ref[...] * acc


def out_proj(ya, yb, w_out, x, mod, gate_idx):
    S, D = x.shape
    kh = ya.shape[1]
    assert yb.shape[1] == kh and w_out.shape[0] == 2 * kh
    tm = _tile(S, 1024)
    tn = _tile(D, 512)
    nj = D // tn
    return pl.pallas_call(
        _out_proj_kernel,
        grid=(S // tm, nj),
        in_specs=[pl.BlockSpec((tm, kh), lambda i, j: (i, 0)),
                  pl.BlockSpec((tm, kh), lambda i, j: (i, 0)),
                  pl.BlockSpec((kh, tn), lambda i, j: (0, j)),
                  pl.BlockSpec((kh, tn), lambda i, j: (1, j)),
                  pl.BlockSpec((tm, tn), lambda i, j: (i, j)),
                  pl.BlockSpec((1, tn), lambda i, j: (0, gate_idx * nj + j))],
        out_specs=pl.BlockSpec((tm, tn), lambda i, j: (i, j)),
        out_shape=jax.ShapeDtypeStruct((S, D), F32),
        compiler_params=_params(("arbitrary", "arbitrary"), 48),
        name="out_proj",
    )(ya, yb, w_out, w_out, x, mod)


def _top16(x):
    rows = lax.broadcasted_iota(jnp.int32, x.shape, 0)
    rank = jnp.full(x.shape, PEER_TOPK, jnp.int32)
    vals = []
    for r in range(PEER_TOPK):
        m = jnp.max(x, axis=0, keepdims=True)
        idx = jnp.min(jnp.where(x == m, rows, x.shape[0]), axis=0, keepdims=True)
        hit = rows == idx
        rank = jnp.where(hit, r, rank)
        x = jnp.where(hit, -jnp.inf, x)
        vals.append(m)
    return jnp.concatenate(vals, axis=0), rank


def _peer_router_kernel(h_ref, wq_ref, sk_ref, rank1_ref, cnt0_ref, e0_ref, e1_ref):
    q = jnp.dot(h_ref[...], wq_ref[...].astype(h_ref.dtype), preferred_element_type=F32)
    half = q.shape[1] // 2
    s = [lax.dot_general(sk_ref[0, p], q[:, p * half:(p + 1) * half], NT_DIMS,
                         precision=lax.Precision.HIGHEST, preferred_element_type=F32)
         for p in range(2)]
    a, rank0 = _top16(s[0])
    b, rank1 = _top16(s[1])
    n_q = [PEER_TOPK // (p + 1) for p in range(PEER_TOPK)]
    pad = -sum(n_q) % 8
    cand = jnp.concatenate([a[p:p + 1] + b[:n_q[p]] for p in range(PEER_TOPK)]
                           + [jnp.full((pad, a.shape[1]), -jnp.inf, F32)], axis=0)
    fin, sel_rank = _top16(cand)
    sel = jnp.where(sel_rank < PEER_TOPK, 1.0, 0.0)
    starts = [sum(n_q[:p]) for p in range(PEER_TOPK)]
    cnt = jnp.concatenate([jnp.sum(sel[starts[p]:starts[p] + n_q[p]], axis=0, keepdims=True)
                           for p in range(PEER_TOPK)], axis=0)
    cnt0 = jnp.zeros(s[0].shape, F32)
    for p in range(PEER_TOPK):
        cnt0 = jnp.where(rank0 == p, cnt[p:p + 1], cnt0)
    z = jnp.sum(jnp.exp(fin - fin[0:1]), axis=0, keepdims=True)
    rank1_ref[0] = rank1.astype(F32).astype(BF16)
    cnt0_ref[0] = cnt0.astype(BF16)
    e0_ref[0] = jnp.exp(s[0] - a[0:1]).astype(BF16)
    e1_ref[0] = (jnp.exp(s[1] - b[0:1]) / z).astype(BF16)


def peer_router(h, w_q, subkeys):
    S, D = h.shape
    n_heads = subkeys.shape[0]
    qd = w_q.shape[1] // n_heads
    T = _tile(S, 512)
    ospec = pl.BlockSpec((1, N_KEYS, T), lambda t, hd: (hd, 0, t))
    return pl.pallas_call(
        _peer_router_kernel,
        grid=(S // T, n_heads),
        in_specs=[pl.BlockSpec((T, D), lambda t, hd: (t, 0)),
                  pl.BlockSpec((D, qd), lambda t, hd: (0, hd)),
                  pl.BlockSpec((1, 2, N_KEYS, qd // 2), lambda t, hd: (hd, 0, 0, 0))],
        out_specs=[ospec] * 4,
        out_shape=[jax.ShapeDtypeStruct((n_heads, N_KEYS, S), BF16)] * 4,
        compiler_params=_params(("arbitrary", "arbitrary"), 48),
        name="peer_router",
    )(h, w_q, subkeys)


def _gelu(x):
    return 0.5 * x * (1.0 + lax.erf(x * (0.5 ** 0.5)))


PEER_GI = 8
PEER_GJ = 64
PEER_CHUNKS = 4


def _peer_expert_kernel(h_ref, hs_ref, wi_ref, u_ref, us_ref, v_ref, vs_ref, rank1_ref, cnt0_ref, e0_ref, e1_ref,
                        x_ref, ws_ref, gate_ref, fg_ref, o_ref, aa_ref, ab_ref, w8a_ref, w8b_ref, *, nj, normalize):
    e = pl.program_id(1)
    ne = pl.num_programs(1) - 2
    gi, gj, D = u_ref.shape
    n_heads, _, T = rank1_ref.shape
    eb = gi * gj
    kc = D // PEER_CHUNKS
    assert T == PEER_CHUNKS * LANES
    ew = jnp.clip(e - 1, 0, ne - 1)
    i0 = pl.multiple_of((ew // nj) * gi, gi)
    j0 = pl.multiple_of((ew % nj) * gj, gj)

    @pl.when(e == 0)
    def _():
        o_ref[...] = jnp.zeros(o_ref.shape, F32)
        ab_ref[...] = jnp.zeros(ab_ref.shape, F32)
        w8a_ref[...] = jnp.zeros(w8a_ref.shape, w8a_ref.dtype)

    def step(a_next, a_cur, w8_cur, w8_prev):
        u_scale = us_ref[...].reshape(eb, LANES)
        v_scale = vs_ref[...].reshape(eb, LANES)

        def piece(c, carry):
            cols = pl.ds(pl.multiple_of(c * kc, kc), kc)
            part = lax.dot_general(u_ref[:, :, cols].reshape(eb, kc), h_ref[:, cols], NT_DIMS,
                                   preferred_element_type=F32)
            a_next[...] = jnp.where(c == 0, part, a_next[...] + part)
            y = lax.dot_general(w8_prev[...], v_ref[:, :, cols].reshape(eb, kc), TN_DIMS,
                                preferred_element_type=F32)
            o_ref[:, cols] += y
            lsl = pl.ds(pl.multiple_of(c * LANES, LANES), LANES)
            act = _gelu(a_cur[:, lsl] * (u_scale * hs_ref[:, lsl])) * (v_scale * wi_ref[:, lsl])
            c0 = [cnt0_ref[hd, pl.ds(i0, gi), lsl] for hd in range(n_heads)]
            e0 = [e0_ref[hd, pl.ds(i0, gi), lsl] for hd in range(n_heads)]
            for ii in range(gi):
                rsl = slice(ii * gj, (ii + 1) * gj)
                g = jnp.zeros((gj, LANES), BF16)
                for hd in range(n_heads):
                    r1 = rank1_ref[hd, pl.ds(j0, gj), lsl]
                    e1 = e1_ref[hd, pl.ds(j0, gj), lsl]
                    g = g + jnp.where(r1 < c0[hd][ii:ii + 1], e1 * e0[hd][ii:ii + 1], jnp.zeros((), BF16))
                w8_cur[rsl, lsl] = (g.astype(F32) * act[rsl]).astype(w8_cur.dtype)
            return carry

        lax.fori_loop(0, PEER_CHUNKS, piece, 0)

    pl.when(e % 2 == 0)(lambda: step(aa_ref, ab_ref, w8b_ref, w8a_ref))
    pl.when(e % 2 == 1)(lambda: step(ab_ref, aa_ref, w8a_ref, w8b_ref))

    @pl.when(e == pl.num_programs(1) - 1)
    def _():
        x = x_ref[...] + gate_ref[...] * (o_ref[...] * ws_ref[...])
        if normalize:
            x = x * lax.rsqrt(jnp.mean(x * x, axis=-1, keepdims=True) + NORM_EPS) * fg_ref[...]
        o_ref[...] = x


def _quant_rows_kernel(w_ref, q_ref, s_ref, n_ref):
    w = w_ref[...]
    scale = jnp.maximum(jnp.max(jnp.abs(w), axis=-1, keepdims=True), FP8_TINY) / FP8_MAX
    q_ref[...] = (w / scale).astype(q_ref.dtype)
    s_ref[...] = jnp.broadcast_to(scale, s_ref.shape)
    n_ref[...] = jnp.sqrt(jnp.sum(w * w, axis=-1, keepdims=True))


def quant_rows(w):
    E, D = w.shape
    te = _tile(E, 512)
    return pl.pallas_call(
        _quant_rows_kernel,
        grid=(E // te,),
        in_specs=[pl.BlockSpec((te, D), lambda i: (i, 0))],
        out_specs=[pl.BlockSpec((te, D), lambda i: (i, 0)), pl.BlockSpec((te, LANES), lambda i: (i, 0)),
                   pl.BlockSpec((te, 1), lambda i: (i, 0))],
        out_shape=[jax.ShapeDtypeStruct((E, D), F8), jax.ShapeDtypeStruct((E, LANES), F32),
                   jax.ShapeDtypeStruct((E, 1), F32)],
        compiler_params=_params(("arbitrary",), 32),
        name="quant_rows",
    )(w)


PEER_W_MARGIN = 1.25


def peer_experts(h8, h_scale, h_norm, u, v, tables, x, mod, gate_idx, final_g, normalize):
    S, D = h8.shape
    n_heads = tables[0].shape[0]
    T = _tile(S, 512)
    u8, su, u_norm = quant_rows(u)
    v8, sv, _ = quant_rows(v)
    w_bound = (PEER_W_MARGIN * n_heads) * h_norm * jnp.max(u_norm * sv[:, :1])
    w_scale = jnp.maximum(w_bound, FP8_TINY) / FP8_MAX
    u3 = u8.reshape(N_KEYS, N_KEYS, D)
    v3 = v8.reshape(N_KEYS, N_KEYS, D)
    su3 = su.reshape(N_KEYS, N_KEYS, LANES)
    sv3 = sv.reshape(N_KEYS, N_KEYS, LANES)
    ni, nj = N_KEYS // PEER_GI, N_KEYS // PEER_GJ
    ne = ni * nj
    eb = PEER_GI * PEER_GJ

    def u_idx(t, e):
        eu = jnp.minimum(e, ne - 1)
        return eu // nj, eu % nj, 0

    def w_idx(t, e):
        ew = jnp.clip(e - 1, 0, ne - 1)
        return ew // nj, ew % nj, 0

    def v_idx(t, e):
        ev = jnp.maximum(e - 2, 0)
        return ev // nj, ev % nj, 0

    once = pl.Buffered(1)
    tspec = pl.BlockSpec((n_heads, N_KEYS, T), lambda t, e: (0, 0, t), pipeline_mode=once)
    a_buf = pltpu.VMEM((eb, T), F32)
    w8_buf = pltpu.VMEM((eb, T), F8)
    row = pl.BlockSpec((1, T), lambda t, e: (0, t))
    return pl.pallas_call(
        functools.partial(_peer_expert_kernel, nj=nj, normalize=normalize),
        grid=(S // T, ne + 2),
        in_specs=[pl.BlockSpec((T, D), lambda t, e: (t, 0), pipeline_mode=once),
                  row, row,
                  pl.BlockSpec((PEER_GI, PEER_GJ, D), u_idx),
                  pl.BlockSpec((PEER_GI, PEER_GJ, LANES), w_idx),
                  pl.BlockSpec((PEER_GI, PEER_GJ, D), v_idx),
                  pl.BlockSpec((PEER_GI, PEER_GJ, LANES), w_idx),
                  tspec, tspec, tspec, tspec,
                  pl.BlockSpec((T, D), lambda t, e: (t, 0), pipeline_mode=once),
                  pl.BlockSpec((T, 1), lambda t, e: (t, 0), pipeline_mode=once),
                  pl.BlockSpec((1, D), lambda t, e: (0, gate_idx)),
                  pl.BlockSpec((1, D), lambda t, e: (0, 0))],
        out_specs=pl.BlockSpec((T, D), lambda t, e: (t, 0)),
        out_shape=jax.ShapeDtypeStruct((S, D), F32),
        scratch_shapes=[a_buf, a_buf, w8_buf, w8_buf],
        compiler_params=_params(("arbitrary", "arbitrary"), 56),
        name="peer_experts",
    )(h8, h_scale.reshape(1, S), (1.0 / w_scale).reshape(1, S), u3, su3, v3, sv3, *tables,
      x, w_scale, mod, final_g.reshape(1, D))


def kernel(x, c, positions, norm1_g, norm2_g, w_ada, b_ada, w_in, lam_q1, lam_k1, lam_q2, lam_k2,
           diff_subln_g, dil_out_g, w_out, peer_wq, peer_subkeys, peer_u, peer_v, final_g):
    B, S, D = x.shape
    depth = w_ada.shape[0]
    assert B == 1, "one sequence per call"
    diff_width = D // 2
    diff_heads = diff_width // (2 * HEAD_DIM)
    dil_heads = (D - diff_width) // HEAD_DIM
    xs = x.reshape(S, D)
    tables = rope_tables(positions.reshape(S))
    for l in range(depth):
        lam_init = 0.8 - 0.6 * math.exp(-0.3 * l)
        mod = ada_mod(c, w_ada[l], b_ada[l])
        h = norm_mod(xs, norm1_g[l], mod, 0, 1)
        proj = in_proj(h, w_in[l], tables)
        v_t = v_proj_t(h, w_in[l])
        lam_rows = jnp.stack([lam_q1[l], lam_k1[l], lam_q2[l], lam_k2[l]])
        y_diff = diff_attn(proj, v_t, lam_rows, diff_subln_g[l], diff_heads, lam_init)
        y_dil = dil_attn(proj, v_t, dil_out_g[l], dil_heads, 2 * diff_width, diff_width)
        xs = out_proj(y_diff, y_dil, w_out[l], xs, mod, 2)
        h, h8, h_scale, h_norm = norm_mod_q(xs, norm2_g[l], mod, 3, 4)
        gates = peer_router(h, peer_wq[l], peer_subkeys[l])
        xs = peer_experts(h8, h_scale, h_norm, peer_u[l], peer_v[l], gates, xs, mod, 5, final_g,
                          normalize=(l == depth - 1))
    return xs.reshape(B, S, D)
```

```python
import functools
import math

import jax
import jax.numpy as jnp
from jax import lax
from jax.experimental import pallas as pl
from jax.experimental.pallas import tpu as pltpu

F32 = jnp.float32
BF16 = jnp.bfloat16
F8 = jnp.float8_e4m3fn
FP8_MAX = float(jnp.finfo(F8).max)
FP8_TINY = 1e-30

HEAD_DIM = 128
ROT_DIM = HEAD_DIM // 4
ROT_HALF = ROT_DIM // 2
ROPE_THETA = 500000.0
DILATED_CONFIGS = ((128, 1), (512, 4), (2048, 16))
DIL_BACK = max(w for w, _ in DILATED_CONFIGS)
PEER_HEADS = 8
N_KEYS = 128
PEER_TOPK = 16
NORM_EPS = 1e-6
NEG_INF = -1e30
QK_SCALE = HEAD_DIM ** -0.5 * math.log2(math.e)
DIFF_ROW_SPLIT = 2
LANES = 128
MIB = 1024 * 1024

NT_DIMS = (((1,), (1,)), ((), ()))
TN_DIMS = (((0,), (0,)), ((), ()))


def _params(semantics, vmem_mib):
    return pltpu.CompilerParams(dimension_semantics=semantics, vmem_limit_bytes=vmem_mib * MIB)


def _tile(n, pref):
    t = min(n, pref)
    assert n % t == 0, (n, pref)
    return t


def _ada_kernel(c_ref, w_ref, b_ref, o_ref):
    c = c_ref[...]
    ca = c * jax.nn.sigmoid(c)
    for j in range(o_ref.shape[1] // LANES):
        sl = slice(j * LANES, (j + 1) * LANES)
        o_ref[:, sl] = jnp.sum(w_ref[:, sl] * ca, axis=0, keepdims=True) + b_ref[:, sl]


def ada_mod(c, w_ada, b_ada):
    D, N = w_ada.shape
    tn = _tile(N, 1024)
    cb = jnp.broadcast_to(c.reshape(D, 1), (D, LANES))
    return pl.pallas_call(
        _ada_kernel,
        grid=(N // tn,),
        in_specs=[pl.BlockSpec((D, LANES), lambda j: (0, 0)),
                  pl.BlockSpec((D, tn), lambda j: (0, j)),
                  pl.BlockSpec((1, tn), lambda j: (0, j))],
        out_specs=pl.BlockSpec((1, tn), lambda j: (0, j)),
        out_shape=jax.ShapeDtypeStruct((1, N), F32),
        compiler_params=_params(("arbitrary",), 48),
        name="ada_mod",
    )(cb, w_ada, b_ada.reshape(1, N))


def _rope_kernel(pos_ref, freq_ref, c_ref, s1_ref, s2_ref):
    ang = pos_ref[...].astype(F32) * freq_ref[...]
    lane = lax.broadcasted_iota(jnp.int32, ang.shape, 1)
    cs = jnp.cos(ang)
    sn = jnp.sin(ang)
    c_ref[...] = jnp.where(lane < ROT_DIM, cs, 1.0)
    s1_ref[...] = jnp.where(lane < ROT_HALF, -sn, 0.0)
    s2_ref[...] = jnp.where((lane >= ROT_HALF) & (lane < ROT_DIM), sn, 0.0)


def rope_tables(positions):
    S = positions.shape[0]
    ts = _tile(S, 1024)
    inv_freq = jnp.power(ROPE_THETA, -jnp.arange(0, ROT_DIM, 2, dtype=F32) / ROT_DIM)
    freq = jnp.tile(inv_freq, LANES // ROT_HALF).reshape(1, LANES)
    spec = pl.BlockSpec((ts, LANES), lambda i: (i, 0))
    return pl.pallas_call(
        _rope_kernel,
        grid=(S // ts,),
        in_specs=[pl.BlockSpec((ts, 1), lambda i: (i, 0)), pl.BlockSpec((1, LANES), lambda i: (0, 0))],
        out_specs=[spec, spec, spec],
        out_shape=[jax.ShapeDtypeStruct((S, LANES), F32)] * 3,
        compiler_params=_params(("arbitrary",), 32),
        name="rope_tables",
    )(positions.reshape(S, 1), freq)


def _rms_mod(x, g, scale, shift):
    y = x * lax.rsqrt(jnp.mean(x * x, axis=-1, keepdims=True) + NORM_EPS) * g
    return y * (1 + scale) + shift


def _norm_mod_kernel(x_ref, g_ref, sh_ref, sc_ref, o_ref):
    o_ref[...] = _rms_mod(x_ref[...], g_ref[...], sc_ref[...], sh_ref[...]).astype(o_ref.dtype)


def norm_mod(x, g, mod, shift_idx, scale_idx):
    S, D = x.shape
    tm = _tile(S, 256)
    return pl.pallas_call(
        _norm_mod_kernel,
        grid=(S // tm,),
        in_specs=[pl.BlockSpec((tm, D), lambda i: (i, 0)),
                  pl.BlockSpec((1, D), lambda i: (0, 0)),
                  pl.BlockSpec((1, D), lambda i: (0, shift_idx)),
                  pl.BlockSpec((1, D), lambda i: (0, scale_idx))],
        out_specs=pl.BlockSpec((tm, D), lambda i: (i, 0)),
        out_shape=jax.ShapeDtypeStruct((S, D), BF16),
        compiler_params=_params(("arbitrary",), 32),
        name="norm_mod",
    )(x, g.reshape(1, D), mod, mod)


def _norm_mod_q_kernel(x_ref, g_ref, sh_ref, sc_ref, o_ref, q_ref, s_ref, n_ref):
    y = _rms_mod(x_ref[...], g_ref[...], sc_ref[...], sh_ref[...])
    o_ref[...] = y.astype(o_ref.dtype)
    scale = jnp.maximum(jnp.max(jnp.abs(y), axis=-1, keepdims=True), FP8_TINY) / FP8_MAX
    q_ref[...] = (y / scale).astype(q_ref.dtype)
    s_ref[...] = scale
    n_ref[...] = jnp.sqrt(jnp.sum(y * y, axis=-1, keepdims=True))


def norm_mod_q(x, g, mod, shift_idx, scale_idx):
    S, D = x.shape
    tm = _tile(S, 256)
    blk = pl.BlockSpec((tm, D), lambda i: (i, 0))
    return pl.pallas_call(
        _norm_mod_q_kernel,
        grid=(S // tm,),
        in_specs=[blk,
                  pl.BlockSpec((1, D), lambda i: (0, 0)),
                  pl.BlockSpec((1, D), lambda i: (0, shift_idx)),
                  pl.BlockSpec((1, D), lambda i: (0, scale_idx))],
        out_specs=[blk, blk, pl.BlockSpec((tm, 1), lambda i: (i, 0)), pl.BlockSpec((tm, 1), lambda i: (i, 0))],
        out_shape=[jax.ShapeDtypeStruct((S, D), BF16), jax.ShapeDtypeStruct((S, D), F8),
                   jax.ShapeDtypeStruct((S, 1), F32), jax.ShapeDtypeStruct((S, 1), F32)],
        compiler_params=_params(("arbitrary",), 32),
        name="norm_mod_q",
    )(x, g.reshape(1, D), mod, mod)


def _in_proj_kernel(h_ref, w_ref, c_ref, s1_ref, s2_ref, o_ref, *, per_sec):
    tn = o_ref.shape[1]
    acc = jnp.dot(h_ref[...], w_ref[...].astype(h_ref.dtype), preferred_element_type=F32)
    section = pl.program_id(1) // per_sec
    q_scale = jnp.where((section == 0) | (section == 2), QK_SCALE, 1.0).astype(F32)
    cs, s1, s2 = c_ref[...], s1_ref[...], s2_ref[...]
    for j in range(tn // HEAD_DIM):
        sl = slice(j * HEAD_DIM, (j + 1) * HEAD_DIM)
        t = acc[:, sl]
        r = (t * cs + pltpu.roll(t, HEAD_DIM - ROT_HALF, 1) * s1 + pltpu.roll(t, ROT_HALF, 1) * s2)
        o_ref[:, sl] = (r * q_scale).astype(o_ref.dtype)


def in_proj(h, w_in, tables):
    S, D = h.shape
    sec_width = w_in.shape[1] // 6
    tm = _tile(S, 1024)
    tn = _tile(sec_width, 512)
    per_sec = sec_width // tn
    src = lambda j: j + jnp.where(j >= 2 * per_sec, per_sec, 0)
    tspec = pl.BlockSpec((tm, LANES), lambda i, j: (i, 0))
    return pl.pallas_call(
        functools.partial(_in_proj_kernel, per_sec=per_sec),
        grid=(S // tm, 4 * per_sec),
        in_specs=[pl.BlockSpec((tm, D), lambda i, j: (i, 0)),
                  pl.BlockSpec((D, tn), lambda i, j: (0, src(j))),
                  tspec, tspec, tspec],
        out_specs=pl.BlockSpec((tm, tn), lambda i, j: (i, j)),
        out_shape=jax.ShapeDtypeStruct((S, 4 * sec_width), BF16),
        compiler_params=_params(("arbitrary", "arbitrary"), 48),
        name="in_proj",
    )(h, w_in, *tables)


def _v_proj_t_kernel(w_ref, h_ref, o_ref, wt_ref):
    @pl.when(pl.program_id(1) == 0)
    def _():
        wt_ref[...] = w_ref[...].astype(wt_ref.dtype).T

    o_ref[...] = lax.dot_general(wt_ref[...], h_ref[...], NT_DIMS, preferred_element_type=F32).astype(o_ref.dtype)


def v_proj_t(h, w_in):
    S, D = h.shape
    sec_width = w_in.shape[1] // 6
    tn = _tile(sec_width, 512)
    ts = _tile(S, 1024)
    per_sec = sec_width // tn
    src = lambda j: j + jnp.where(j >= per_sec, 4 * per_sec, 2 * per_sec)
    return pl.pallas_call(
        _v_proj_t_kernel,
        grid=(2 * per_sec, S // ts),
        in_specs=[pl.BlockSpec((D, tn), lambda j, i: (0, src(j))), pl.BlockSpec((ts, D), lambda j, i: (i, 0))],
        out_specs=pl.BlockSpec((tn, ts), lambda j, i: (j, i)),
        out_shape=jax.ShapeDtypeStruct((2 * sec_width, S), BF16),
        scratch_shapes=[pltpu.VMEM((tn, D), BF16)],
        compiler_params=_params(("arbitrary", "arbitrary"), 48),
        name="v_proj_t",
    )(w_in, h)


def _diff_attn_kernel(lam_ref, q_ref, k_ref, vt_ref, g_ref, o_ref,
                      m1, l1, a1, m2, l2, a2, sa_ref, sb_ref, *, tq, lam_init):
    i = pl.program_id(1)
    stats = ((m1, l1, a1), (m2, l2, a2))
    for m, l, a in stats:
        m[...] = jnp.full(m.shape, -jnp.inf, F32)
        l[...] = jnp.zeros(l.shape, F32)
        a[...] = jnp.zeros(a.shape, F32)

    def scores(kv, s_ref):
        k0 = pl.multiple_of(kv * tq, tq)
        for c in range(2):
            sl = slice(c * HEAD_DIM, (c + 1) * HEAD_DIM)
            s_ref[c] = lax.dot_general(k_ref[pl.ds(k0, tq), sl], q_ref[:, sl], NT_DIMS,
                                       preferred_element_type=F32)

    def softmax_pv(kv, s_ref, masked):
        k0 = pl.multiple_of(kv * tq, tq)
        vt = vt_ref[:, pl.ds(k0, tq)]
        for c, (m, l, a) in enumerate(stats):
            s = s_ref[c]
            if masked:
                keep = (lax.broadcasted_iota(jnp.int32, (tq, tq), 0) <= lax.broadcasted_iota(jnp.int32, (tq, tq), 1))
                s = jnp.where(keep, s, NEG_INF)
            m_prev = m[...]
            m_new = jnp.maximum(m_prev, jnp.max(s, axis=0, keepdims=True))
            alpha = jnp.exp2(m_prev - m_new)
            p = jnp.exp2(s - m_new)
            l[...] = alpha * l[...] + jnp.sum(p, axis=0, keepdims=True)
            a[...] = alpha * a[...] + jnp.dot(vt, p.astype(vt.dtype), preferred_element_type=F32)
            m[...] = m_new

    scores(0, sa_ref)

    def pair(n, carry):
        scores(2 * n + 1, sb_ref)
        softmax_pv(2 * n, sa_ref, False)
        scores(2 * n + 2, sa_ref)
        softmax_pv(2 * n + 1, sb_ref, False)
        return carry

    lax.fori_loop(0, i // 2, pair, 0)

    @pl.when(i % 2 == 0)
    def _():
        softmax_pv(i, sa_ref, True)

    @pl.when(i % 2 == 1)
    def _():
        scores(i, sb_ref)
        softmax_pv(i - 1, sa_ref, False)
        softmax_pv(i, sb_ref, True)

    lp = lam_ref[...]
    lam = (jnp.exp(jnp.sum(lp[0:1] * lp[1:2], axis=-1, keepdims=True))
           - jnp.exp(jnp.sum(lp[2:3] * lp[3:4], axis=-1, keepdims=True)) + lam_init)
    o = (a1[...] / l1[...] - lam * (a2[...] / l2[...])).T
    y = o * lax.rsqrt(jnp.mean(o * o, axis=-1, keepdims=True) + NORM_EPS) * g_ref[...]
    o_ref[...] = (y * (1 - lam_init)).astype(o_ref.dtype)


def diff_attn(proj, v_t, lam_rows, subln_g, n_heads, lam_init):
    S = proj.shape[0]
    tq = _tile(S, 512)
    hw = 2 * HEAD_DIM
    return pl.pallas_call(
        functools.partial(_diff_attn_kernel, tq=tq, lam_init=lam_init),
        grid=(n_heads, S // tq),
        in_specs=[pl.BlockSpec((4, HEAD_DIM), lambda h, i: (0, 0)),
                  pl.BlockSpec((tq, hw), lambda h, i: (i, h)),
                  pl.BlockSpec((S, hw), lambda h, i: (0, n_heads + h)),
                  pl.BlockSpec((hw, S), lambda h, i: (h, 0)),
                  pl.BlockSpec((1, hw), lambda h, i: (0, 0))],
        out_specs=pl.BlockSpec((tq, hw), lambda h, i: (i, h)),
        out_shape=jax.ShapeDtypeStruct((S, n_heads * hw), BF16),
        scratch_shapes=[pltpu.VMEM((1, tq), F32), pltpu.VMEM((1, tq), F32), pltpu.VMEM((hw, tq), F32),
                        pltpu.VMEM((1, tq), F32), pltpu.VMEM((1, tq), F32), pltpu.VMEM((hw, tq), F32),
                        pltpu.VMEM((2, tq, tq), F32), pltpu.VMEM((2, tq, tq), F32)],
        compiler_params=_params(("arbitrary", "arbitrary"), 48),
        name="diff_attn",
    )(lam_rows, proj, proj, v_t, subln_g.reshape(1, hw))


def _dil_bias(delta):
    count = jnp.zeros(delta.shape, F32)
    for window, dilation in DILATED_CONFIGS:
        ok = (delta >= 0) & (delta <= window) & ((delta & (dilation - 1)) == 0)
        count = count + jnp.where(ok, 1.0, 0.0)
    return jnp.where(count > 0, jnp.log2(jnp.maximum(count, 1.0)), NEG_INF)


DIL_HEADS_PER_STEP = 2


def _dil_attn_kernel(q_ref, k_ref, vt_ref, g_ref, o_ref, bias_ref, sa_ref, sb_ref, *, tq, win):
    i = pl.program_id(1)
    n_blocks = pl.num_programs(1)
    heads = o_ref.shape[1] // HEAD_DIM

    @pl.when((pl.program_id(0) == 0) & (i == 0))
    def _():
        kj = lax.broadcasted_iota(jnp.int32, bias_ref.shape, 0)
        qi = lax.broadcasted_iota(jnp.int32, bias_ref.shape, 1)
        bias_ref[...] = _dil_bias(qi + DIL_BACK - kj)

    def window(j):
        start = pl.multiple_of(jnp.maximum(j * tq - DIL_BACK, 0), tq)
        row0 = pl.multiple_of(jnp.maximum(DIL_BACK - j * tq, 0), tq)
        return start, row0

    def scores(j, s_ref):
        start, row0 = window(j)
        bias = bias_ref[pl.ds(row0, win), :]
        q0 = pl.multiple_of(j * tq, tq)
        for hd in range(heads):
            sl = slice(hd * HEAD_DIM, (hd + 1) * HEAD_DIM)
            s_ref[hd] = lax.dot_general(k_ref[pl.ds(start, win), sl], q_ref[pl.ds(q0, tq), sl], NT_DIMS,
                                        preferred_element_type=F32) + bias

    def softmax_pv(j, s_ref):
        start, _ = window(j)
        for hd in range(heads):
            sl = slice(hd * HEAD_DIM, (hd + 1) * HEAD_DIM)
            s = s_ref[hd]
            p = jnp.exp2(s - jnp.max(s, axis=0, keepdims=True))
            vtw = vt_ref[sl, pl.ds(start, win)]
            o_t = jnp.dot(vtw, p.astype(vtw.dtype), preferred_element_type=F32) / jnp.sum(p, axis=0, keepdims=True)
            o = o_t.T
            y = o * lax.rsqrt(jnp.mean(o * o, axis=-1, keepdims=True) + NORM_EPS) * g_ref[:, sl]
            o_ref[:, sl] = y.astype(o_ref.dtype)

    @pl.when(i == 0)
    def _():
        scores(0, sa_ref)

    nxt = jnp.minimum(i + 1, n_blocks - 1)

    @pl.when(i % 2 == 0)
    def _():
        scores(nxt, sb_ref)
        softmax_pv(i, sa_ref)

    @pl.when(i % 2 == 1)
    def _():
        scores(nxt, sa_ref)
        softmax_pv(i, sb_ref)


def dil_attn(proj, v_t, out_g, n_heads, col0, row0):
    S = proj.shape[0]
    for window, dilation in DILATED_CONFIGS:
        assert window % dilation == 0 and dilation & (dilation - 1) == 0
    tq = _tile(S, 256)
    win = DIL_BACK + tq
    hp = DIL_HEADS_PER_STEP
    hw = hp * HEAD_DIM
    assert S >= win and DIL_BACK % tq == 0 and win % LANES == 0
    assert n_heads % hp == 0 and col0 % hw == 0 and row0 % hw == 0
    c0 = col0 // hw
    r0 = row0 // hw
    ng = n_heads // hp
    return pl.pallas_call(
        functools.partial(_dil_attn_kernel, tq=tq, win=win),
        grid=(ng, S // tq),
        in_specs=[pl.BlockSpec((S, hw), lambda h, i: (0, c0 + h)),
                  pl.BlockSpec((S, hw), lambda h, i: (0, c0 + ng + h)),
                  pl.BlockSpec((hw, S), lambda h, i: (r0 + h, 0)),
                  pl.BlockSpec((1, hw), lambda h, i: (0, h))],
        out_specs=pl.BlockSpec((tq, hw), lambda h, i: (i, h)),
        out_shape=jax.ShapeDtypeStruct((S, n_heads * HEAD_DIM), BF16),
        scratch_shapes=[pltpu.VMEM((win + DIL_BACK, tq), F32),
                        pltpu.VMEM((hp, win, tq), F32), pltpu.VMEM((hp, win, tq), F32)],
        compiler_params=_params(("arbitrary", "arbitrary"), 56),
        name="dil_attn",
    )(proj, proj, v_t, out_g.reshape(1, n_heads * HEAD_DIM))


def _out_proj_kernel(ya_ref, yb_ref, wa_ref, wb_ref, x_ref, gate_ref, x1_ref):
    acc = (jnp.dot(ya_ref[...], wa_ref[...].astype(ya_ref.dtype), preferred_element_type=F32)
           + jnp.dot(yb_ref[...], wb_ref[...].astype(yb_ref.dtype), preferred_element_type=F32))
    x1_ref[...] = x_ref[...] + gate_ref[...] * acc


def out_proj(ya, yb, w_out, x, mod, gate_idx):
    S, D = x.shape
    kh = ya.shape[1]
    assert yb.shape[1] == kh and w_out.shape[0] == 2 * kh
    tm = _tile(S, 1024)
    tn = _tile(D, 512)
    nj = D // tn
    return pl.pallas_call(
        _out_proj_kernel,
        grid=(S // tm, nj),
        in_specs=[pl.BlockSpec((tm, kh), lambda i, j: (i, 0)),
                  pl.BlockSpec((tm, kh), lambda i, j: (i, 0)),
                  pl.BlockSpec((kh, tn), lambda i, j: (0, j)),
                  pl.BlockSpec((kh, tn), lambda i, j: (1, j)),
                  pl.BlockSpec((tm, tn), lambda i, j: (i, j)),
                  pl.BlockSpec((1, tn), lambda i, j: (0, gate_idx * nj + j))],
        out_specs=pl.BlockSpec((tm, tn), lambda i, j: (i, j)),
        out_shape=jax.ShapeDtypeStruct((S, D), F32),
        compiler_params=_params(("arbitrary", "arbitrary"), 48),
        name="out_proj",
    )(ya, yb, w_out, w_out, x, mod)


def _top16(x, tie_safe):
    rows = lax.broadcasted_iota(jnp.int32, x.shape, 0)
    rank = jnp.full(x.shape, PEER_TOPK, jnp.int32)
    vals = []
    for r in range(PEER_TOPK):
        m = jnp.max(x, axis=0, keepdims=True)
        if tie_safe:
            hit = rows == jnp.min(jnp.where(x == m, rows, x.shape[0]), axis=0, keepdims=True)
        else:
            hit = x == m
        rank = jnp.where(hit, r, rank)
        x = jnp.where(hit, -jnp.inf, x)
        vals.append(m)
    return jnp.concatenate(vals, axis=0), rank


def _route(s, tie_safe):
    a, rank0 = _top16(s[0], tie_safe)
    b, rank1 = _top16(s[1], tie_safe)
    n_q = [PEER_TOPK // (p + 1) for p in range(PEER_TOPK)]
    pad = -sum(n_q) % 8
    cand = jnp.concatenate([a[p:p + 1] + b[:n_q[p]] for p in range(PEER_TOPK)]
                           + [jnp.full((pad, a.shape[1]), -jnp.inf, F32)], axis=0)
    fin, sel_rank = _top16(cand, tie_safe)
    sel = jnp.where(sel_rank < PEER_TOPK, 1.0, 0.0)
    starts = [sum(n_q[:p]) for p in range(PEER_TOPK)]
    cnt = jnp.concatenate([jnp.sum(sel[starts[p]:starts[p] + n_q[p]], axis=0, keepdims=True)
                           for p in range(PEER_TOPK)], axis=0)
    cnt0 = jnp.zeros(s[0].shape, F32)
    for p in range(PEER_TOPK):
        cnt0 = jnp.where(rank0 == p, cnt[p:p + 1], cnt0)
    z = jnp.sum(jnp.exp(fin - fin[0:1]), axis=0, keepdims=True)
    removed = sum(jnp.sum(jnp.where(r < PEER_TOPK, 1.0, 0.0), axis=0, keepdims=True)
                  for r in (rank0, rank1, sel_rank))
    return rank1.astype(F32), cnt0, jnp.exp(s[0] - a[0:1]), jnp.exp(s[1] - b[0:1]) / z, removed


def _peer_router_kernel(h_ref, wq_ref, sk_ref, rank1_ref, cnt0_ref, e0_ref, e1_ref):
    q = jnp.dot(h_ref[...], wq_ref[...].astype(h_ref.dtype), preferred_element_type=F32)
    half = q.shape[1] // 2
    s = [lax.dot_general(sk_ref[0, p], q[:, p * half:(p + 1) * half], NT_DIMS,
                         precision=lax.Precision.HIGHEST, preferred_element_type=F32)
         for p in range(2)]

    def emit(tie_safe):
        *outs, removed = _route(s, tie_safe)
        for ref, val in zip((rank1_ref, cnt0_ref, e0_ref, e1_ref), outs):
            ref[0] = val.astype(ref.dtype)
        return removed

    removed = emit(False)
    tie = jnp.max(jnp.abs(removed - 3.0 * PEER_TOPK)) > 0.0

    @pl.when(tie)
    def _():
        emit(True)


def peer_router(h, w_q, subkeys):
    S, D = h.shape
    n_heads = subkeys.shape[0]
    qd = w_q.shape[1] // n_heads
    T = _tile(S, 512)
    ospec = pl.BlockSpec((1, N_KEYS, T), lambda t, hd: (hd, 0, t))
    return pl.pallas_call(
        _peer_router_kernel,
        grid=(S // T, n_heads),
        in_specs=[pl.BlockSpec((T, D), lambda t, hd: (t, 0)),
                  pl.BlockSpec((D, qd), lambda t, hd: (0, hd)),
                  pl.BlockSpec((1, 2, N_KEYS, qd // 2), lambda t, hd: (hd, 0, 0, 0))],
        out_specs=[ospec] * 4,
        out_shape=[jax.ShapeDtypeStruct((n_heads, N_KEYS, S), BF16)] * 4,
        compiler_params=_params(("arbitrary", "arbitrary"), 48),
        name="peer_router",
    )(h, w_q, subkeys)


def _gelu(x):
    return 0.5 * x * (1.0 + lax.erf(x * (0.5 ** 0.5)))


PEER_GI = 8
PEER_GJ = 64
PEER_CHUNKS = 4


def _peer_expert_kernel(h_ref, hs_ref, wi_ref, u_ref, us_ref, v_ref, vs_ref, rank1_ref, cnt0_ref, e0_ref, e1_ref,
                        x_ref, ws_ref, gate_ref, fg_ref, o_ref, aa_ref, ab_ref, w8a_ref, w8b_ref, *, nj, normalize):
    e = pl.program_id(1)
    ne = pl.num_programs(1) - 2
    gi, gj, D = u_ref.shape
    n_heads, _, T = rank1_ref.shape
    eb = gi * gj
    kc = D // PEER_CHUNKS
    assert T == PEER_CHUNKS * LANES
    ew = jnp.clip(e - 1, 0, ne - 1)
    i0 = pl.multiple_of((ew // nj) * gi, gi)
    j0 = pl.multiple_of((ew % nj) * gj, gj)

    @pl.when(e == 0)
    def _():
        o_ref[...] = jnp.zeros(o_ref.shape, F32)
        ab_ref[...] = jnp.zeros(ab_ref.shape, F32)
        w8a_ref[...] = jnp.zeros(w8a_ref.shape, w8a_ref.dtype)

    def step(a_next, a_cur, w8_cur, w8_prev):
        u_scale = us_ref[...].reshape(eb, LANES)
        v_scale = vs_ref[...].reshape(eb, LANES)

        def piece(c, carry):
            cols = pl.ds(pl.multiple_of(c * kc, kc), kc)
            part = lax.dot_general(u_ref[:, :, cols].reshape(eb, kc), h_ref[:, cols], NT_DIMS,
                                   preferred_element_type=F32)
            a_next[...] = jnp.where(c == 0, part, a_next[...] + part)
            y = lax.dot_general(w8_prev[...], v_ref[:, :, cols].reshape(eb, kc), TN_DIMS,
                                preferred_element_type=F32)
            o_ref[:, cols] += y
            lsl = pl.ds(pl.multiple_of(c * LANES, LANES), LANES)
            act = _gelu(a_cur[:, lsl] * (u_scale * hs_ref[:, lsl])) * (v_scale * wi_ref[:, lsl])
            c0 = [cnt0_ref[hd, pl.ds(i0, gi), lsl] for hd in range(n_heads)]
            e0 = [e0_ref[hd, pl.ds(i0, gi), lsl] for hd in range(n_heads)]
            for ii in range(gi):
                rsl = slice(ii * gj, (ii + 1) * gj)
                g = jnp.zeros((gj, LANES), BF16)
                for hd in range(n_heads):
                    r1 = rank1_ref[hd, pl.ds(j0, gj), lsl]
                    e1 = e1_ref[hd, pl.ds(j0, gj), lsl]
                    g = g + jnp.where(r1 < c0[hd][ii:ii + 1], e1 * e0[hd][ii:ii + 1], jnp.zeros((), BF16))
                w8_cur[rsl, lsl] = (g.astype(F32) * act[rsl]).astype(w8_cur.dtype)
            return carry

        lax.fori_loop(0, PEER_CHUNKS, piece, 0)

    pl.when(e % 2 == 0)(lambda: step(aa_ref, ab_ref, w8b_ref, w8a_ref))
    pl.when(e % 2 == 1)(lambda: step(ab_ref, aa_ref, w8a_ref, w8b_ref))

    @pl.when(e == pl.num_programs(1) - 1)
    def _():
        x = x_ref[...] + gate_ref[...] * (o_ref[...] * ws_ref[...])
        if normalize:
            x = x * lax.rsqrt(jnp.mean(x * x, axis=-1, keepdims=True) + NORM_EPS) * fg_ref[...]
        o_ref[...] = x


def _quant_rows_kernel(w_ref, q_ref, s_ref, n_ref):
    w = w_ref[...]
    scale = jnp.maximum(jnp.max(jnp.abs(w), axis=-1, keepdims=True), FP8_TINY) / FP8_MAX
    q_ref[...] = (w / scale).astype(q_ref.dtype)
    s_ref[...] = jnp.broadcast_to(scale, s_ref.shape)
    n_ref[...] = jnp.sqrt(jnp.sum(w * w, axis=-1, keepdims=True))


def quant_rows(w):
    E, D = w.shape
    te = _tile(E, 512)
    return pl.pallas_call(
        _quant_rows_kernel,
        grid=(E // te,),
        in_specs=[pl.BlockSpec((te, D), lambda i: (i, 0))],
        out_specs=[pl.BlockSpec((te, D), lambda i: (i, 0)), pl.BlockSpec((te, LANES), lambda i: (i, 0)),
                   pl.BlockSpec((te, 1), lambda i: (i, 0))],
        out_shape=[jax.ShapeDtypeStruct((E, D), F8), jax.ShapeDtypeStruct((E, LANES), F32),
                   jax.ShapeDtypeStruct((E, 1), F32)],
        compiler_params=_params(("arbitrary",), 32),
        name="quant_rows",
    )(w)


PEER_W_MARGIN = 1.25


def peer_experts(h8, h_scale, h_norm, u, v, tables, x, mod, gate_idx, final_g, normalize):
    S, D = h8.shape
    n_heads = tables[0].shape[0]
    T = _tile(S, 512)
    u8, su, u_norm = quant_rows(u)
    v8, sv, _ = quant_rows(v)
    w_bound = (PEER_W_MARGIN * n_heads) * h_norm * jnp.max(u_norm * sv[:, :1])
    w_scale = jnp.maximum(w_bound, FP8_TINY) / FP8_MAX
    u3 = u8.reshape(N_KEYS, N_KEYS, D)
    v3 = v8.reshape(N_KEYS, N_KEYS, D)
    su3 = su.reshape(N_KEYS, N_KEYS, LANES)
    sv3 = sv.reshape(N_KEYS, N_KEYS, LANES)
    ni, nj = N_KEYS // PEER_GI, N_KEYS // PEER_GJ
    ne = ni * nj
    eb = PEER_GI * PEER_GJ

    def u_idx(t, e):
        eu = jnp.minimum(e, ne - 1)
        return eu // nj, eu % nj, 0

    def w_idx(t, e):
        ew = jnp.clip(e - 1, 0, ne - 1)
        return ew // nj, ew % nj, 0

    def v_idx(t, e):
        ev = jnp.maximum(e - 2, 0)
        return ev // nj, ev % nj, 0

    once = pl.Buffered(1)
    tspec = pl.BlockSpec((n_heads, N_KEYS, T), lambda t, e: (0, 0, t), pipeline_mode=once)
    a_buf = pltpu.VMEM((eb, T), F32)
    w8_buf = pltpu.VMEM((eb, T), F8)
    row = pl.BlockSpec((1, T), lambda t, e: (0, t))
    return pl.pallas_call(
        functools.partial(_peer_expert_kernel, nj=nj, normalize=normalize),
        grid=(S // T, ne + 2),
        in_specs=[pl.BlockSpec((T, D), lambda t, e: (t, 0), pipeline_mode=once),
                  row, row,
                  pl.BlockSpec((PEER_GI, PEER_GJ, D), u_idx),
                  pl.BlockSpec((PEER_GI, PEER_GJ, LANES), w_idx),
                  pl.BlockSpec((PEER_GI, PEER_GJ, D), v_idx),
                  pl.BlockSpec((PEER_GI, PEER_GJ, LANES), w_idx),
                  tspec, tspec, tspec, tspec,
                  pl.BlockSpec((T, D), lambda t, e: (t, 0), pipeline_mode=once),
                  pl.BlockSpec((T, 1), lambda t, e: (t, 0), pipeline_mode=once),
                  pl.BlockSpec((1, D), lambda t, e: (0, gate_idx)),
                  pl.BlockSpec((1, D), lambda t, e: (0, 0))],
        out_specs=pl.BlockSpec((T, D), lambda t, e: (t, 0)),
        out_shape=jax.ShapeDtypeStruct((S, D), F32),
        scratch_shapes=[a_buf, a_buf, w8_buf, w8_buf],
        compiler_params=_params(("arbitrary", "arbitrary"), 56),
        name="peer_experts",
    )(h8, h_scale.reshape(1, S), (1.0 / w_scale).reshape(1, S), u3, su3, v3, sv3, *tables,
      x, w_scale, mod, final_g.reshape(1, D))


def kernel(x, c, positions, norm1_g, norm2_g, w_ada, b_ada, w_in, lam_q1, lam_k1, lam_q2, lam_k2,
           diff_subln_g, dil_out_g, w_out, peer_wq, peer_subkeys, peer_u, peer_v, final_g):
    B, S, D = x.shape
    depth = w_ada.shape[0]
    assert B == 1, "one sequence per call"
    diff_width = D // 2
    diff_heads = diff_width // (2 * HEAD_DIM)
    dil_heads = (D - diff_width) // HEAD_DIM
    xs = x.reshape(S, D)
    tables = rope_tables(positions.reshape(S))
    for l in range(depth):
        lam_init = 0.8 - 0.6 * math.exp(-0.3 * l)
        mod = ada_mod(c, w_ada[l], b_ada[l])
        h = norm_mod(xs, norm1_g[l], mod, 0, 1)
        proj = in_proj(h, w_in[l], tables)
        v_t = v_proj_t(h, w_in[l])
        lam_rows = jnp.stack([lam_q1[l], lam_k1[l], lam_q2[l], lam_k2[l]])
        y_diff = diff_attn(proj, v_t, lam_rows, diff_subln_g[l], diff_heads, lam_init)
        y_dil = dil_attn(proj, v_t, dil_out_g[l], dil_heads, 2 * diff_width, diff_width)
        xs = out_proj(y_diff, y_dil, w_out[l], xs, mod, 2)
        h, h8, h_scale, h_norm = norm_mod_q(xs, norm2_g[l], mod, 3, 4)
        gates = peer_router(h, peer_wq[l], peer_subkeys[l])
        xs = peer_experts(h8, h_scale, h_norm, peer_u[l], peer_v[l], gates, xs, mod, 5, final_g,
                          normalize=(l == depth - 1))
    return xs.reshape(B, S, D)
```

```python
import functools
import math

import jax
import jax.numpy as jnp
from jax import lax
from jax.experimental import pallas as pl
from jax.experimental.pallas import tpu as pltpu

F32 = jnp.float32
BF16 = jnp.bfloat16
F8 = jnp.float8_e4m3fn
FP8_MAX = float(jnp.finfo(F8).max)
FP8_TINY = 1e-30

HEAD_DIM = 128
ROT_DIM = HEAD_DIM // 4
ROT_HALF = ROT_DIM // 2
ROPE_THETA = 500000.0
DILATED_CONFIGS = ((128, 1), (512, 4), (2048, 16))
DIL_BACK = max(w for w, _ in DILATED_CONFIGS)
PEER_HEADS = 8
N_KEYS = 128
PEER_TOPK = 16
NORM_EPS = 1e-6
NEG_INF = -1e30
QK_SCALE = HEAD_DIM ** -0.5 * math.log2(math.e)
DIFF_ROW_SPLIT = 2
LANES = 128
MIB = 1024 * 1024

NT_DIMS = (((1,), (1,)), ((), ()))
TN_DIMS = (((0,), (0,)), ((), ()))


def _params(semantics, vmem_mib):
    return pltpu.CompilerParams(dimension_semantics=semantics, vmem_limit_bytes=vmem_mib * MIB)


def _tile(n, pref):
    t = min(n, pref)
    assert n % t == 0, (n, pref)
    return t


def _ada_kernel(c_ref, w_ref, b_ref, o_ref):
    c = c_ref[...]
    ca = c * jax.nn.sigmoid(c)
    for j in range(o_ref.shape[1] // LANES):
        sl = slice(j * LANES, (j + 1) * LANES)
        o_ref[:, sl] = jnp.sum(w_ref[:, sl] * ca, axis=0, keepdims=True) + b_ref[:, sl]


def ada_mod(c, w_ada, b_ada):
    D, N = w_ada.shape
    tn = _tile(N, 1024)
    cb = jnp.broadcast_to(c.reshape(D, 1), (D, LANES))
    return pl.pallas_call(
        _ada_kernel,
        grid=(N // tn,),
        in_specs=[pl.BlockSpec((D, LANES), lambda j: (0, 0)),
                  pl.BlockSpec((D, tn), lambda j: (0, j)),
                  pl.BlockSpec((1, tn), lambda j: (0, j))],
        out_specs=pl.BlockSpec((1, tn), lambda j: (0, j)),
        out_shape=jax.ShapeDtypeStruct((1, N), F32),
        compiler_params=_params(("arbitrary",), 48),
        name="ada_mod",
    )(cb, w_ada, b_ada.reshape(1, N))


def _rope_kernel(pos_ref, freq_ref, c_ref, s1_ref, s2_ref):
    ang = pos_ref[...].astype(F32) * freq_ref[...]
    lane = lax.broadcasted_iota(jnp.int32, ang.shape, 1)
    cs = jnp.cos(ang)
    sn = jnp.sin(ang)
    c_ref[...] = jnp.where(lane < ROT_DIM, cs, 1.0)
    s1_ref[...] = jnp.where(lane < ROT_HALF, -sn, 0.0)
    s2_ref[...] = jnp.where((lane >= ROT_HALF) & (lane < ROT_DIM), sn, 0.0)


def rope_tables(positions):
    S = positions.shape[0]
    ts = _tile(S, 1024)
    inv_freq = jnp.power(ROPE_THETA, -jnp.arange(0, ROT_DIM, 2, dtype=F32) / ROT_DIM)
    freq = jnp.tile(inv_freq, LANES // ROT_HALF).reshape(1, LANES)
    spec = pl.BlockSpec((ts, LANES), lambda i: (i, 0))
    return pl.pallas_call(
        _rope_kernel,
        grid=(S // ts,),
        in_specs=[pl.BlockSpec((ts, 1), lambda i: (i, 0)), pl.BlockSpec((1, LANES), lambda i: (0, 0))],
        out_specs=[spec, spec, spec],
        out_shape=[jax.ShapeDtypeStruct((S, LANES), F32)] * 3,
        compiler_params=_params(("arbitrary",), 32),
        name="rope_tables",
    )(positions.reshape(S, 1), freq)


def _rms_mod(x, g, scale, shift):
    y = x * lax.rsqrt(jnp.mean(x * x, axis=-1, keepdims=True) + NORM_EPS) * g
    return y * (1 + scale) + shift


def _norm_mod_kernel(x_ref, g_ref, sh_ref, sc_ref, o_ref):
    o_ref[...] = _rms_mod(x_ref[...], g_ref[...], sc_ref[...], sh_ref[...]).astype(o_ref.dtype)


def norm_mod(x, g, mod, shift_idx, scale_idx):
    S, D = x.shape
    tm = _tile(S, 256)
    return pl.pallas_call(
        _norm_mod_kernel,
        grid=(S // tm,),
        in_specs=[pl.BlockSpec((tm, D), lambda i: (i, 0)),
                  pl.BlockSpec((1, D), lambda i: (0, 0)),
                  pl.BlockSpec((1, D), lambda i: (0, shift_idx)),
                  pl.BlockSpec((1, D), lambda i: (0, scale_idx))],
        out_specs=pl.BlockSpec((tm, D), lambda i: (i, 0)),
        out_shape=jax.ShapeDtypeStruct((S, D), BF16),
        compiler_params=_params(("arbitrary",), 32),
        name="norm_mod",
    )(x, g.reshape(1, D), mod, mod)


def _norm_mod_q_kernel(x_ref, g_ref, sh_ref, sc_ref, o_ref, q_ref, s_ref, n_ref):
    y = _rms_mod(x_ref[...], g_ref[...], sc_ref[...], sh_ref[...])
    o_ref[...] = y.astype(o_ref.dtype)
    scale = jnp.maximum(jnp.max(jnp.abs(y), axis=-1, keepdims=True), FP8_TINY) / FP8_MAX
    q_ref[...] = (y / scale).astype(q_ref.dtype)
    s_ref[...] = scale
    n_ref[...] = jnp.sqrt(jnp.sum(y * y, axis=-1, keepdims=True))


def norm_mod_q(x, g, mod, shift_idx, scale_idx):
    S, D = x.shape
    tm = _tile(S, 256)
    blk = pl.BlockSpec((tm, D), lambda i: (i, 0))
    return pl.pallas_call(
        _norm_mod_q_kernel,
        grid=(S // tm,),
        in_specs=[blk,
                  pl.BlockSpec((1, D), lambda i: (0, 0)),
                  pl.BlockSpec((1, D), lambda i: (0, shift_idx)),
                  pl.BlockSpec((1, D), lambda i: (0, scale_idx))],
        out_specs=[blk, blk, pl.BlockSpec((tm, 1), lambda i: (i, 0)), pl.BlockSpec((tm, 1), lambda i: (i, 0))],
        out_shape=[jax.ShapeDtypeStruct((S, D), BF16), jax.ShapeDtypeStruct((S, D), F8),
                   jax.ShapeDtypeStruct((S, 1), F32), jax.ShapeDtypeStruct((S, 1), F32)],
        compiler_params=_params(("arbitrary",), 32),
        name="norm_mod_q",
    )(x, g.reshape(1, D), mod, mod)


def _in_proj_kernel(h_ref, w_ref, c_ref, s1_ref, s2_ref, o_ref, acc_a, acc_b, *, per_sec):
    j = pl.program_id(1)
    tn = o_ref.shape[1]
    section = jnp.maximum(j - 1, 0) // per_sec
    q_scale = jnp.where((section == 0) | (section == 2), QK_SCALE, 1.0).astype(F32)

    @pl.when((pl.program_id(0) == 0) & (j == 0))
    def _():
        acc_b[...] = jnp.zeros(acc_b.shape, F32)

    def step(acc_mm, acc_ep):
        acc_mm[...] = jnp.dot(h_ref[...], w_ref[...].astype(h_ref.dtype), preferred_element_type=F32)
        cs, s1, s2 = c_ref[...], s1_ref[...], s2_ref[...]
        for k in range(tn // HEAD_DIM):
            sl = slice(k * HEAD_DIM, (k + 1) * HEAD_DIM)
            t = acc_ep[:, sl]
            r = (t * cs + pltpu.roll(t, HEAD_DIM - ROT_HALF, 1) * s1 + pltpu.roll(t, ROT_HALF, 1) * s2)
            o_ref[:, sl] = (r * q_scale).astype(o_ref.dtype)

    pl.when(j % 2 == 0)(lambda: step(acc_a, acc_b))
    pl.when(j % 2 == 1)(lambda: step(acc_b, acc_a))


def in_proj(h, w_in, tables):
    S, D = h.shape
    sec_width = w_in.shape[1] // 6
    tm = _tile(S, 1024)
    tn = _tile(sec_width, 512)
    per_sec = sec_width // tn
    nj = 4 * per_sec

    def src(j):
        jm = jnp.minimum(j, nj - 1)
        return jm + jnp.where(jm >= 2 * per_sec, per_sec, 0)

    tspec = pl.BlockSpec((tm, LANES), lambda i, j: (i, 0))
    acc = pltpu.VMEM((tm, tn), F32)
    return pl.pallas_call(
        functools.partial(_in_proj_kernel, per_sec=per_sec),
        grid=(S // tm, nj + 1),
        in_specs=[pl.BlockSpec((tm, D), lambda i, j: (i, 0)),
                  pl.BlockSpec((D, tn), lambda i, j: (0, src(j))),
                  tspec, tspec, tspec],
        out_specs=pl.BlockSpec((tm, tn), lambda i, j: (i, jnp.maximum(j - 1, 0))),
        out_shape=jax.ShapeDtypeStruct((S, 4 * sec_width), BF16),
        scratch_shapes=[acc, acc],
        compiler_params=_params(("arbitrary", "arbitrary"), 48),
        name="in_proj",
    )(h, w_in, *tables)


def _v_proj_t_kernel(w_ref, h_ref, o_ref, wt_ref):
    @pl.when(pl.program_id(1) == 0)
    def _():
        wt_ref[...] = w_ref[...].astype(wt_ref.dtype).T

    o_ref[...] = lax.dot_general(wt_ref[...], h_ref[...], NT_DIMS, preferred_element_type=F32).astype(o_ref.dtype)


def v_proj_t(h, w_in):
    S, D = h.shape
    sec_width = w_in.shape[1] // 6
    tn = _tile(sec_width, 512)
    ts = _tile(S, 1024)
    per_sec = sec_width // tn
    src = lambda j: j + jnp.where(j >= per_sec, 4 * per_sec, 2 * per_sec)
    return pl.pallas_call(
        _v_proj_t_kernel,
        grid=(2 * per_sec, S // ts),
        in_specs=[pl.BlockSpec((D, tn), lambda j, i: (0, src(j))), pl.BlockSpec((ts, D), lambda j, i: (i, 0))],
        out_specs=pl.BlockSpec((tn, ts), lambda j, i: (j, i)),
        out_shape=jax.ShapeDtypeStruct((2 * sec_width, S), BF16),
        scratch_shapes=[pltpu.VMEM((tn, D), BF16)],
        compiler_params=_params(("arbitrary", "arbitrary"), 48),
        name="v_proj_t",
    )(w_in, h)


def _diff_attn_kernel(lam_ref, q_ref, k_ref, vt_ref, g_ref, o_ref,
                      m1, l1, a1, m2, l2, a2, sa_ref, sb_ref, *, tq, lam_init):
    i = pl.program_id(1)
    stats = ((m1, l1, a1), (m2, l2, a2))
    for m, l, a in stats:
        m[...] = jnp.full(m.shape, -jnp.inf, F32)
        l[...] = jnp.zeros(l.shape, F32)
        a[...] = jnp.zeros(a.shape, F32)

    def scores(kv, s_ref):
        k0 = pl.multiple_of(kv * tq, tq)
        for c in range(2):
            sl = slice(c * HEAD_DIM, (c + 1) * HEAD_DIM)
            s_ref[c] = lax.dot_general(k_ref[pl.ds(k0, tq), sl], q_ref[:, sl], NT_DIMS,
                                       preferred_element_type=F32)

    def softmax_pv(kv, s_ref, masked):
        k0 = pl.multiple_of(kv * tq, tq)
        vt = vt_ref[:, pl.ds(k0, tq)]
        for c, (m, l, a) in enumerate(stats):
            s = s_ref[c]
            if masked:
                keep = (lax.broadcasted_iota(jnp.int32, (tq, tq), 0) <= lax.broadcasted_iota(jnp.int32, (tq, tq), 1))
                s = jnp.where(keep, s, NEG_INF)
            m_prev = m[...]
            m_new = jnp.maximum(m_prev, jnp.max(s, axis=0, keepdims=True))
            alpha = jnp.exp2(m_prev - m_new)
            p = jnp.exp2(s - m_new)
            l[...] = alpha * l[...] + jnp.sum(p, axis=0, keepdims=True)
            a[...] = alpha * a[...] + jnp.dot(vt, p.astype(vt.dtype), preferred_element_type=F32)
            m[...] = m_new

    scores(0, sa_ref)

    def pair(n, carry):
        scores(2 * n + 1, sb_ref)
        softmax_pv(2 * n, sa_ref, False)
        scores(2 * n + 2, sa_ref)
        softmax_pv(2 * n + 1, sb_ref, False)
        return carry

    lax.fori_loop(0, i // 2, pair, 0)

    @pl.when(i % 2 == 0)
    def _():
        softmax_pv(i, sa_ref, True)

    @pl.when(i % 2 == 1)
    def _():
        scores(i, sb_ref)
        softmax_pv(i - 1, sa_ref, False)
        softmax_pv(i, sb_ref, True)

    lp = lam_ref[...]
    lam = (jnp.exp(jnp.sum(lp[0:1] * lp[1:2], axis=-1, keepdims=True))
           - jnp.exp(jnp.sum(lp[2:3] * lp[3:4], axis=-1, keepdims=True)) + lam_init)
    o = (a1[...] / l1[...] - lam * (a2[...] / l2[...])).T
    y = o * lax.rsqrt(jnp.mean(o * o, axis=-1, keepdims=True) + NORM_EPS) * g_ref[...]
    o_ref[...] = (y * (1 - lam_init)).astype(o_ref.dtype)


def diff_attn(proj, v_t, lam_rows, subln_g, n_heads, lam_init):
    S = proj.shape[0]
    tq = _tile(S, 512)
    hw = 2 * HEAD_DIM
    return pl.pallas_call(
        functools.partial(_diff_attn_kernel, tq=tq, lam_init=lam_init),
        grid=(n_heads, S // tq),
        in_specs=[pl.BlockSpec((4, HEAD_DIM), lambda h, i: (0, 0)),
                  pl.BlockSpec((tq, hw), lambda h, i: (i, h)),
                  pl.BlockSpec((S, hw), lambda h, i: (0, n_heads + h)),
                  pl.BlockSpec((hw, S), lambda h, i: (h, 0)),
                  pl.BlockSpec((1, hw), lambda h, i: (0, 0))],
        out_specs=pl.BlockSpec((tq, hw), lambda h, i: (i, h)),
        out_shape=jax.ShapeDtypeStruct((S, n_heads * hw), BF16),
        scratch_shapes=[pltpu.VMEM((1, tq), F32), pltpu.VMEM((1, tq), F32), pltpu.VMEM((hw, tq), F32),
                        pltpu.VMEM((1, tq), F32), pltpu.VMEM((1, tq), F32), pltpu.VMEM((hw, tq), F32),
                        pltpu.VMEM((2, tq, tq), F32), pltpu.VMEM((2, tq, tq), F32)],
        compiler_params=_params(("arbitrary", "arbitrary"), 48),
        name="diff_attn",
    )(lam_rows, proj, proj, v_t, subln_g.reshape(1, hw))


def _dil_bias(delta):
    count = jnp.zeros(delta.shape, F32)
    for window, dilation in DILATED_CONFIGS:
        ok = (delta >= 0) & (delta <= window) & ((delta & (dilation - 1)) == 0)
        count = count + jnp.where(ok, 1.0, 0.0)
    return jnp.where(count > 0, jnp.log2(jnp.maximum(count, 1.0)), NEG_INF)


DIL_HEADS_PER_STEP = 2


def _dil_attn_kernel(q_ref, k_ref, vt_ref, g_ref, o_ref, bias_ref, sa_ref, sb_ref, *, tq, win):
    i = pl.program_id(1)
    n_blocks = pl.num_programs(1)
    heads = o_ref.shape[1] // HEAD_DIM

    @pl.when((pl.program_id(0) == 0) & (i == 0))
    def _():
        kj = lax.broadcasted_iota(jnp.int32, bias_ref.shape, 0)
        qi = lax.broadcasted_iota(jnp.int32, bias_ref.shape, 1)
        bias_ref[...] = _dil_bias(qi + DIL_BACK - kj)

    def window(j):
        start = pl.multiple_of(jnp.maximum(j * tq - DIL_BACK, 0), tq)
        row0 = pl.multiple_of(jnp.maximum(DIL_BACK - j * tq, 0), tq)
        return start, row0

    def scores(j, s_ref):
        start, row0 = window(j)
        bias = bias_ref[pl.ds(row0, win), :]
        q0 = pl.multiple_of(j * tq, tq)
        for hd in range(heads):
            sl = slice(hd * HEAD_DIM, (hd + 1) * HEAD_DIM)
            s_ref[hd] = lax.dot_general(k_ref[pl.ds(start, win), sl], q_ref[pl.ds(q0, tq), sl], NT_DIMS,
                                        preferred_element_type=F32) + bias

    def softmax_pv(j, s_ref):
        start, _ = window(j)
        for hd in range(heads):
            sl = slice(hd * HEAD_DIM, (hd + 1) * HEAD_DIM)
            s = s_ref[hd]
            p = jnp.exp2(s - jnp.max(s, axis=0, keepdims=True))
            vtw = vt_ref[sl, pl.ds(start, win)]
            o_t = jnp.dot(vtw, p.astype(vtw.dtype), preferred_element_type=F32) / jnp.sum(p, axis=0, keepdims=True)
            o = o_t.T
            y = o * lax.rsqrt(jnp.mean(o * o, axis=-1, keepdims=True) + NORM_EPS) * g_ref[:, sl]
            o_ref[:, sl] = y.astype(o_ref.dtype)

    @pl.when(i == 0)
    def _():
        scores(0, sa_ref)

    nxt = jnp.minimum(i + 1, n_blocks - 1)

    @pl.when(i % 2 == 0)
    def _():
        scores(nxt, sb_ref)
        softmax_pv(i, sa_ref)

    @pl.when(i % 2 == 1)
    def _():
        scores(nxt, sa_ref)
        softmax_pv(i, sb_ref)


def dil_attn(proj, v_t, out_g, n_heads, col0, row0):
    S = proj.shape[0]
    for window, dilation in DILATED_CONFIGS:
        assert window % dilation == 0 and dilation & (dilation - 1) == 0
    tq = _tile(S, 256)
    win = DIL_BACK + tq
    hp = DIL_HEADS_PER_STEP
    hw = hp * HEAD_DIM
    assert S >= win and DIL_BACK % tq == 0 and win % LANES == 0
    assert n_heads % hp == 0 and col0 % hw == 0 and row0 % hw == 0
    c0 = col0 // hw
    r0 = row0 // hw
    ng = n_heads // hp
    return pl.pallas_call(
        functools.partial(_dil_attn_kernel, tq=tq, win=win),
        grid=(ng, S // tq),
        in_specs=[pl.BlockSpec((S, hw), lambda h, i: (0, c0 + h)),
                  pl.BlockSpec((S, hw), lambda h, i: (0, c0 + ng + h)),
                  pl.BlockSpec((hw, S), lambda h, i: (r0 + h, 0)),
                  pl.BlockSpec((1, hw), lambda h, i: (0, h))],
        out_specs=pl.BlockSpec((tq, hw), lambda h, i: (i, h)),
        out_shape=jax.ShapeDtypeStruct((S, n_heads * HEAD_DIM), BF16),
        scratch_shapes=[pltpu.VMEM((win + DIL_BACK, tq), F32),
                        pltpu.VMEM((hp, win, tq), F32), pltpu.VMEM((hp, win, tq), F32)],
        compiler_params=_params(("arbitrary", "arbitrary"), 56),
        name="dil_attn",
    )(proj, proj, v_t, out_g.reshape(1, n_heads * HEAD_DIM))


def _out_proj_kernel(ya_ref, yb_ref, wa_ref, wb_ref, x_ref, gate_ref, x1_ref):
    acc = (jnp.dot(ya_ref[...], wa_ref[...].astype(ya_ref.dtype), preferred_element_type=F32)
           + jnp.dot(yb_ref[...], wb_ref[...].astype(yb_ref.dtype), preferred_element_type=F32))
    x1_ref[...] = x_ref[...] + gate_ref[...] * acc


def out_proj(ya, yb, w_out, x, mod, gate_idx):
    S, D = x.shape
    kh = ya.shape[1]
    assert yb.shape[1] == kh and w_out.shape[0] == 2 * kh
    tm = _tile(S, 1024)
    tn = _tile(D, 512)
    nj = D // tn
    return pl.pallas_call(
        _out_proj_kernel,
        grid=(S // tm, nj),
        in_specs=[pl.BlockSpec((tm, kh), lambda i, j: (i, 0)),
                  pl.BlockSpec((tm, kh), lambda i, j: (i, 0)),
                  pl.BlockSpec((kh, tn), lambda i, j: (0, j)),
                  pl.BlockSpec((kh, tn), lambda i, j: (1, j)),
                  pl.BlockSpec((tm, tn), lambda i, j: (i, j)),
                  pl.BlockSpec((1, tn), lambda i, j: (0, gate_idx * nj + j))],
        out_specs=pl.BlockSpec((tm, tn), lambda i, j: (i, j)),
        out_shape=jax.ShapeDtypeStruct((S, D), F32),
        compiler_params=_params(("arbitrary", "arbitrary"), 48),
        name="out_proj",
    )(ya, yb, w_out, w_out, x, mod)


def _top16(x, tie_safe):
    rows = lax.broadcasted_iota(jnp.int32, x.shape, 0)
    rank = jnp.full(x.shape, PEER_TOPK, jnp.int32)
    vals = []
    for r in range(PEER_TOPK):
        m = jnp.max(x, axis=0, keepdims=True)
        if tie_safe:
            hit = rows == jnp.min(jnp.where(x == m, rows, x.shape[0]), axis=0, keepdims=True)
        else:
            hit = x == m
        rank = jnp.where(hit, r, rank)
        x = jnp.where(hit, -jnp.inf, x)
        vals.append(m)
    return jnp.concatenate(vals, axis=0), rank


def _route(s, tie_safe):
    a, rank0 = _top16(s[0], tie_safe)
    b, rank1 = _top16(s[1], tie_safe)
    n_q = [PEER_TOPK // (p + 1) for p in range(PEER_TOPK)]
    pad = -sum(n_q) % 8
    cand = jnp.concatenate([a[p:p + 1] + b[:n_q[p]] for p in range(PEER_TOPK)]
                           + [jnp.full((pad, a.shape[1]), -jnp.inf, F32)], axis=0)
    fin, sel_rank = _top16(cand, tie_safe)
    sel = jnp.where(sel_rank < PEER_TOPK, 1.0, 0.0)
    starts = [sum(n_q[:p]) for p in range(PEER_TOPK)]
    cnt = jnp.concatenate([jnp.sum(sel[starts[p]:starts[p] + n_q[p]], axis=0, keepdims=True)
                           for p in range(PEER_TOPK)], axis=0)
    cnt0 = jnp.zeros(s[0].shape, F32)
    for p in range(PEER_TOPK):
        cnt0 = jnp.where(rank0 == p, cnt[p:p + 1], cnt0)
    z = jnp.sum(jnp.exp(fin - fin[0:1]), axis=0, keepdims=True)
    removed = sum(jnp.sum(jnp.where(r < PEER_TOPK, 1.0, 0.0), axis=0, keepdims=True)
                  for r in (rank0, rank1, sel_rank))
    return rank1.astype(F32), cnt0, jnp.exp(s[0] - a[0:1]), jnp.exp(s[1] - b[0:1]) / z, removed


def _peer_router_kernel(h_ref, wq_ref, sk_ref, rank1_ref, cnt0_ref, e0_ref, e1_ref):
    q = jnp.dot(h_ref[...], wq_ref[...].astype(h_ref.dtype), preferred_element_type=F32)
    half = q.shape[1] // 2
    s = [lax.dot_general(sk_ref[0, p], q[:, p * half:(p + 1) * half], NT_DIMS,
                         precision=lax.Precision.HIGHEST, preferred_element_type=F32)
         for p in range(2)]

    def emit(tie_safe):
        *outs, removed = _route(s, tie_safe)
        for ref, val in zip((rank1_ref, cnt0_ref, e0_ref, e1_ref), outs):
            ref[0] = val.astype(ref.dtype)
        return removed

    removed = emit(False)
    tie = jnp.max(jnp.abs(removed - 3.0 * PEER_TOPK)) > 0.0

    @pl.when(tie)
    def _():
        emit(True)


def peer_router(h, w_q, subkeys):
    S, D = h.shape
    n_heads = subkeys.shape[0]
    qd = w_q.shape[1] // n_heads
    T = _tile(S, 512)
    ospec = pl.BlockSpec((1, N_KEYS, T), lambda t, hd: (hd, 0, t))
    return pl.pallas_call(
        _peer_router_kernel,
        grid=(S // T, n_heads),
        in_specs=[pl.BlockSpec((T, D), lambda t, hd: (t, 0)),
                  pl.BlockSpec((D, qd), lambda t, hd: (0, hd)),
                  pl.BlockSpec((1, 2, N_KEYS, qd // 2), lambda t, hd: (hd, 0, 0, 0))],
        out_specs=[ospec] * 4,
        out_shape=[jax.ShapeDtypeStruct((n_heads, N_KEYS, S), BF16)] * 4,
        compiler_params=_params(("arbitrary", "arbitrary"), 48),
        name="peer_router",
    )(h, w_q, subkeys)


def _gelu(x):
    return 0.5 * x * (1.0 + lax.erf(x * (0.5 ** 0.5)))


PEER_GI = 8
PEER_GJ = 64
PEER_CHUNKS = 2


def _peer_expert_kernel(h_ref, hs_ref, wi_ref, u_ref, us_ref, v_ref, vs_ref, rank1_ref, cnt0_ref, e0_ref, e1_ref,
                        x_ref, ws_ref, gate_ref, fg_ref, o_ref, aa_ref, ab_ref, w8a_ref, w8b_ref, *, nj, normalize):
    e = pl.program_id(1)
    ne = pl.num_programs(1) - 2
    gi, gj, D = u_ref.shape
    n_heads, _, T = rank1_ref.shape
    eb = gi * gj
    kc = D // PEER_CHUNKS
    lanes_per_piece = T // PEER_CHUNKS
    assert lanes_per_piece % LANES == 0
    ew = jnp.clip(e - 1, 0, ne - 1)
    i0 = pl.multiple_of((ew // nj) * gi, gi)
    j0 = pl.multiple_of((ew % nj) * gj, gj)

    @pl.when(e == 0)
    def _():
        o_ref[...] = jnp.zeros(o_ref.shape, F32)
        ab_ref[...] = jnp.zeros(ab_ref.shape, F32)
        w8a_ref[...] = jnp.zeros(w8a_ref.shape, w8a_ref.dtype)

    def step(a_next, a_cur, w8_cur, w8_prev):
        u_scale = us_ref[...].reshape(eb, LANES)
        v_scale = vs_ref[...].reshape(eb, LANES)

        def piece(c, carry):
            cols = pl.ds(pl.multiple_of(c * kc, kc), kc)
            part = lax.dot_general(u_ref[:, :, cols].reshape(eb, kc), h_ref[:, cols], NT_DIMS,
                                   preferred_element_type=F32)
            a_next[...] = jnp.where(c == 0, part, a_next[...] + part)
            y = lax.dot_general(w8_prev[...], v_ref[:, :, cols].reshape(eb, kc), TN_DIMS,
                                preferred_element_type=F32)
            o_ref[:, cols] += y
            for sub in range(lanes_per_piece // LANES):
                lsl = pl.ds(pl.multiple_of(c * lanes_per_piece + sub * LANES, LANES), LANES)
                act = _gelu(a_cur[:, lsl] * (u_scale * hs_ref[:, lsl])) * (v_scale * wi_ref[:, lsl])
                c0 = [cnt0_ref[hd, pl.ds(i0, gi), lsl] for hd in range(n_heads)]
                e0 = [e0_ref[hd, pl.ds(i0, gi), lsl] for hd in range(n_heads)]
                for ii in range(gi):
                    rsl = slice(ii * gj, (ii + 1) * gj)
                    g = jnp.zeros((gj, LANES), BF16)
                    for hd in range(n_heads):
                        r1 = rank1_ref[hd, pl.ds(j0, gj), lsl]
                        e1 = e1_ref[hd, pl.ds(j0, gj), lsl]
                        g = g + jnp.where(r1 < c0[hd][ii:ii + 1], e1 * e0[hd][ii:ii + 1], jnp.zeros((), BF16))
                    w8_cur[rsl, lsl] = (g.astype(F32) * act[rsl]).astype(w8_cur.dtype)
            return carry

        lax.fori_loop(0, PEER_CHUNKS, piece, 0)

    pl.when(e % 2 == 0)(lambda: step(aa_ref, ab_ref, w8b_ref, w8a_ref))
    pl.when(e % 2 == 1)(lambda: step(ab_ref, aa_ref, w8a_ref, w8b_ref))

    @pl.when(e == pl.num_programs(1) - 1)
    def _():
        x = x_ref[...] + gate_ref[...] * (o_ref[...] * ws_ref[...])
        if normalize:
            x = x * lax.rsqrt(jnp.mean(x * x, axis=-1, keepdims=True) + NORM_EPS) * fg_ref[...]
        o_ref[...] = x


def _quant_rows_kernel(w_ref, q_ref, s_ref, n_ref):
    w = w_ref[...]
    scale = jnp.maximum(jnp.max(jnp.abs(w), axis=-1, keepdims=True), FP8_TINY) / FP8_MAX
    q_ref[...] = (w / scale).astype(q_ref.dtype)
    s_ref[...] = jnp.broadcast_to(scale, s_ref.shape)
    n_ref[...] = jnp.sqrt(jnp.sum(w * w, axis=-1, keepdims=True))


def quant_rows(w):
    E, D = w.shape
    te = _tile(E, 512)
    return pl.pallas_call(
        _quant_rows_kernel,
        grid=(E // te,),
        in_specs=[pl.BlockSpec((te, D), lambda i: (i, 0))],
        out_specs=[pl.BlockSpec((te, D), lambda i: (i, 0)), pl.BlockSpec((te, LANES), lambda i: (i, 0)),
                   pl.BlockSpec((te, 1), lambda i: (i, 0))],
        out_shape=[jax.ShapeDtypeStruct((E, D), F8), jax.ShapeDtypeStruct((E, LANES), F32),
                   jax.ShapeDtypeStruct((E, 1), F32)],
        compiler_params=_params(("arbitrary",), 32),
        name="quant_rows",
    )(w)


PEER_W_MARGIN = 1.25


def peer_experts(h8, h_scale, h_norm, u, v, tables, x, mod, gate_idx, final_g, normalize):
    S, D = h8.shape
    n_heads = tables[0].shape[0]
    T = _tile(S, 512)
    u8, su, u_norm = quant_rows(u)
    v8, sv, _ = quant_rows(v)
    w_bound = (PEER_W_MARGIN * n_heads) * h_norm * jnp.max(u_norm * sv[:, :1])
    w_scale = jnp.maximum(w_bound, FP8_TINY) / FP8_MAX
    u3 = u8.reshape(N_KEYS, N_KEYS, D)
    v3 = v8.reshape(N_KEYS, N_KEYS, D)
    su3 = su.reshape(N_KEYS, N_KEYS, LANES)
    sv3 = sv.reshape(N_KEYS, N_KEYS, LANES)
    ni, nj = N_KEYS // PEER_GI, N_KEYS // PEER_GJ
    ne = ni * nj
    eb = PEER_GI * PEER_GJ

    def u_idx(t, e):
        eu = jnp.minimum(e, ne - 1)
        return eu // nj, eu % nj, 0

    def w_idx(t, e):
        ew = jnp.clip(e - 1, 0, ne - 1)
        return ew // nj, ew % nj, 0

    def v_idx(t, e):
        ev = jnp.maximum(e - 2, 0)
        return ev // nj, ev % nj, 0

    once = pl.Buffered(1)
    tspec = pl.BlockSpec((n_heads, N_KEYS, T), lambda t, e: (0, 0, t), pipeline_mode=once)
    a_buf = pltpu.VMEM((eb, T), F32)
    w8_buf = pltpu.VMEM((eb, T), F8)
    row = pl.BlockSpec((1, T), lambda t, e: (0, t))
    return pl.pallas_call(
        functools.partial(_peer_expert_kernel, nj=nj, normalize=normalize),
        grid=(S // T, ne + 2),
        in_specs=[pl.BlockSpec((T, D), lambda t, e: (t, 0), pipeline_mode=once),
                  row, row,
                  pl.BlockSpec((PEER_GI, PEER_GJ, D), u_idx),
                  pl.BlockSpec((PEER_GI, PEER_GJ, LANES), w_idx),
                  pl.BlockSpec((PEER_GI, PEER_GJ, D), v_idx),
                  pl.BlockSpec((PEER_GI, PEER_GJ, LANES), w_idx),
                  tspec, tspec, tspec, tspec,
                  pl.BlockSpec((T, D), lambda t, e: (t, 0), pipeline_mode=once),
                  pl.BlockSpec((T, 1), lambda t, e: (t, 0), pipeline_mode=once),
                  pl.BlockSpec((1, D), lambda t, e: (0, gate_idx)),
                  pl.BlockSpec((1, D), lambda t, e: (0, 0))],
        out_specs=pl.BlockSpec((T, D), lambda t, e: (t, 0)),
        out_shape=jax.ShapeDtypeStruct((S, D), F32),
        scratch_shapes=[a_buf, a_buf, w8_buf, w8_buf],
        compiler_params=_params(("arbitrary", "arbitrary"), 56),
        name="peer_experts",
    )(h8, h_scale.reshape(1, S), (1.0 / w_scale).reshape(1, S), u3, su3, v3, sv3, *tables,
      x, w_scale, mod, final_g.reshape(1, D))


def kernel(x, c, positions, norm1_g, norm2_g, w_ada, b_ada, w_in, lam_q1, lam_k1, lam_q2, lam_k2,
           diff_subln_g, dil_out_g, w_out, peer_wq, peer_subkeys, peer_u, peer_v, final_g):
    B, S, D = x.shape
    depth = w_ada.shape[0]
    assert B == 1, "one sequence per call"
    diff_width = D // 2
    diff_heads = diff_width // (2 * HEAD_DIM)
    dil_heads = (D - diff_width) // HEAD_DIM
    xs = x.reshape(S, D)
    tables = rope_tables(positions.reshape(S))
    for l in range(depth):
        lam_init = 0.8 - 0.6 * math.exp(-0.3 * l)
        mod = ada_mod(c, w_ada[l], b_ada[l])
        h = norm_mod(xs, norm1_g[l], mod, 0, 1)
        proj = in_proj(h, w_in[l], tables)
        v_t = v_proj_t(h, w_in[l])
        lam_rows = jnp.stack([lam_q1[l], lam_k1[l], lam_q2[l], lam_k2[l]])
        y_diff = diff_attn(proj, v_t, lam_rows, diff_subln_g[l], diff_heads, lam_init)
        y_dil = dil_attn(proj, v_t, dil_out_g[l], dil_heads, 2 * diff_width, diff_width)
        xs = out_proj(y_diff, y_dil, w_out[l], xs, mod, 2)
        h, h8, h_scale, h_norm = norm_mod_q(xs, norm2_g[l], mod, 3, 4)
        gates = peer_router(h, peer_wq[l], peer_subkeys[l])
        xs = peer_experts(h8, h_scale, h_norm, peer_u[l], peer_v[l], gates, xs, mod, 5, final_g,
                          normalize=(l == depth - 1))
    return xs.reshape(B, S, D)
```

```python
import functools
import math

import jax
import jax.numpy as jnp
from jax import lax
from jax.experimental import pallas as pl
from jax.experimental.pallas import tpu as pltpu

F32 = jnp.float32
BF16 = jnp.bfloat16
F8 = jnp.float8_e4m3fn
FP8_MAX = float(jnp.finfo(F8).max)
FP8_TINY = 1e-30

HEAD_DIM = 128
ROT_DIM = HEAD_DIM // 4
ROT_HALF = ROT_DIM // 2
ROPE_THETA = 500000.0
DILATED_CONFIGS = ((128, 1), (512, 4), (2048, 16))
DIL_BACK = max(w for w, _ in DILATED_CONFIGS)
N_KEYS = 128
PEER_TOPK = 16
NORM_EPS = 1e-6
NEG_INF = -1e30
QK_SCALE = HEAD_DIM ** -0.5 * math.log2(math.e)
LANES = 128
MIB = 1024 * 1024

NT_DIMS = (((1,), (1,)), ((), ()))
TN_DIMS = (((0,), (0,)), ((), ()))


def _params(semantics, vmem_mib):
    return pltpu.CompilerParams(dimension_semantics=semantics, vmem_limit_bytes=vmem_mib * MIB)


def _tile(n, pref):
    t = min(n, pref)
    assert n % t == 0, (n, pref)
    return t


def _ada_kernel(c_ref, w_ref, b_ref, o_ref):
    c = c_ref[...]
    ca = c * jax.nn.sigmoid(c)
    for j in range(o_ref.shape[1] // LANES):
        sl = slice(j * LANES, (j + 1) * LANES)
        o_ref[:, sl] = jnp.sum(w_ref[:, sl] * ca, axis=0, keepdims=True) + b_ref[:, sl]


def ada_mod(c, w_ada, b_ada):
    D, N = w_ada.shape
    tn = _tile(N, 1024)
    cb = jnp.broadcast_to(c.reshape(D, 1), (D, LANES))
    return pl.pallas_call(
        _ada_kernel,
        grid=(N // tn,),
        in_specs=[pl.BlockSpec((D, LANES), lambda j: (0, 0)),
                  pl.BlockSpec((D, tn), lambda j: (0, j)),
                  pl.BlockSpec((1, tn), lambda j: (0, j))],
        out_specs=pl.BlockSpec((1, tn), lambda j: (0, j)),
        out_shape=jax.ShapeDtypeStruct((1, N), F32),
        compiler_params=_params(("arbitrary",), 48),
        name="ada_mod",
    )(cb, w_ada, b_ada.reshape(1, N))


def _rope_kernel(pos_ref, freq_ref, c_ref, s1_ref, s2_ref):
    ang = pos_ref[...].astype(F32) * freq_ref[...]
    lane = lax.broadcasted_iota(jnp.int32, ang.shape, 1)
    cs = jnp.cos(ang)
    sn = jnp.sin(ang)
    c_ref[...] = jnp.where(lane < ROT_DIM, cs, 1.0)
    s1_ref[...] = jnp.where(lane < ROT_HALF, -sn, 0.0)
    s2_ref[...] = jnp.where((lane >= ROT_HALF) & (lane < ROT_DIM), sn, 0.0)


def rope_tables(positions):
    S = positions.shape[0]
    ts = _tile(S, 1024)
    inv_freq = jnp.power(ROPE_THETA, -jnp.arange(0, ROT_DIM, 2, dtype=F32) / ROT_DIM)
    freq = jnp.tile(inv_freq, LANES // ROT_HALF).reshape(1, LANES)
    spec = pl.BlockSpec((ts, LANES), lambda i: (i, 0))
    return pl.pallas_call(
        _rope_kernel,
        grid=(S // ts,),
        in_specs=[pl.BlockSpec((ts, 1), lambda i: (i, 0)), pl.BlockSpec((1, LANES), lambda i: (0, 0))],
        out_specs=[spec, spec, spec],
        out_shape=[jax.ShapeDtypeStruct((S, LANES), F32)] * 3,
        compiler_params=_params(("arbitrary",), 32),
        name="rope_tables",
    )(positions.reshape(S, 1), freq)


def _rms_mod(x, g, scale, shift):
    y = x * lax.rsqrt(jnp.mean(x * x, axis=-1, keepdims=True) + NORM_EPS) * g
    return y * (1 + scale) + shift


def _norm_mod_kernel(x_ref, g_ref, sh_ref, sc_ref, o_ref):
    o_ref[...] = _rms_mod(x_ref[...], g_ref[...], sc_ref[...], sh_ref[...]).astype(o_ref.dtype)


def norm_mod(x, g, mod, shift_idx, scale_idx):
    S, D = x.shape
    tm = _tile(S, 256)
    return pl.pallas_call(
        _norm_mod_kernel,
        grid=(S // tm,),
        in_specs=[pl.BlockSpec((tm, D), lambda i: (i, 0)),
                  pl.BlockSpec((1, D), lambda i: (0, 0)),
                  pl.BlockSpec((1, D), lambda i: (0, shift_idx)),
                  pl.BlockSpec((1, D), lambda i: (0, scale_idx))],
        out_specs=pl.BlockSpec((tm, D), lambda i: (i, 0)),
        out_shape=jax.ShapeDtypeStruct((S, D), BF16),
        compiler_params=_params(("arbitrary",), 32),
        name="norm_mod",
    )(x, g.reshape(1, D), mod, mod)


def _norm_mod_q_kernel(x_ref, g_ref, sh_ref, sc_ref, o_ref, q_ref, s_ref, n_ref):
    y = _rms_mod(x_ref[...], g_ref[...], sc_ref[...], sh_ref[...])
    o_ref[...] = y.astype(o_ref.dtype)
    scale = jnp.maximum(jnp.max(jnp.abs(y), axis=-1, keepdims=True), FP8_TINY) / FP8_MAX
    q_ref[...] = (y / scale).astype(q_ref.dtype)
    s_ref[...] = scale
    n_ref[...] = jnp.sqrt(jnp.sum(y * y, axis=-1, keepdims=True))


def norm_mod_q(x, g, mod, shift_idx, scale_idx):
    S, D = x.shape
    tm = _tile(S, 256)
    blk = pl.BlockSpec((tm, D), lambda i: (i, 0))
    return pl.pallas_call(
        _norm_mod_q_kernel,
        grid=(S // tm,),
        in_specs=[blk,
                  pl.BlockSpec((1, D), lambda i: (0, 0)),
                  pl.BlockSpec((1, D), lambda i: (0, shift_idx)),
                  pl.BlockSpec((1, D), lambda i: (0, scale_idx))],
        out_specs=[blk, blk, pl.BlockSpec((tm, 1), lambda i: (i, 0)), pl.BlockSpec((tm, 1), lambda i: (i, 0))],
        out_shape=[jax.ShapeDtypeStruct((S, D), BF16), jax.ShapeDtypeStruct((S, D), F8),
                   jax.ShapeDtypeStruct((S, 1), F32), jax.ShapeDtypeStruct((S, 1), F32)],
        compiler_params=_params(("arbitrary",), 32),
        name="norm_mod_q",
    )(x, g.reshape(1, D), mod, mod)


def _in_proj_kernel(h_ref, w_ref, c_ref, s1_ref, s2_ref, o_ref, acc_a, acc_b, *, per_sec):
    j = pl.program_id(1)
    tn = o_ref.shape[1]
    section = jnp.maximum(j - 1, 0) // per_sec
    q_scale = jnp.where((section == 0) | (section == 2), QK_SCALE, 1.0).astype(F32)

    @pl.when((pl.program_id(0) == 0) & (j == 0))
    def _():
        acc_b[...] = jnp.zeros(acc_b.shape, F32)

    def step(acc_mm, acc_ep):
        acc_mm[...] = jnp.dot(h_ref[...], w_ref[...].astype(h_ref.dtype), preferred_element_type=F32)
        cs, s1, s2 = c_ref[...], s1_ref[...], s2_ref[...]
        for k in range(tn // HEAD_DIM):
            sl = slice(k * HEAD_DIM, (k + 1) * HEAD_DIM)
            t = acc_ep[:, sl]
            r = (t * cs + pltpu.roll(t, HEAD_DIM - ROT_HALF, 1) * s1 + pltpu.roll(t, ROT_HALF, 1) * s2)
            o_ref[:, sl] = (r * q_scale).astype(o_ref.dtype)

    pl.when(j % 2 == 0)(lambda: step(acc_a, acc_b))
    pl.when(j % 2 == 1)(lambda: step(acc_b, acc_a))


def in_proj(h, w_in, tables):
    S, D = h.shape
    sec_width = w_in.shape[1] // 6
    tm = _tile(S, 1024)
    tn = _tile(sec_width, 512)
    per_sec = sec_width // tn
    nj = 4 * per_sec

    def src(j):
        jm = jnp.minimum(j, nj - 1)
        return jm + jnp.where(jm >= 2 * per_sec, per_sec, 0)

    tspec = pl.BlockSpec((tm, LANES), lambda i, j: (i, 0))
    acc = pltpu.VMEM((tm, tn), F32)
    return pl.pallas_call(
        functools.partial(_in_proj_kernel, per_sec=per_sec),
        grid=(S // tm, nj + 1),
        in_specs=[pl.BlockSpec((tm, D), lambda i, j: (i, 0)),
                  pl.BlockSpec((D, tn), lambda i, j: (0, src(j))),
                  tspec, tspec, tspec],
        out_specs=pl.BlockSpec((tm, tn), lambda i, j: (i, jnp.maximum(j - 1, 0))),
        out_shape=jax.ShapeDtypeStruct((S, 4 * sec_width), BF16),
        scratch_shapes=[acc, acc],
        compiler_params=_params(("arbitrary", "arbitrary"), 48),
        name="in_proj",
    )(h, w_in, *tables)


def _v_proj_t_kernel(w_ref, h_ref, o_ref, wt_ref):
    @pl.when(pl.program_id(1) == 0)
    def _():
        wt_ref[...] = w_ref[...].astype(wt_ref.dtype).T

    o_ref[...] = lax.dot_general(wt_ref[...], h_ref[...], NT_DIMS, preferred_element_type=F32).astype(o_ref.dtype)


def v_proj_t(h, w_in):
    S, D = h.shape
    sec_width = w_in.shape[1] // 6
    tn = _tile(sec_width, 512)
    ts = _tile(S, 1024)
    per_sec = sec_width // tn
    src = lambda j: j + jnp.where(j >= per_sec, 4 * per_sec, 2 * per_sec)
    return pl.pallas_call(
        _v_proj_t_kernel,
        grid=(2 * per_sec, S // ts),
        in_specs=[pl.BlockSpec((D, tn), lambda j, i: (0, src(j))), pl.BlockSpec((ts, D), lambda j, i: (i, 0))],
        out_specs=pl.BlockSpec((tn, ts), lambda j, i: (j, i)),
        out_shape=jax.ShapeDtypeStruct((2 * sec_width, S), BF16),
        scratch_shapes=[pltpu.VMEM((tn, D), BF16)],
        compiler_params=_params(("arbitrary", "arbitrary"), 48),
        name="v_proj_t",
    )(w_in, h)


def _diff_attn_kernel(lam_ref, q_ref, k_ref, vt_ref, g_ref, o_ref,
                      m1, l1, a1, m2, l2, a2, sa_ref, sb_ref, *, tq, lam_init):
    i = pl.program_id(1)
    stats = ((m1, l1, a1), (m2, l2, a2))
    for m, l, a in stats:
        m[...] = jnp.full(m.shape, -jnp.inf, F32)
        l[...] = jnp.zeros(l.shape, F32)
        a[...] = jnp.zeros(a.shape, F32)

    def scores(kv, s_ref):
        k0 = pl.multiple_of(kv * tq, tq)
        for c in range(2):
            sl = slice(c * HEAD_DIM, (c + 1) * HEAD_DIM)
            s_ref[c] = lax.dot_general(k_ref[pl.ds(k0, tq), sl], q_ref[:, sl], NT_DIMS,
                                       preferred_element_type=F32)

    def softmax_pv(kv, s_ref, masked):
        k0 = pl.multiple_of(kv * tq, tq)
        vt = vt_ref[:, pl.ds(k0, tq)]
        for c, (m, l, a) in enumerate(stats):
            s = s_ref[c]
            if masked:
                keep = (lax.broadcasted_iota(jnp.int32, (tq, tq), 0) <= lax.broadcasted_iota(jnp.int32, (tq, tq), 1))
                s = jnp.where(keep, s, NEG_INF)
            m_prev = m[...]
            m_new = jnp.maximum(m_prev, jnp.max(s, axis=0, keepdims=True))
            alpha = jnp.exp2(m_prev - m_new)
            p = jnp.exp2(s - m_new)
            l[...] = alpha * l[...] + jnp.sum(p, axis=0, keepdims=True)
            a[...] = alpha * a[...] + jnp.dot(vt, p.astype(vt.dtype), preferred_element_type=F32)
            m[...] = m_new

    scores(0, sa_ref)

    def pair(n, carry):
        scores(2 * n + 1, sb_ref)
        softmax_pv(2 * n, sa_ref, False)
        scores(2 * n + 2, sa_ref)
        softmax_pv(2 * n + 1, sb_ref, False)
        return carry

    lax.fori_loop(0, i // 2, pair, 0)

    @pl.when(i % 2 == 0)
    def _():
        softmax_pv(i, sa_ref, True)

    @pl.when(i % 2 == 1)
    def _():
        scores(i, sb_ref)
        softmax_pv(i - 1, sa_ref, False)
        softmax_pv(i, sb_ref, True)

    lp = lam_ref[...]
    lam = (jnp.exp(jnp.sum(lp[0:1] * lp[1:2], axis=-1, keepdims=True))
           - jnp.exp(jnp.sum(lp[2:3] * lp[3:4], axis=-1, keepdims=True)) + lam_init)
    o = (a1[...] / l1[...] - lam * (a2[...] / l2[...])).T
    y = o * lax.rsqrt(jnp.mean(o * o, axis=-1, keepdims=True) + NORM_EPS) * g_ref[...]
    o_ref[...] = (y * (1 - lam_init)).astype(o_ref.dtype)


def diff_attn(proj, v_t, lam_rows, subln_g, n_heads, lam_init):
    S = proj.shape[0]
    tq = _tile(S, 512)
    hw = 2 * HEAD_DIM
    return pl.pallas_call(
        functools.partial(_diff_attn_kernel, tq=tq, lam_init=lam_init),
        grid=(n_heads, S // tq),
        in_specs=[pl.BlockSpec((4, HEAD_DIM), lambda h, i: (0, 0)),
                  pl.BlockSpec((tq, hw), lambda h, i: (i, h)),
                  pl.BlockSpec((S, hw), lambda h, i: (0, n_heads + h)),
                  pl.BlockSpec((hw, S), lambda h, i: (h, 0)),
                  pl.BlockSpec((1, hw), lambda h, i: (0, 0))],
        out_specs=pl.BlockSpec((tq, hw), lambda h, i: (i, h)),
        out_shape=jax.ShapeDtypeStruct((S, n_heads * hw), BF16),
        scratch_shapes=[pltpu.VMEM((1, tq), F32), pltpu.VMEM((1, tq), F32), pltpu.VMEM((hw, tq), F32),
                        pltpu.VMEM((1, tq), F32), pltpu.VMEM((1, tq), F32), pltpu.VMEM((hw, tq), F32),
                        pltpu.VMEM((2, tq, tq), F32), pltpu.VMEM((2, tq, tq), F32)],
        compiler_params=_params(("arbitrary", "arbitrary"), 48),
        name="diff_attn",
    )(lam_rows, proj, proj, v_t, subln_g.reshape(1, hw))


def _dil_bias(delta):
    count = jnp.zeros(delta.shape, F32)
    for window, dilation in DILATED_CONFIGS:
        ok = (delta >= 0) & (delta <= window) & ((delta & (dilation - 1)) == 0)
        count = count + jnp.where(ok, 1.0, 0.0)
    return jnp.where(count > 0, jnp.log2(jnp.maximum(count, 1.0)), NEG_INF)


DIL_HEADS_PER_STEP = 2


def _dil_attn_kernel(q_ref, k_ref, vt_ref, g_ref, o_ref, bias_ref, sa_ref, sb_ref, *, tq, win):
    i = pl.program_id(1)
    n_blocks = pl.num_programs(1)
    heads = o_ref.shape[1] // HEAD_DIM

    @pl.when((pl.program_id(0) == 0) & (i == 0))
    def _():
        kj = lax.broadcasted_iota(jnp.int32, bias_ref.shape, 0)
        qi = lax.broadcasted_iota(jnp.int32, bias_ref.shape, 1)
        bias_ref[...] = _dil_bias(qi + DIL_BACK - kj)

    def window(j):
        start = pl.multiple_of(jnp.maximum(j * tq - DIL_BACK, 0), tq)
        row0 = pl.multiple_of(jnp.maximum(DIL_BACK - j * tq, 0), tq)
        return start, row0

    def scores(j, s_ref):
        start, row0 = window(j)
        bias = bias_ref[pl.ds(row0, win), :]
        q0 = pl.multiple_of(j * tq, tq)
        for hd in range(heads):
            sl = slice(hd * HEAD_DIM, (hd + 1) * HEAD_DIM)
            s_ref[hd] = lax.dot_general(k_ref[pl.ds(start, win), sl], q_ref[pl.ds(q0, tq), sl], NT_DIMS,
                                        preferred_element_type=F32) + bias

    def softmax_pv(j, s_ref):
        start, _ = window(j)
        for hd in range(heads):
            sl = slice(hd * HEAD_DIM, (hd + 1) * HEAD_DIM)
            s = s_ref[hd]
            p = jnp.exp2(s - jnp.max(s, axis=0, keepdims=True))
            vtw = vt_ref[sl, pl.ds(start, win)]
            o_t = jnp.dot(vtw, p.astype(vtw.dtype), preferred_element_type=F32) / jnp.sum(p, axis=0, keepdims=True)
            o = o_t.T
            y = o * lax.rsqrt(jnp.mean(o * o, axis=-1, keepdims=True) + NORM_EPS) * g_ref[:, sl]
            o_ref[:, sl] = y.astype(o_ref.dtype)

    @pl.when(i == 0)
    def _():
        scores(0, sa_ref)

    nxt = jnp.minimum(i + 1, n_blocks - 1)

    @pl.when(i % 2 == 0)
    def _():
        scores(nxt, sb_ref)
        softmax_pv(i, sa_ref)

    @pl.when(i % 2 == 1)
    def _():
        scores(nxt, sa_ref)
        softmax_pv(i, sb_ref)


def dil_attn(proj, v_t, out_g, n_heads, col0, row0):
    S = proj.shape[0]
    for window, dilation in DILATED_CONFIGS:
        assert window % dilation == 0 and dilation & (dilation - 1) == 0
    tq = _tile(S, 256)
    win = DIL_BACK + tq
    hp = DIL_HEADS_PER_STEP
    hw = hp * HEAD_DIM
    assert S >= win and DIL_BACK % tq == 0 and win % LANES == 0
    assert n_heads % hp == 0 and col0 % hw == 0 and row0 % hw == 0
    c0 = col0 // hw
    r0 = row0 // hw
    ng = n_heads // hp
    return pl.pallas_call(
        functools.partial(_dil_attn_kernel, tq=tq, win=win),
        grid=(ng, S // tq),
        in_specs=[pl.BlockSpec((S, hw), lambda h, i: (0, c0 + h)),
                  pl.BlockSpec((S, hw), lambda h, i: (0, c0 + ng + h)),
                  pl.BlockSpec((hw, S), lambda h, i: (r0 + h, 0)),
                  pl.BlockSpec((1, hw), lambda h, i: (0, h))],
        out_specs=pl.BlockSpec((tq, hw), lambda h, i: (i, h)),
        out_shape=jax.ShapeDtypeStruct((S, n_heads * HEAD_DIM), BF16),
        scratch_shapes=[pltpu.VMEM((win + DIL_BACK, tq), F32),
                        pltpu.VMEM((hp, win, tq), F32), pltpu.VMEM((hp, win, tq), F32)],
        compiler_params=_params(("arbitrary", "arbitrary"), 56),
        name="dil_attn",
    )(proj, proj, v_t, out_g.reshape(1, n_heads * HEAD_DIM))


def _out_proj_kernel(ya_ref, yb_ref, wa_ref, wb_ref, x_ref, gate_ref, x1_ref):
    acc = (jnp.dot(ya_ref[...], wa_ref[...].astype(ya_ref.dtype), preferred_element_type=F32)
           + jnp.dot(yb_ref[...], wb_ref[...].astype(yb_ref.dtype), preferred_element_type=F32))
    x1_ref[...] = x_ref[...] + gate_ref[...] * acc


def out_proj(ya, yb, w_out, x, mod, gate_idx):
    S, D = x.shape
    kh = ya.shape[1]
    assert yb.shape[1] == kh and w_out.shape[0] == 2 * kh
    tm = _tile(S, 1024)
    tn = _tile(D, 512)
    nj = D // tn
    return pl.pallas_call(
        _out_proj_kernel,
        grid=(S // tm, nj),
        in_specs=[pl.BlockSpec((tm, kh), lambda i, j: (i, 0)),
                  pl.BlockSpec((tm, kh), lambda i, j: (i, 0)),
                  pl.BlockSpec((kh, tn), lambda i, j: (0, j)),
                  pl.BlockSpec((kh, tn), lambda i, j: (1, j)),
                  pl.BlockSpec((tm, tn), lambda i, j: (i, j)),
                  pl.BlockSpec((1, tn), lambda i, j: (0, gate_idx * nj + j))],
        out_specs=pl.BlockSpec((tm, tn), lambda i, j: (i, j)),
        out_shape=jax.ShapeDtypeStruct((S, D), F32),
        compiler_params=_params(("arbitrary", "arbitrary"), 48),
        name="out_proj",
    )(ya, yb, w_out, w_out, x, mod)


def _top16(x, tie_safe):
    rows = lax.broadcasted_iota(jnp.int32, x.shape, 0)
    rank = jnp.full(x.shape, PEER_TOPK, jnp.int32)
    vals = []
    for r in range(PEER_TOPK):
        m = jnp.max(x, axis=0, keepdims=True)
        if tie_safe:
            hit = rows == jnp.min(jnp.where(x == m, rows, x.shape[0]), axis=0, keepdims=True)
        else:
            hit = x == m
        rank = jnp.where(hit, r, rank)
        x = jnp.where(hit, -jnp.inf, x)
        vals.append(m)
    return jnp.concatenate(vals, axis=0), rank


def _route(s, tie_safe):
    a, rank0 = _top16(s[0], tie_safe)
    b, rank1 = _top16(s[1], tie_safe)
    n_q = [PEER_TOPK // (p + 1) for p in range(PEER_TOPK)]
    pad = -sum(n_q) % 8
    cand = jnp.concatenate([a[p:p + 1] + b[:n_q[p]] for p in range(PEER_TOPK)]
                           + [jnp.full((pad, a.shape[1]), -jnp.inf, F32)], axis=0)
    fin, sel_rank = _top16(cand, tie_safe)
    sel = jnp.where(sel_rank < PEER_TOPK, 1.0, 0.0)
    starts = [sum(n_q[:p]) for p in range(PEER_TOPK)]
    cnt = jnp.concatenate([jnp.sum(sel[starts[p]:starts[p] + n_q[p]], axis=0, keepdims=True)
                           for p in range(PEER_TOPK)], axis=0)
    cnt0 = jnp.zeros(s[0].shape, F32)
    for p in range(PEER_TOPK):
        cnt0 = jnp.where(rank0 == p, cnt[p:p + 1], cnt0)
    z = jnp.sum(jnp.exp(fin - fin[0:1]), axis=0, keepdims=True)
    removed = sum(jnp.sum(jnp.where(r < PEER_TOPK, 1.0, 0.0), axis=0, keepdims=True)
                  for r in (rank0, rank1, sel_rank))
    return rank1.astype(F32), cnt0, jnp.exp(s[0] - a[0:1]), jnp.exp(s[1] - b[0:1]) / z, removed


def _peer_router_kernel(h_ref, wq_ref, sk_ref, rank1_ref, cnt0_ref, e0_ref, e1_ref):
    q = jnp.dot(h_ref[...], wq_ref[...].astype(h_ref.dtype), preferred_element_type=F32)
    half = q.shape[1] // 2
    s = [lax.dot_general(sk_ref[0, p], q[:, p * half:(p + 1) * half], NT_DIMS,
                         precision=lax.Precision.HIGHEST, preferred_element_type=F32)
         for p in range(2)]

    def emit(tie_safe):
        *outs, removed = _route(s, tie_safe)
        for ref, val in zip((rank1_ref, cnt0_ref, e0_ref, e1_ref), outs):
            ref[0] = val.astype(ref.dtype)
        return removed

    removed = emit(False)
    tie = jnp.max(jnp.abs(removed - 3.0 * PEER_TOPK)) > 0.0

    @pl.when(tie)
    def _():
        emit(True)


def peer_router(h, w_q, subkeys):
    S, D = h.shape
    n_heads = subkeys.shape[0]
    qd = w_q.shape[1] // n_heads
    T = _tile(S, 512)
    ospec = pl.BlockSpec((1, N_KEYS, T), lambda t, hd: (hd, 0, t))
    return pl.pallas_call(
        _peer_router_kernel,
        grid=(S // T, n_heads),
        in_specs=[pl.BlockSpec((T, D), lambda t, hd: (t, 0)),
                  pl.BlockSpec((D, qd), lambda t, hd: (0, hd)),
                  pl.BlockSpec((1, 2, N_KEYS, qd // 2), lambda t, hd: (hd, 0, 0, 0))],
        out_specs=[ospec] * 4,
        out_shape=[jax.ShapeDtypeStruct((n_heads, N_KEYS, S), BF16)] * 4,
        compiler_params=_params(("arbitrary", "arbitrary"), 48),
        name="peer_router",
    )(h, w_q, subkeys)


def _gelu(x):
    return 0.5 * x * (1.0 + lax.erf(x * (0.5 ** 0.5)))


PEER_GI = 8
PEER_GJ = 64
PEER_CHUNKS = 2


def _peer_expert_kernel(h_ref, hs_ref, wi_ref, u_ref, us_ref, v_ref, vs_ref, rank1_ref, cnt0_ref, e0_ref, e1_ref,
                        x_ref, ws_ref, gate_ref, fg_ref, o_ref, aa_ref, ab_ref, w8a_ref, w8b_ref, *, nj, normalize):
    e = pl.program_id(1)
    ne = pl.num_programs(1) - 2
    gi, gj, D = u_ref.shape
    n_heads, _, T = rank1_ref.shape
    eb = gi * gj
    kc = D // PEER_CHUNKS
    lanes_per_piece = T // PEER_CHUNKS
    assert lanes_per_piece % LANES == 0
    ew = jnp.clip(e - 1, 0, ne - 1)
    i0 = pl.multiple_of((ew // nj) * gi, gi)
    j0 = pl.multiple_of((ew % nj) * gj, gj)

    @pl.when(e == 0)
    def _():
        o_ref[...] = jnp.zeros(o_ref.shape, F32)
        ab_ref[...] = jnp.zeros(ab_ref.shape, F32)
        w8a_ref[...] = jnp.zeros(w8a_ref.shape, w8a_ref.dtype)

    def step(a_next, a_cur, w8_cur, w8_prev):
        u_scale = us_ref[...].reshape(eb, LANES)
        v_scale = vs_ref[...].reshape(eb, LANES)

        def piece(c, carry):
            cols = pl.ds(pl.multiple_of(c * kc, kc), kc)
            part = lax.dot_general(u_ref[:, :, cols].reshape(eb, kc), h_ref[:, cols], NT_DIMS,
                                   preferred_element_type=F32)
            a_next[...] = jnp.where(c == 0, part, a_next[...] + part)
            y = lax.dot_general(w8_prev[...], v_ref[:, :, cols].reshape(eb, kc), TN_DIMS,
                                preferred_element_type=F32)
            o_ref[:, cols] += y
            for sub in range(lanes_per_piece // LANES):
                lsl = pl.ds(pl.multiple_of(c * lanes_per_piece + sub * LANES, LANES), LANES)
                act = _gelu(a_cur[:, lsl] * (u_scale * hs_ref[:, lsl])) * (v_scale * wi_ref[:, lsl])
                c0 = [cnt0_ref[hd, pl.ds(i0, gi), lsl] for hd in range(n_heads)]
                e0 = [e0_ref[hd, pl.ds(i0, gi), lsl] for hd in range(n_heads)]
                for ii in range(gi):
                    rsl = slice(ii * gj, (ii + 1) * gj)
                    g = jnp.zeros((gj, LANES), BF16)
                    for hd in range(n_heads):
                        r1 = rank1_ref[hd, pl.ds(j0, gj), lsl]
                        e1 = e1_ref[hd, pl.ds(j0, gj), lsl]
                        g = g + jnp.where(r1 < c0[hd][ii:ii + 1], e1 * e0[hd][ii:ii + 1], jnp.zeros((), BF16))
                    w8_cur[rsl, lsl] = (g.astype(F32) * act[rsl]).astype(w8_cur.dtype)
            return carry

        lax.fori_loop(0, PEER_CHUNKS, piece, 0)

    pl.when(e % 2 == 0)(lambda: step(aa_ref, ab_ref, w8b_ref, w8a_ref))
    pl.when(e % 2 == 1)(lambda: step(ab_ref, aa_ref, w8a_ref, w8b_ref))

    @pl.when(e == pl.num_programs(1) - 1)
    def _():
        x = x_ref[...] + gate_ref[...] * (o_ref[...] * ws_ref[...])
        if normalize:
            x = x * lax.rsqrt(jnp.mean(x * x, axis=-1, keepdims=True) + NORM_EPS) * fg_ref[...]
        o_ref[...] = x


def _quant_rows_kernel(w_ref, q_ref, s_ref, n_ref):
    w = w_ref[...]
    scale = jnp.maximum(jnp.max(jnp.abs(w), axis=-1, keepdims=True), FP8_TINY) / FP8_MAX
    q_ref[...] = (w / scale).astype(q_ref.dtype)
    s_ref[...] = jnp.broadcast_to(scale, s_ref.shape)
    n_ref[...] = jnp.sqrt(jnp.sum(w * w, axis=-1, keepdims=True))


def quant_rows(w):
    E, D = w.shape
    te = _tile(E, 512)
    return pl.pallas_call(
        _quant_rows_kernel,
        grid=(E // te,),
        in_specs=[pl.BlockSpec((te, D), lambda i: (i, 0))],
        out_specs=[pl.BlockSpec((te, D), lambda i: (i, 0)), pl.BlockSpec((te, LANES), lambda i: (i, 0)),
                   pl.BlockSpec((te, 1), lambda i: (i, 0))],
        out_shape=[jax.ShapeDtypeStruct((E, D), F8), jax.ShapeDtypeStruct((E, LANES), F32),
                   jax.ShapeDtypeStruct((E, 1), F32)],
        compiler_params=_params(("arbitrary",), 32),
        name="quant_rows",
    )(w)


PEER_W_MARGIN = 1.25


def peer_experts(h8, h_scale, h_norm, u, v, tables, x, mod, gate_idx, final_g, normalize):
    S, D = h8.shape
    n_heads = tables[0].shape[0]
    T = _tile(S, 512)
    u8, su, u_norm = quant_rows(u)
    v8, sv, _ = quant_rows(v)
    w_bound = (PEER_W_MARGIN * n_heads) * h_norm * jnp.max(u_norm * sv[:, :1])
    w_scale = jnp.maximum(w_bound, FP8_TINY) / FP8_MAX
    u3 = u8.reshape(N_KEYS, N_KEYS, D)
    v3 = v8.reshape(N_KEYS, N_KEYS, D)
    su3 = su.reshape(N_KEYS, N_KEYS, LANES)
    sv3 = sv.reshape(N_KEYS, N_KEYS, LANES)
    ni, nj = N_KEYS // PEER_GI, N_KEYS // PEER_GJ
    ne = ni * nj
    eb = PEER_GI * PEER_GJ

    def u_idx(t, e):
        eu = jnp.minimum(e, ne - 1)
        return eu // nj, eu % nj, 0

    def w_idx(t, e):
        ew = jnp.clip(e - 1, 0, ne - 1)
        return ew // nj, ew % nj, 0

    def v_idx(t, e):
        ev = jnp.maximum(e - 2, 0)
        return ev // nj, ev % nj, 0

    once = pl.Buffered(1)
    tspec = pl.BlockSpec((n_heads, N_KEYS, T), lambda t, e: (0, 0, t), pipeline_mode=once)
    a_buf = pltpu.VMEM((eb, T), F32)
    w8_buf = pltpu.VMEM((eb, T), F8)
    row = pl.BlockSpec((1, T), lambda t, e: (0, t))
    return pl.pallas_call(
        functools.partial(_peer_expert_kernel, nj=nj, normalize=normalize),
        grid=(S // T, ne + 2),
        in_specs=[pl.BlockSpec((T, D), lambda t, e: (t, 0), pipeline_mode=once),
                  row, row,
                  pl.BlockSpec((PEER_GI, PEER_GJ, D), u_idx),
                  pl.BlockSpec((PEER_GI, PEER_GJ, LANES), w_idx),
                  pl.BlockSpec((PEER_GI, PEER_GJ, D), v_idx),
                  pl.BlockSpec((PEER_GI, PEER_GJ, LANES), w_idx),
                  tspec, tspec, tspec, tspec,
                  pl.BlockSpec((T, D), lambda t, e: (t, 0), pipeline_mode=once),
                  pl.BlockSpec((T, 1), lambda t, e: (t, 0), pipeline_mode=once),
                  pl.BlockSpec((1, D), lambda t, e: (0, gate_idx)),
                  pl.BlockSpec((1, D), lambda t, e: (0, 0))],
        out_specs=pl.BlockSpec((T, D), lambda t, e: (t, 0)),
        out_shape=jax.ShapeDtypeStruct((S, D), F32),
        scratch_shapes=[a_buf, a_buf, w8_buf, w8_buf],
        compiler_params=_params(("arbitrary", "arbitrary"), 56),
        name="peer_experts",
    )(h8, h_scale.reshape(1, S), (1.0 / w_scale).reshape(1, S), u3, su3, v3, sv3, *tables,
      x, w_scale, mod, final_g.reshape(1, D))


def kernel(x, c, positions, norm1_g, norm2_g, w_ada, b_ada, w_in, lam_q1, lam_k1, lam_q2, lam_k2,
           diff_subln_g, dil_out_g, w_out, peer_wq, peer_subkeys, peer_u, peer_v, final_g):
    B, S, D = x.shape
    depth = w_ada.shape[0]
    assert B == 1, "one sequence per call"
    diff_width = D // 2
    diff_heads = diff_width // (2 * HEAD_DIM)
    dil_heads = (D - diff_width) // HEAD_DIM
    xs = x.reshape(S, D)
    tables = rope_tables(positions.reshape(S))
    for l in range(depth):
        lam_init = 0.8 - 0.6 * math.exp(-0.3 * l)
        mod = ada_mod(c, w_ada[l], b_ada[l])
        h = norm_mod(xs, norm1_g[l], mod, 0, 1)
        proj = in_proj(h, w_in[l], tables)
        v_t = v_proj_t(h, w_in[l])
        lam_rows = jnp.stack([lam_q1[l], lam_k1[l], lam_q2[l], lam_k2[l]])
        y_diff = diff_attn(proj, v_t, lam_rows, diff_subln_g[l], diff_heads, lam_init)
        y_dil = dil_attn(proj, v_t, dil_out_g[l], dil_heads, 2 * diff_width, diff_width)
        xs = out_proj(y_diff, y_dil, w_out[l], xs, mod, 2)
        h, h8, h_scale, h_norm = norm_mod_q(xs, norm2_g[l], mod, 3, 4)
        gates = peer_router(h, peer_wq[l], peer_subkeys[l])
        xs = peer_experts(h8, h_scale, h_norm, peer_u[l], peer_v[l], gates, xs, mod, 5, final_g,
                          normalize=(l == depth - 1))
    return xs.reshape(B, S, D)
```

```python
import functools
import math

import jax
import jax.numpy as jnp
from jax import lax
from jax.experimental import pallas as pl
from jax.experimental.pallas import tpu as pltpu

F32 = jnp.float32
BF16 = jnp.bfloat16
F8 = jnp.float8_e4m3fn
FP8_MAX = float(jnp.finfo(F8).max)
FP8_TINY = 1e-30

HEAD_DIM = 128
ROT_DIM = HEAD_DIM // 4
ROT_HALF = ROT_DIM // 2
ROPE_THETA = 500000.0
DILATED_CONFIGS = ((128, 1), (512, 4), (2048, 16))
DIL_BACK = max(w for w, _ in DILATED_CONFIGS)
N_KEYS = 128
PEER_TOPK = 16
NORM_EPS = 1e-6
NEG_INF = -1e30
QK_SCALE = HEAD_DIM ** -0.5 * math.log2(math.e)
LANES = 128
MIB = 1024 * 1024

NT_DIMS = (((1,), (1,)), ((), ()))
TN_DIMS = (((0,), (0,)), ((), ()))


def _params(semantics, vmem_mib):
    return pltpu.CompilerParams(dimension_semantics=semantics, vmem_limit_bytes=vmem_mib * MIB)


def _tile(n, pref):
    t = min(n, pref)
    assert n % t == 0, (n, pref)
    return t


def _ada_kernel(c_ref, w_ref, b_ref, o_ref):
    c = c_ref[...]
    ca = c * jax.nn.sigmoid(c)
    for j in range(o_ref.shape[1] // LANES):
        sl = slice(j * LANES, (j + 1) * LANES)
        o_ref[:, sl] = jnp.sum(w_ref[:, sl] * ca, axis=0, keepdims=True) + b_ref[:, sl]


def ada_mod(c, w_ada, b_ada):
    D, N = w_ada.shape
    tn = _tile(N, 1024)
    cb = jnp.broadcast_to(c.reshape(D, 1), (D, LANES))
    return pl.pallas_call(
        _ada_kernel,
        grid=(N // tn,),
        in_specs=[pl.BlockSpec((D, LANES), lambda j: (0, 0)),
                  pl.BlockSpec((D, tn), lambda j: (0, j)),
                  pl.BlockSpec((1, tn), lambda j: (0, j))],
        out_specs=pl.BlockSpec((1, tn), lambda j: (0, j)),
        out_shape=jax.ShapeDtypeStruct((1, N), F32),
        compiler_params=_params(("arbitrary",), 48),
        name="ada_mod",
    )(cb, w_ada, b_ada.reshape(1, N))


def _rope_kernel(pos_ref, freq_ref, c_ref, s1_ref, s2_ref):
    ang = pos_ref[...].astype(F32) * freq_ref[...]
    lane = lax.broadcasted_iota(jnp.int32, ang.shape, 1)
    cs = jnp.cos(ang)
    sn = jnp.sin(ang)
    c_ref[...] = jnp.where(lane < ROT_DIM, cs, 1.0)
    s1_ref[...] = jnp.where(lane < ROT_HALF, -sn, 0.0)
    s2_ref[...] = jnp.where((lane >= ROT_HALF) & (lane < ROT_DIM), sn, 0.0)


def rope_tables(positions):
    S = positions.shape[0]
    ts = _tile(S, 1024)
    inv_freq = jnp.power(ROPE_THETA, -jnp.arange(0, ROT_DIM, 2, dtype=F32) / ROT_DIM)
    freq = jnp.tile(inv_freq, LANES // ROT_HALF).reshape(1, LANES)
    spec = pl.BlockSpec((ts, LANES), lambda i: (i, 0))
    return pl.pallas_call(
        _rope_kernel,
        grid=(S // ts,),
        in_specs=[pl.BlockSpec((ts, 1), lambda i: (i, 0)), pl.BlockSpec((1, LANES), lambda i: (0, 0))],
        out_specs=[spec, spec, spec],
        out_shape=[jax.ShapeDtypeStruct((S, LANES), F32)] * 3,
        compiler_params=_params(("arbitrary",), 32),
        name="rope_tables",
    )(positions.reshape(S, 1), freq)


def _rms_mod(x, g, scale, shift):
    y = x * lax.rsqrt(jnp.mean(x * x, axis=-1, keepdims=True) + NORM_EPS) * g
    return y * (1 + scale) + shift


def _norm_mod_kernel(x_ref, g_ref, sh_ref, sc_ref, o_ref):
    o_ref[...] = _rms_mod(x_ref[...], g_ref[...], sc_ref[...], sh_ref[...]).astype(o_ref.dtype)


def norm_mod(x, g, mod, shift_idx, scale_idx):
    S, D = x.shape
    tm = _tile(S, 256)
    return pl.pallas_call(
        _norm_mod_kernel,
        grid=(S // tm,),
        in_specs=[pl.BlockSpec((tm, D), lambda i: (i, 0)),
                  pl.BlockSpec((1, D), lambda i: (0, 0)),
                  pl.BlockSpec((1, D), lambda i: (0, shift_idx)),
                  pl.BlockSpec((1, D), lambda i: (0, scale_idx))],
        out_specs=pl.BlockSpec((tm, D), lambda i: (i, 0)),
        out_shape=jax.ShapeDtypeStruct((S, D), BF16),
        compiler_params=_params(("arbitrary",), 32),
        name="norm_mod",
    )(x, g.reshape(1, D), mod, mod)


def _norm_mod_q_kernel(x_ref, g_ref, sh_ref, sc_ref, o_ref, q_ref, s_ref, n_ref):
    y = _rms_mod(x_ref[...], g_ref[...], sc_ref[...], sh_ref[...])
    o_ref[...] = y.astype(o_ref.dtype)
    scale = jnp.maximum(jnp.max(jnp.abs(y), axis=-1, keepdims=True), FP8_TINY) / FP8_MAX
    q_ref[...] = (y / scale).astype(q_ref.dtype)
    s_ref[...] = scale
    n_ref[...] = jnp.sqrt(jnp.sum(y * y, axis=-1, keepdims=True))


def norm_mod_q(x, g, mod, shift_idx, scale_idx):
    S, D = x.shape
    tm = _tile(S, 256)
    blk = pl.BlockSpec((tm, D), lambda i: (i, 0))
    return pl.pallas_call(
        _norm_mod_q_kernel,
        grid=(S // tm,),
        in_specs=[blk,
                  pl.BlockSpec((1, D), lambda i: (0, 0)),
                  pl.BlockSpec((1, D), lambda i: (0, shift_idx)),
                  pl.BlockSpec((1, D), lambda i: (0, scale_idx))],
        out_specs=[blk, blk, pl.BlockSpec((tm, 1), lambda i: (i, 0)), pl.BlockSpec((tm, 1), lambda i: (i, 0))],
        out_shape=[jax.ShapeDtypeStruct((S, D), BF16), jax.ShapeDtypeStruct((S, D), F8),
                   jax.ShapeDtypeStruct((S, 1), F32), jax.ShapeDtypeStruct((S, 1), F32)],
        compiler_params=_params(("arbitrary",), 32),
        name="norm_mod_q",
    )(x, g.reshape(1, D), mod, mod)


def _quant_block(w_ref, q_ref, s_ref, n_ref):
    w = w_ref[...]
    scale = jnp.maximum(jnp.max(jnp.abs(w), axis=-1, keepdims=True), FP8_TINY) / FP8_MAX
    q_ref[...] = (w / scale).astype(q_ref.dtype)
    s_ref[...] = jnp.broadcast_to(scale, s_ref.shape)
    n_ref[...] = jnp.sqrt(jnp.sum(w * w, axis=-1, keepdims=True))


def _quant_specs(w, n_steps, step_of):
    E, D = w.shape
    rows = 8
    while rows * n_steps < E:
        rows *= 2
    nblk = E // rows
    assert E % rows == 0
    idx = lambda *g: (jnp.minimum(step_of(*g), nblk - 1), 0)
    in_spec = pl.BlockSpec((rows, D), idx)
    out_specs = [pl.BlockSpec((rows, D), idx), pl.BlockSpec((rows, LANES), idx), pl.BlockSpec((rows, 1), idx)]
    out_shapes = [jax.ShapeDtypeStruct((E, D), F8), jax.ShapeDtypeStruct((E, LANES), F32),
                  jax.ShapeDtypeStruct((E, 1), F32)]
    return in_spec, out_specs, out_shapes


def _in_proj_kernel(h_ref, w_ref, c_ref, s1_ref, s2_ref, p_ref, o_ref, pq_ref, ps_ref, pn_ref, acc_a, acc_b,
                    *, per_sec):
    j = pl.program_id(1)
    tn = o_ref.shape[1]
    section = jnp.maximum(j - 1, 0) // per_sec
    q_scale = jnp.where((section == 0) | (section == 2), QK_SCALE, 1.0).astype(F32)

    @pl.when((pl.program_id(0) == 0) & (j == 0))
    def _():
        acc_b[...] = jnp.zeros(acc_b.shape, F32)

    def step(acc_mm, acc_ep):
        acc_mm[...] = jnp.dot(h_ref[...], w_ref[...].astype(h_ref.dtype), preferred_element_type=F32)
        _quant_block(p_ref, pq_ref, ps_ref, pn_ref)
        cs, s1, s2 = c_ref[...], s1_ref[...], s2_ref[...]
        for k in range(tn // HEAD_DIM):
            sl = slice(k * HEAD_DIM, (k + 1) * HEAD_DIM)
            t = acc_ep[:, sl]
            r = (t * cs + pltpu.roll(t, HEAD_DIM - ROT_HALF, 1) * s1 + pltpu.roll(t, ROT_HALF, 1) * s2)
            o_ref[:, sl] = (r * q_scale).astype(o_ref.dtype)

    pl.when(j % 2 == 0)(lambda: step(acc_a, acc_b))
    pl.when(j % 2 == 1)(lambda: step(acc_b, acc_a))


def in_proj(h, w_in, tables, p):
    S, D = h.shape
    sec_width = w_in.shape[1] // 6
    tm = _tile(S, 1024)
    tn = _tile(sec_width, 512)
    per_sec = sec_width // tn
    nj = 4 * per_sec

    def src(j):
        jm = jnp.minimum(j, nj - 1)
        return jm + jnp.where(jm >= 2 * per_sec, per_sec, 0)

    tspec = pl.BlockSpec((tm, LANES), lambda i, j: (i, 0))
    acc = pltpu.VMEM((tm, tn), F32)
    p_in, p_outs, p_shapes = _quant_specs(p, (S // tm) * (nj + 1), lambda i, j: i * (nj + 1) + j)
    proj, *quant = pl.pallas_call(
        functools.partial(_in_proj_kernel, per_sec=per_sec),
        grid=(S // tm, nj + 1),
        in_specs=[pl.BlockSpec((tm, D), lambda i, j: (i, 0)),
                  pl.BlockSpec((D, tn), lambda i, j: (0, src(j))),
                  tspec, tspec, tspec, p_in],
        out_specs=[pl.BlockSpec((tm, tn), lambda i, j: (i, jnp.maximum(j - 1, 0)))] + p_outs,
        out_shape=[jax.ShapeDtypeStruct((S, 4 * sec_width), BF16)] + p_shapes,
        scratch_shapes=[acc, acc],
        compiler_params=_params(("arbitrary", "arbitrary"), 56),
        name="in_proj",
    )(h, w_in, *tables, p)
    return proj, quant


def _v_proj_t_kernel(w_ref, h_ref, p_ref, o_ref, pq_ref, ps_ref, pn_ref, wt_ref):
    @pl.when(pl.program_id(1) == 0)
    def _():
        wt_ref[...] = w_ref[...].astype(wt_ref.dtype).T

    o_ref[...] = lax.dot_general(wt_ref[...], h_ref[...], NT_DIMS, preferred_element_type=F32).astype(o_ref.dtype)
    _quant_block(p_ref, pq_ref, ps_ref, pn_ref)


def v_proj_t(h, w_in, p):
    S, D = h.shape
    sec_width = w_in.shape[1] // 6
    tn = _tile(sec_width, 512)
    ts = _tile(S, 1024)
    per_sec = sec_width // tn
    src = lambda j: j + jnp.where(j >= per_sec, 4 * per_sec, 2 * per_sec)
    ni = S // ts
    p_in, p_outs, p_shapes = _quant_specs(p, 2 * per_sec * ni, lambda j, i: j * ni + i)
    vt, *quant = pl.pallas_call(
        _v_proj_t_kernel,
        grid=(2 * per_sec, ni),
        in_specs=[pl.BlockSpec((D, tn), lambda j, i: (0, src(j))), pl.BlockSpec((ts, D), lambda j, i: (i, 0)), p_in],
        out_specs=[pl.BlockSpec((tn, ts), lambda j, i: (j, i))] + p_outs,
        out_shape=[jax.ShapeDtypeStruct((2 * sec_width, S), BF16)] + p_shapes,
        scratch_shapes=[pltpu.VMEM((tn, D), BF16)],
        compiler_params=_params(("arbitrary", "arbitrary"), 56),
        name="v_proj_t",
    )(w_in, h, p)
    return vt, quant


def _diff_attn_kernel(lam_ref, q_ref, k_ref, vt_ref, g_ref, o_ref,
                      m1, l1, a1, m2, l2, a2, sa_ref, sb_ref, *, tq, lam_init):
    i = pl.program_id(1)
    stats = ((m1, l1, a1), (m2, l2, a2))
    for m, l, a in stats:
        m[...] = jnp.full(m.shape, -jnp.inf, F32)
        l[...] = jnp.zeros(l.shape, F32)
        a[...] = jnp.zeros(a.shape, F32)

    def scores(kv, s_ref):
        k0 = pl.multiple_of(kv * tq, tq)
        for c in range(2):
            sl = slice(c * HEAD_DIM, (c + 1) * HEAD_DIM)
            s_ref[c] = lax.dot_general(k_ref[pl.ds(k0, tq), sl], q_ref[:, sl], NT_DIMS,
                                       preferred_element_type=F32)

    def softmax_pv(kv, s_ref, masked):
        k0 = pl.multiple_of(kv * tq, tq)
        vt = vt_ref[:, pl.ds(k0, tq)]
        for c, (m, l, a) in enumerate(stats):
            s = s_ref[c]
            if masked:
                keep = (lax.broadcasted_iota(jnp.int32, (tq, tq), 0) <= lax.broadcasted_iota(jnp.int32, (tq, tq), 1))
                s = jnp.where(keep, s, NEG_INF)
            m_prev = m[...]
            m_new = jnp.maximum(m_prev, jnp.max(s, axis=0, keepdims=True))
            alpha = jnp.exp2(m_prev - m_new)
            p = jnp.exp2(s - m_new)
            l[...] = alpha * l[...] + jnp.sum(p, axis=0, keepdims=True)
            a[...] = alpha * a[...] + jnp.dot(vt, p.astype(vt.dtype), preferred_element_type=F32)
            m[...] = m_new

    scores(0, sa_ref)

    def pair(n, carry):
        scores(2 * n + 1, sb_ref)
        softmax_pv(2 * n, sa_ref, False)
        scores(2 * n + 2, sa_ref)
        softmax_pv(2 * n + 1, sb_ref, False)
        return carry

    lax.fori_loop(0, i // 2, pair, 0)

    @pl.when(i % 2 == 0)
    def _():
        softmax_pv(i, sa_ref, True)

    @pl.when(i % 2 == 1)
    def _():
        scores(i, sb_ref)
        softmax_pv(i - 1, sa_ref, False)
        softmax_pv(i, sb_ref, True)

    lp = lam_ref[...]
    lam = (jnp.exp(jnp.sum(lp[0:1] * lp[1:2], axis=-1, keepdims=True))
           - jnp.exp(jnp.sum(lp[2:3] * lp[3:4], axis=-1, keepdims=True)) + lam_init)
    o = (a1[...] / l1[...] - lam * (a2[...] / l2[...])).T
    y = o * lax.rsqrt(jnp.mean(o * o, axis=-1, keepdims=True) + NORM_EPS) * g_ref[...]
    o_ref[...] = (y * (1 - lam_init)).astype(o_ref.dtype)


def diff_attn(proj, v_t, lam_rows, subln_g, n_heads, lam_init):
    S = proj.shape[0]
    tq = _tile(S, 512)
    hw = 2 * HEAD_DIM
    return pl.pallas_call(
        functools.partial(_diff_attn_kernel, tq=tq, lam_init=lam_init),
        grid=(n_heads, S // tq),
        in_specs=[pl.BlockSpec((4, HEAD_DIM), lambda h, i: (0, 0)),
                  pl.BlockSpec((tq, hw), lambda h, i: (i, h)),
                  pl.BlockSpec((S, hw), lambda h, i: (0, n_heads + h)),
                  pl.BlockSpec((hw, S), lambda h, i: (h, 0)),
                  pl.BlockSpec((1, hw), lambda h, i: (0, 0))],
        out_specs=pl.BlockSpec((tq, hw), lambda h, i: (i, h)),
        out_shape=jax.ShapeDtypeStruct((S, n_heads * hw), BF16),
        scratch_shapes=[pltpu.VMEM((1, tq), F32), pltpu.VMEM((1, tq), F32), pltpu.VMEM((hw, tq), F32),
                        pltpu.VMEM((1, tq), F32), pltpu.VMEM((1, tq), F32), pltpu.VMEM((hw, tq), F32),
                        pltpu.VMEM((2, tq, tq), F32), pltpu.VMEM((2, tq, tq), F32)],
        compiler_params=_params(("arbitrary", "arbitrary"), 48),
        name="diff_attn",
    )(lam_rows, proj, proj, v_t, subln_g.reshape(1, hw))


def _dil_bias(delta):
    count = jnp.zeros(delta.shape, F32)
    for window, dilation in DILATED_CONFIGS:
        ok = (delta >= 0) & (delta <= window) & ((delta & (dilation - 1)) == 0)
        count = count + jnp.where(ok, 1.0, 0.0)
    return jnp.where(count > 0, jnp.log2(jnp.maximum(count, 1.0)), NEG_INF)


DIL_HEADS_PER_STEP = 2


def _dil_attn_kernel(q_ref, k_ref, vt_ref, g_ref, o_ref, bias_ref, sa_ref, sb_ref, *, tq, win):
    i = pl.program_id(1)
    n_blocks = pl.num_programs(1)
    heads = o_ref.shape[1] // HEAD_DIM

    @pl.when((pl.program_id(0) == 0) & (i == 0))
    def _():
        kj = lax.broadcasted_iota(jnp.int32, bias_ref.shape, 0)
        qi = lax.broadcasted_iota(jnp.int32, bias_ref.shape, 1)
        bias_ref[...] = _dil_bias(qi + DIL_BACK - kj)

    def window(j):
        start = pl.multiple_of(jnp.maximum(j * tq - DIL_BACK, 0), tq)
        row0 = pl.multiple_of(jnp.maximum(DIL_BACK - j * tq, 0), tq)
        return start, row0

    def scores(j, s_ref):
        start, row0 = window(j)
        bias = bias_ref[pl.ds(row0, win), :]
        q0 = pl.multiple_of(j * tq, tq)
        for hd in range(heads):
            sl = slice(hd * HEAD_DIM, (hd + 1) * HEAD_DIM)
            s_ref[hd] = lax.dot_general(k_ref[pl.ds(start, win), sl], q_ref[pl.ds(q0, tq), sl], NT_DIMS,
                                        preferred_element_type=F32) + bias

    def softmax_pv(j, s_ref):
        start, _ = window(j)
        for hd in range(heads):
            sl = slice(hd * HEAD_DIM, (hd + 1) * HEAD_DIM)
            s = s_ref[hd]
            p = jnp.exp2(s - jnp.max(s, axis=0, keepdims=True))
            vtw = vt_ref[sl, pl.ds(start, win)]
            o_t = jnp.dot(vtw, p.astype(vtw.dtype), preferred_element_type=F32) / jnp.sum(p, axis=0, keepdims=True)
            o = o_t.T
            y = o * lax.rsqrt(jnp.mean(o * o, axis=-1, keepdims=True) + NORM_EPS) * g_ref[:, sl]
            o_ref[:, sl] = y.astype(o_ref.dtype)

    @pl.when(i == 0)
    def _():
        scores(0, sa_ref)

    nxt = jnp.minimum(i + 1, n_blocks - 1)

    @pl.when(i % 2 == 0)
    def _():
        scores(nxt, sb_ref)
        softmax_pv(i, sa_ref)

    @pl.when(i % 2 == 1)
    def _():
        scores(nxt, sa_ref)
        softmax_pv(i, sb_ref)


def dil_attn(proj, v_t, out_g, n_heads, col0, row0):
    S = proj.shape[0]
    for window, dilation in DILATED_CONFIGS:
        assert window % dilation == 0 and dilation & (dilation - 1) == 0
    tq = _tile(S, 256)
    win = DIL_BACK + tq
    hp = DIL_HEADS_PER_STEP
    hw = hp * HEAD_DIM
    assert S >= win and DIL_BACK % tq == 0 and win % LANES == 0
    assert n_heads % hp == 0 and col0 % hw == 0 and row0 % hw == 0
    c0 = col0 // hw
    r0 = row0 // hw
    ng = n_heads // hp
    return pl.pallas_call(
        functools.partial(_dil_attn_kernel, tq=tq, win=win),
        grid=(ng, S // tq),
        in_specs=[pl.BlockSpec((S, hw), lambda h, i: (0, c0 + h)),
                  pl.BlockSpec((S, hw), lambda h, i: (0, c0 + ng + h)),
                  pl.BlockSpec((hw, S), lambda h, i: (r0 + h, 0)),
                  pl.BlockSpec((1, hw), lambda h, i: (0, h))],
        out_specs=pl.BlockSpec((tq, hw), lambda h, i: (i, h)),
        out_shape=jax.ShapeDtypeStruct((S, n_heads * HEAD_DIM), BF16),
        scratch_shapes=[pltpu.VMEM((win + DIL_BACK, tq), F32),
                        pltpu.VMEM((hp, win, tq), F32), pltpu.VMEM((hp, win, tq), F32)],
        compiler_params=_params(("arbitrary", "arbitrary"), 56),
        name="dil_attn",
    )(proj, proj, v_t, out_g.reshape(1, n_heads * HEAD_DIM))


def _out_proj_kernel(ya_ref, yb_ref, wa_ref, wb_ref, x_ref, gate_ref, x1_ref):
    acc = (jnp.dot(ya_ref[...], wa_ref[...].astype(ya_ref.dtype), preferred_element_type=F32)
           + jnp.dot(yb_ref[...], wb_ref[...].astype(yb_ref.dtype), preferred_element_type=F32))
    x1_ref[...] = x_ref[...] + gate_ref[...] * acc


def out_proj(ya, yb, w_out, x, mod, gate_idx):
    S, D = x.shape
    kh = ya.shape[1]
    assert yb.shape[1] == kh and w_out.shape[0] == 2 * kh
    tm = _tile(S, 1024)
    tn = _tile(D, 512)
    nj = D // tn
    return pl.pallas_call(
        _out_proj_kernel,
        grid=(S // tm, nj),
        in_specs=[pl.BlockSpec((tm, kh), lambda i, j: (i, 0)),
                  pl.BlockSpec((tm, kh), lambda i, j: (i, 0)),
                  pl.BlockSpec((kh, tn), lambda i, j: (0, j)),
                  pl.BlockSpec((kh, tn), lambda i, j: (1, j)),
                  pl.BlockSpec((tm, tn), lambda i, j: (i, j)),
                  pl.BlockSpec((1, tn), lambda i, j: (0, gate_idx * nj + j))],
        out_specs=pl.BlockSpec((tm, tn), lambda i, j: (i, j)),
        out_shape=jax.ShapeDtypeStruct((S, D), F32),
        compiler_params=_params(("arbitrary", "arbitrary"), 48),
        name="out_proj",
    )(ya, yb, w_out, w_out, x, mod)


def _top16(x, tie_safe):
    rows = lax.broadcasted_iota(jnp.int32, x.shape, 0)
    rank = jnp.full(x.shape, PEER_TOPK, jnp.int32)
    vals = []
    for r in range(PEER_TOPK):
        m = jnp.max(x, axis=0, keepdims=True)
        if tie_safe:
            hit = rows == jnp.min(jnp.where(x == m, rows, x.shape[0]), axis=0, keepdims=True)
        else:
            hit = x == m
        rank = jnp.where(hit, r, rank)
        x = jnp.where(hit, -jnp.inf, x)
        vals.append(m)
    return jnp.concatenate(vals, axis=0), rank


def _route(s, tie_safe):
    a, rank0 = _top16(s[0], tie_safe)
    b, rank1 = _top16(s[1], tie_safe)
    n_q = [PEER_TOPK // (p + 1) for p in range(PEER_TOPK)]
    pad = -sum(n_q) % 8
    cand = jnp.concatenate([a[p:p + 1] + b[:n_q[p]] for p in range(PEER_TOPK)]
                           + [jnp.full((pad, a.shape[1]), -jnp.inf, F32)], axis=0)
    fin, sel_rank = _top16(cand, tie_safe)
    sel = jnp.where(sel_rank < PEER_TOPK, 1.0, 0.0)
    starts = [sum(n_q[:p]) for p in range(PEER_TOPK)]
    cnt = jnp.concatenate([jnp.sum(sel[starts[p]:starts[p] + n_q[p]], axis=0, keepdims=True)
                           for p in range(PEER_TOPK)], axis=0)
    cnt0 = jnp.zeros(s[0].shape, F32)
    for p in range(PEER_TOPK):
        cnt0 = jnp.where(rank0 == p, cnt[p:p + 1], cnt0)
    z = jnp.sum(jnp.exp(fin - fin[0:1]), axis=0, keepdims=True)
    removed = sum(jnp.sum(jnp.where(r < PEER_TOPK, 1.0, 0.0), axis=0, keepdims=True)
                  for r in (rank0, rank1, sel_rank))
    return rank1.astype(F32), cnt0, jnp.exp(s[0] - a[0:1]), jnp.exp(s[1] - b[0:1]) / z, removed


def _peer_router_kernel(h_ref, wq_ref, sk_ref, rank1_ref, cnt0_ref, e0_ref, e1_ref):
    q = jnp.dot(h_ref[...], wq_ref[...].astype(h_ref.dtype), preferred_element_type=F32)
    half = q.shape[1] // 2
    s = [lax.dot_general(sk_ref[0, p], q[:, p * half:(p + 1) * half], NT_DIMS,
                         precision=lax.Precision.HIGHEST, preferred_element_type=F32)
         for p in range(2)]

    def emit(tie_safe):
        *outs, removed = _route(s, tie_safe)
        for ref, val in zip((rank1_ref, cnt0_ref, e0_ref, e1_ref), outs):
            ref[0] = val.astype(ref.dtype)
        return removed

    removed = emit(False)
    tie = jnp.max(jnp.abs(removed - 3.0 * PEER_TOPK)) > 0.0

    @pl.when(tie)
    def _():
        emit(True)


def peer_router(h, w_q, subkeys):
    S, D = h.shape
    n_heads = subkeys.shape[0]
    qd = w_q.shape[1] // n_heads
    T = _tile(S, 512)
    ospec = pl.BlockSpec((1, N_KEYS, T), lambda t, hd: (hd, 0, t))
    return pl.pallas_call(
        _peer_router_kernel,
        grid=(S // T, n_heads),
        in_specs=[pl.BlockSpec((T, D), lambda t, hd: (t, 0)),
                  pl.BlockSpec((D, qd), lambda t, hd: (0, hd)),
                  pl.BlockSpec((1, 2, N_KEYS, qd // 2), lambda t, hd: (hd, 0, 0, 0))],
        out_specs=[ospec] * 4,
        out_shape=[jax.ShapeDtypeStruct((n_heads, N_KEYS, S), BF16)] * 4,
        compiler_params=_params(("arbitrary", "arbitrary"), 48),
        name="peer_router",
    )(h, w_q, subkeys)


def _gelu(x):
    return 0.5 * x * (1.0 + lax.erf(x * (0.5 ** 0.5)))


PEER_GI = 8
PEER_GJ = 64
PEER_CHUNKS = 2


def _peer_expert_kernel(h_ref, hs_ref, wi_ref, u_ref, us_ref, v_ref, vs_ref, rank1_ref, cnt0_ref, e0_ref, e1_ref,
                        x_ref, ws_ref, gate_ref, fg_ref, o_ref, aa_ref, ab_ref, w8a_ref, w8b_ref, *, nj, normalize):
    e = pl.program_id(1)
    ne = pl.num_programs(1) - 2
    gi, gj, D = u_ref.shape
    n_heads, _, T = rank1_ref.shape
    eb = gi * gj
    kc = D // PEER_CHUNKS
    lanes_per_piece = T // PEER_CHUNKS
    assert lanes_per_piece % LANES == 0
    ew = jnp.clip(e - 1, 0, ne - 1)
    i0 = pl.multiple_of((ew // nj) * gi, gi)
    j0 = pl.multiple_of((ew % nj) * gj, gj)

    @pl.when(e == 0)
    def _():
        o_ref[...] = jnp.zeros(o_ref.shape, F32)
        ab_ref[...] = jnp.zeros(ab_ref.shape, F32)
        w8a_ref[...] = jnp.zeros(w8a_ref.shape, w8a_ref.dtype)

    def step(a_next, a_cur, w8_cur, w8_prev):
        u_scale = us_ref[...].reshape(eb, LANES)
        v_scale = vs_ref[...].reshape(eb, LANES)

        def piece(c, carry):
            cols = pl.ds(pl.multiple_of(c * kc, kc), kc)
            part = lax.dot_general(u_ref[:, :, cols].reshape(eb, kc), h_ref[:, cols], NT_DIMS,
                                   preferred_element_type=F32)
            a_next[...] = jnp.where(c == 0, part, a_next[...] + part)
            y = lax.dot_general(w8_prev[...], v_ref[:, :, cols].reshape(eb, kc), TN_DIMS,
                                preferred_element_type=F32)
            o_ref[:, cols] += y
            for sub in range(lanes_per_piece // LANES):
                lsl = pl.ds(pl.multiple_of(c * lanes_per_piece + sub * LANES, LANES), LANES)
                act = _gelu(a_cur[:, lsl] * (u_scale * hs_ref[:, lsl])) * (v_scale * wi_ref[:, lsl])
                c0 = [cnt0_ref[hd, pl.ds(i0, gi), lsl] for hd in range(n_heads)]
                e0 = [e0_ref[hd, pl.ds(i0, gi), lsl] for hd in range(n_heads)]
                for ii in range(gi):
                    rsl = slice(ii * gj, (ii + 1) * gj)
                    g = jnp.zeros((gj, LANES), BF16)
                    for hd in range(n_heads):
                        r1 = rank1_ref[hd, pl.ds(j0, gj), lsl]
                        e1 = e1_ref[hd, pl.ds(j0, gj), lsl]
                        g = g + jnp.where(r1 < c0[hd][ii:ii + 1], e1 * e0[hd][ii:ii + 1], jnp.zeros((), BF16))
                    w8_cur[rsl, lsl] = (g.astype(F32) * act[rsl]).astype(w8_cur.dtype)
            return carry

        lax.fori_loop(0, PEER_CHUNKS, piece, 0)

    pl.when(e % 2 == 0)(lambda: step(aa_ref, ab_ref, w8b_ref, w8a_ref))
    pl.when(e % 2 == 1)(lambda: step(ab_ref, aa_ref, w8a_ref, w8b_ref))

    @pl.when(e == pl.num_programs(1) - 1)
    def _():
        x = x_ref[...] + gate_ref[...] * (o_ref[...] * ws_ref[...])
        if normalize:
            x = x * lax.rsqrt(jnp.mean(x * x, axis=-1, keepdims=True) + NORM_EPS) * fg_ref[...]
        o_ref[...] = x


PEER_W_MARGIN = 1.25


def peer_experts(h8, h_scale, h_norm, u_quant, v_quant, tables, x, mod, gate_idx, final_g, normalize):
    S, D = h8.shape
    n_heads = tables[0].shape[0]
    T = _tile(S, 512)
    u8, su, u_norm = u_quant
    v8, sv, _ = v_quant
    w_bound = (PEER_W_MARGIN * n_heads) * h_norm * jnp.max(u_norm * sv[:, :1])
    w_scale = jnp.maximum(w_bound, FP8_TINY) / FP8_MAX
    u3 = u8.reshape(N_KEYS, N_KEYS, D)
    v3 = v8.reshape(N_KEYS, N_KEYS, D)
    su3 = su.reshape(N_KEYS, N_KEYS, LANES)
    sv3 = sv.reshape(N_KEYS, N_KEYS, LANES)
    ni, nj = N_KEYS // PEER_GI, N_KEYS // PEER_GJ
    ne = ni * nj
    eb = PEER_GI * PEER_GJ

    def u_idx(t, e):
        eu = jnp.minimum(e, ne - 1)
        return eu // nj, eu % nj, 0

    def w_idx(t, e):
        ew = jnp.clip(e - 1, 0, ne - 1)
        return ew // nj, ew % nj, 0

    def v_idx(t, e):
        ev = jnp.maximum(e - 2, 0)
        return ev // nj, ev % nj, 0

    once = pl.Buffered(1)
    tspec = pl.BlockSpec((n_heads, N_KEYS, T), lambda t, e: (0, 0, t), pipeline_mode=once)
    a_buf = pltpu.VMEM((eb, T), F32)
    w8_buf = pltpu.VMEM((eb, T), F8)
    row = pl.BlockSpec((1, T), lambda t, e: (0, t))
    return pl.pallas_call(
        functools.partial(_peer_expert_kernel, nj=nj, normalize=normalize),
        grid=(S // T, ne + 2),
        in_specs=[pl.BlockSpec((T, D), lambda t, e: (t, 0), pipeline_mode=once),
                  row, row,
                  pl.BlockSpec((PEER_GI, PEER_GJ, D), u_idx),
                  pl.BlockSpec((PEER_GI, PEER_GJ, LANES), w_idx),
                  pl.BlockSpec((PEER_GI, PEER_GJ, D), v_idx),
                  pl.BlockSpec((PEER_GI, PEER_GJ, LANES), w_idx),
                  tspec, tspec, tspec, tspec,
                  pl.BlockSpec((T, D), lambda t, e: (t, 0), pipeline_mode=once),
                  pl.BlockSpec((T, 1), lambda t, e: (t, 0), pipeline_mode=once),
                  pl.BlockSpec((1, D), lambda t, e: (0, gate_idx)),
                  pl.BlockSpec((1, D), lambda t, e: (0, 0))],
        out_specs=pl.BlockSpec((T, D), lambda t, e: (t, 0)),
        out_shape=jax.ShapeDtypeStruct((S, D), F32),
        scratch_shapes=[a_buf, a_buf, w8_buf, w8_buf],
        compiler_params=_params(("arbitrary", "arbitrary"), 56),
        name="peer_experts",
    )(h8, h_scale.reshape(1, S), (1.0 / w_scale).reshape(1, S), u3, su3, v3, sv3, *tables,
      x, w_scale, mod, final_g.reshape(1, D))


def kernel(x, c, positions, norm1_g, norm2_g, w_ada, b_ada, w_in, lam_q1, lam_k1, lam_q2, lam_k2,
           diff_subln_g, dil_out_g, w_out, peer_wq, peer_subkeys, peer_u, peer_v, final_g):
    B, S, D = x.shape
    depth = w_ada.shape[0]
    assert B == 1, "one sequence per call"
    diff_width = D // 2
    diff_heads = diff_width // (2 * HEAD_DIM)
    dil_heads = (D - diff_width) // HEAD_DIM
    xs = x.reshape(S, D)
    tables = rope_tables(positions.reshape(S))
    for l in range(depth):
        lam_init = 0.8 - 0.6 * math.exp(-0.3 * l)
        mod = ada_mod(c, w_ada[l], b_ada[l])
        h = norm_mod(xs, norm1_g[l], mod, 0, 1)
        proj, v_quant = in_proj(h, w_in[l], tables, peer_v[l])
        v_t, u_quant = v_proj_t(h, w_in[l], peer_u[l])
        lam_rows = jnp.stack([lam_q1[l], lam_k1[l], lam_q2[l], lam_k2[l]])
        y_diff = diff_attn(proj, v_t, lam_rows, diff_subln_g[l], diff_heads, lam_init)
        y_dil = dil_attn(proj, v_t, dil_out_g[l], dil_heads, 2 * diff_width, diff_width)
        xs = out_proj(y_diff, y_dil, w_out[l], xs, mod, 2)
        h, h8, h_scale, h_norm = norm_mod_q(xs, norm2_g[l], mod, 3, 4)
        gates = peer_router(h, peer_wq[l], peer_subkeys[l])
        xs = peer_experts(h8, h_scale, h_norm, u_quant, v_quant, gates, xs, mod, 5, final_g,
                          normalize=(l == depth - 1))
    return xs.reshape(B, S, D)
```
